```python
import jax, jax.numpy as jnp
from jax import lax
import numpy as np

D_MODEL = 1024
BATCH = 8
SEQ = 2048
DEPTH = 1

HG_HEADS = 8
HG_KEY = 128
HG_VAL = D_MODEL // HG_HEADS
HG_FDIM = HG_HEADS * HG_KEY
HG_VDIM = HG_HEADS * HG_VAL
HG_CHUNK = 64
ATT_GROUPS = ((128, 1), (512, 4), (2048, 16))
N_GROUPS = 3
ATT_HEADS = 8
ATT_HEAD_DIM = 64
ATT_DIM = ATT_HEADS * ATT_HEAD_DIM
ROPE_THETA = 10000.0
N_BRANCH = 2
NORM_EPS = 1e-6
IN_SIZES = (HG_FDIM, HG_FDIM, HG_VDIM, HG_VDIM, 3 * N_GROUPS * ATT_DIM, ATT_DIM, N_BRANCH * D_MODEL)
IN_COLS = HG_FDIM * 2 + HG_VDIM * 2 + 3 * N_GROUPS * ATT_DIM + ATT_DIM + N_BRANCH * D_MODEL
IN_SPLITS = (HG_FDIM, 2 * HG_FDIM, 2 * HG_FDIM + HG_VDIM, 2 * HG_FDIM + 2 * HG_VDIM,
             2 * HG_FDIM + 2 * HG_VDIM + 3 * N_GROUPS * ATT_DIM,
             2 * HG_FDIM + 2 * HG_VDIM + 3 * N_GROUPS * ATT_DIM + ATT_DIM)

kernel_name = "hybrid_hgrn2_dilated_attn_gated_merge"


def rmsnorm(x, w):
    xf = x.astype(jnp.float32)
    y = xf * lax.rsqrt(jnp.mean(xf * xf, axis=-1, keepdims=True) + NORM_EPS)
    return (y * w.astype(jnp.float32)).astype(x.dtype)


def rope_tables(positions, dim):
    inv_freq = ROPE_THETA ** (-jnp.arange(0, dim, 2, dtype=jnp.float32) / dim)
    ang = positions.astype(jnp.float32)[..., None] * inv_freq
    return jnp.cos(ang), jnp.sin(ang)


def apply_rope(t, cos, sin):
    c = cos[:, :, None, None, :]
    s = sin[:, :, None, None, :]
    t1, t2 = jnp.split(t, 2, axis=-1)
    return jnp.concatenate([t1 * c - t2 * s, t2 * c + t1 * s], axis=-1)


def hgrn2_chunked(q, k, logf, v):
    B, S, H, K = q.shape
    V = v.shape[-1]
    C = HG_CHUNK
    nC = S // C

    def chunks(t):
        return t.reshape(B, nC, C, H, t.shape[-1]).transpose(1, 0, 3, 2, 4)

    causal = jnp.tril(jnp.ones((C, C), dtype=bool))

    def step(state, xs):
        qc, kc, gc, vc = xs
        b = jnp.cumsum(gc, axis=2)
        diff = b[:, :, :, None, :] - b[:, :, None, :, :]
        decay = jnp.exp(jnp.where(causal[:, :, None], diff, -jnp.inf))
        scores = jnp.einsum('bhtk,bhtsk,bhsk->bhts', qc, decay, kc)
        o = (jnp.einsum('bhts,bhsv->bhtv', scores, vc)
             + jnp.einsum('bhtk,bhkv->bhtv', qc * jnp.exp(b), state))
        b_last = b[:, :, -1:, :]
        state = (jnp.exp(b_last)[:, :, 0, :, None] * state
                 + jnp.einsum('bhsk,bhsv->bhkv', kc * jnp.exp(b_last - b), vc))
        return state, o

    s0 = jnp.zeros((B, H, K, V), jnp.float32)
    _, o = lax.scan(step, s0, (chunks(q), chunks(k), chunks(logf), chunks(v)))
    return o.transpose(1, 0, 3, 2, 4).reshape(B, S, H, V)


def dilated_window_attention(q, k, v, window, dilation):
    B, S, H, E = q.shape
    n = window // dilation
    M = S // dilation
    nblk = -(-M // n)
    Mp = nblk * n

    def to_residue(t):
        t = t.reshape(B, M, dilation, H, E).transpose(0, 2, 3, 1, 4)
        return jnp.pad(t, ((0, 0), (0, 0), (0, 0), (0, Mp - M), (0, 0)))

    def key_blocks(t):
        t = jnp.pad(t, ((0, 0), (0, 0), (0, 0), (n, 0), (0, 0))).reshape(B, dilation, H, nblk + 1, n, E)
        return jnp.concatenate([t[:, :, :, :-1], t[:, :, :, 1:]], axis=-2)

    qb = to_residue(q).reshape(B, dilation, H, nblk, n, E)
    kb = key_blocks(to_residue(k))
    vb = key_blocks(to_residue(v))
    s = jnp.einsum('brhcqe,brhcke->brhcqk', qb, kb) * (E ** -0.5)
    i = jnp.arange(n)[:, None]
    j = jnp.arange(2 * n)[None, :]
    dist = i + n - j
    c = jnp.arange(nblk)[:, None, None]
    valid = (dist >= 0) & (dist <= n) & (c * n - n + j >= 0)
    s = jnp.where(valid, s, -jnp.inf)
    m = jnp.max(s, axis=-1)
    p = jnp.exp(s - m[..., None])
    l = jnp.sum(p, axis=-1)
    o = jnp.einsum('brhcqk,brhcke->brhcqe', p, vb)

    def back(t):
        rest = t.shape[5:]
        t = t.reshape(B, dilation, H, Mp, *rest)[:, :, :, :M]
        t = jnp.moveaxis(t, 3, 1)
        return t.reshape(B, S, H, *rest)

    return back(o), back(m), back(l)


def hybrid_layer(x, cos, sin, norm_w, w_in, lb, hgrn_norm_w, w_branch_a, w_branch_b, w_out):
    B, S, _ = x.shape
    h = rmsnorm(x, norm_w)
    z = h @ w_in
    hq, hf, hi, hg, aqkv, ag, gates = jnp.split(z, IN_SPLITS, axis=-1)

    f = lb + (1.0 - lb) * jax.nn.sigmoid(hf.astype(jnp.float32))
    shp = (B, S, HG_HEADS, HG_KEY)
    q_a = jax.nn.silu(hq.astype(jnp.float32)).reshape(shp)
    o_a = hgrn2_chunked(q_a, (1.0 - f).reshape(shp), jnp.log(f).reshape(shp),
                        hi.astype(jnp.float32).reshape(B, S, HG_HEADS, HG_VAL))
    o_a = rmsnorm(o_a, hgrn_norm_w) * jax.nn.silu(hg.astype(jnp.float32)).reshape(B, S, HG_HEADS, HG_VAL)
    y_a = o_a.reshape(B, S, HG_VDIM).astype(x.dtype) @ w_branch_a

    aqkv = aqkv.astype(jnp.float32).reshape(B, S, 3, N_GROUPS, ATT_HEADS, ATT_HEAD_DIM)
    q_b = apply_rope(aqkv[:, :, 0], cos, sin)
    k_b = apply_rope(aqkv[:, :, 1], cos, sin)
    v_b = aqkv[:, :, 2]
    outs, maxes, dens = [], [], []
    for g, (window, dilation) in enumerate(ATT_GROUPS):
        o_g, m_g, l_g = dilated_window_attention(q_b[:, :, g], k_b[:, :, g], v_b[:, :, g], window, dilation)
        outs.append(o_g)
        maxes.append(m_g)
        dens.append(l_g)
    ms = jnp.stack(maxes)
    wts = jnp.exp(ms - jnp.max(ms, axis=0))
    den = jnp.sum(wts * jnp.stack(dens), axis=0)
    num = jnp.sum(wts[..., None] * jnp.stack(outs), axis=0)
    o_b = (num / den[..., None]).reshape(B, S, ATT_DIM) * jax.nn.silu(ag.astype(jnp.float32))
    y_b = o_b.astype(x.dtype) @ w_branch_b

    g_a, g_b = jnp.split(jax.nn.sigmoid(gates), N_BRANCH, axis=-1)
    merged = g_a * y_a + g_b * y_b
    return x + merged @ w_out


def setup_inputs(seed: int = 0) -> dict:
    key = jax.random.key(seed)
    ks = jax.random.split(key, 12)
    x = jax.random.normal(ks[0], (BATCH, SEQ, D_MODEL), jnp.float32)
    offsets = jax.random.randint(ks[1], (BATCH, 1), 0, 4096, dtype=jnp.int32)
    positions = offsets + jnp.arange(SEQ, dtype=jnp.int32)[None, :]
    norm_w = 1.0 + 0.02 * jax.random.normal(ks[2], (DEPTH, D_MODEL), jnp.float32)
    w_in = jax.random.normal(ks[3], (DEPTH, D_MODEL, IN_COLS), jnp.float32) * D_MODEL ** -0.5
    lb_logits = 0.5 * jax.random.normal(ks[4], (DEPTH + 1, HG_FDIM), jnp.float32)
    hgrn_norm_w = 1.0 + 0.02 * jax.random.normal(ks[5], (DEPTH, HG_VAL), jnp.float32)
    w_branch_a = jax.random.normal(ks[6], (DEPTH, HG_VDIM, D_MODEL), jnp.float32) * HG_VDIM ** -0.5
    w_branch_b = jax.random.normal(ks[7], (DEPTH, ATT_DIM, D_MODEL), jnp.float32) * ATT_DIM ** -0.5
    w_out = jax.random.normal(ks[8], (DEPTH, D_MODEL, D_MODEL), jnp.float32) * D_MODEL ** -0.5
    final_norm_w = 1.0 + 0.02 * jax.random.normal(ks[9], (D_MODEL,), jnp.float32)
    return {"x": x, "positions": positions, "norm_w": norm_w, "w_in": w_in, "lb_logits": lb_logits,
            "hgrn_norm_w": hgrn_norm_w, "w_branch_a": w_branch_a, "w_branch_b": w_branch_b,
            "w_out": w_out, "final_norm_w": final_norm_w}


def reference(x, positions, norm_w, w_in, lb_logits, hgrn_norm_w, w_branch_a, w_branch_b, w_out, final_norm_w):
    cos, sin = rope_tables(positions, ATT_HEAD_DIM)
    lower_bounds = jnp.cumsum(jax.nn.softmax(lb_logits.astype(jnp.float32), axis=0), axis=0)
    for layer in range(DEPTH):
        x = hybrid_layer(x, cos, sin, norm_w[layer], w_in[layer], lower_bounds[layer], hgrn_norm_w[layer],
                         w_branch_a[layer], w_branch_b[layer], w_out[layer])
    return rmsnorm(x, final_norm_w)
```

```python
import functools

import numpy as np
import jax
import jax.numpy as jnp
from jax import lax
from jax.experimental import pallas as pl
from jax.experimental.pallas import tpu as pltpu

D_MODEL = 1024
HG_HEADS = 8
HG_KEY = 128
HG_VAL = 128
HG_DIM = HG_HEADS * HG_KEY
ATT_DILATIONS = (1, 4, 16)
ATT_STEPS = 128
N_GROUPS = 3
ATT_HEADS = 8
ATT_HEAD_DIM = 64
ATT_DIM = ATT_HEADS * ATT_HEAD_DIM
ROPE_THETA = 10000.0
NORM_EPS = 1e-6
C_HQ, C_HF, C_HI, C_HG = 0, 1024, 2048, 3072
C_AQ, C_AK, C_AV = 4096, 4096 + 1536, 4096 + 3072
C_AG = 8704
C_GATES = 9216
IN_COLS = 11264

LANES = 128
VMEM_LIMIT_BYTES = 56 * 1024 * 1024

IN_TM = 256
HG_CHUNK = 64
HG_TS = 256
HG_LEVELS = 6
OUT_TM = 512
NEG_BIG = -1e30

F32 = jnp.float32
BF16 = jnp.bfloat16


def _dot(a, b):
    return jnp.dot(a, b, preferred_element_type=F32)


def _dot_nt(a, b):
    return lax.dot_general(a, b, (((1,), (1,)), ((), ())), preferred_element_type=F32)


def _dot_tn(a, b):
    return lax.dot_general(a, b, (((0,), (0,)), ((), ())), preferred_element_type=F32)


def _sigmoid(z):
    return 1.0 / (1.0 + jnp.exp(-z))


def _inproj_kernel(x_ref, pos_ref, nw_ref, w_ref, lbl_ref, invf_ref,
                   qa_ref, ka_ref, lf_ref, va_ref, ga_ref,
                   q1_ref, k1_ref, v1_ref, q2_ref, k2_ref, v2_ref, q3_ref, k3_ref, v3_ref,
                   ag_ref, gates_ref):
    x = x_ref[...]
    ms = jnp.mean(x * x, axis=-1, keepdims=True)
    h = (x * lax.rsqrt(ms + NORM_EPS) * nw_ref[...]).astype(BF16)

    def proj(c0, n):
        return _dot(h, w_ref[:, c0:c0 + n])

    lbl = lbl_ref[...]
    lmax = jnp.max(lbl, axis=0, keepdims=True)
    lexp = jnp.exp(lbl - lmax)
    lb = lexp[0:1, :] / jnp.sum(lexp, axis=0, keepdims=True)

    half = 512
    for j in range(2):
        sl = slice(j * half, (j + 1) * half)
        z = proj(C_HQ + j * half, half)
        qa_ref[:, sl] = (z * _sigmoid(z)).astype(BF16)
        z = proj(C_HF + j * half, half)
        lbj = lb[:, sl]
        f = lbj + (1.0 - lbj) * _sigmoid(z)
        lf_ref[:, sl] = jnp.log(f)
        ka_ref[:, sl] = ((1.0 - lbj) * _sigmoid(-z)).astype(BF16)
        z = proj(C_HI + j * half, half)
        va_ref[:, sl] = z.astype(BF16)
        z = proj(C_HG + j * half, half)
        ga_ref[:, sl] = (z * _sigmoid(z)).astype(BF16)

    ang = pos_ref[...].astype(F32) * invf_ref[...]
    cos = jnp.cos(ang)
    sin = jnp.sin(ang)
    lane = lax.broadcasted_iota(jnp.int32, ang.shape, 1)
    first = (lane & (ATT_HEAD_DIM - 1)) < (ATT_HEAD_DIM // 2)
    sin_signed = jnp.where(first, -sin, sin)

    def rope(zs):
        rot = jnp.where(first, pltpu.roll(zs, LANES - ATT_HEAD_DIM // 2, 1), pltpu.roll(zs, ATT_HEAD_DIM // 2, 1))
        return zs * cos + rot * sin_signed

    scale = ATT_HEAD_DIM ** -0.5
    for g, (q_ref, k_ref, v_ref) in enumerate(((q1_ref, k1_ref, v1_ref), (q2_ref, k2_ref, v2_ref),
                                                (q3_ref, k3_ref, v3_ref))):
        zq = proj(C_AQ + g * ATT_DIM, ATT_DIM)
        zk = proj(C_AK + g * ATT_DIM, ATT_DIM)
        for s in range(ATT_DIM // LANES):
            ls = slice(s * LANES, (s + 1) * LANES)
            q_ref[:, ls] = (rope(zq[:, ls]) * scale).astype(BF16)
            k_ref[:, ls] = rope(zk[:, ls]).astype(BF16)
        v_ref[...] = proj(C_AV + g * ATT_DIM, ATT_DIM).astype(BF16)

    z = proj(C_AG, ATT_DIM)
    ag_ref[...] = (z * _sigmoid(z)).astype(BF16)
    for j in range(4):
        sl = slice(j * half, (j + 1) * half)
        gates_ref[:, sl] = _sigmoid(proj(C_GATES + j * half, half)).astype(BF16)


def _inproj(x2, pos_b, norm_w, w_in_bf, lb_logits, invf):
    T = x2.shape[0]
    tm = IN_TM
    row = lambda n: pl.BlockSpec((tm, n), lambda i: (i, 0))
    const = lambda shape: pl.BlockSpec(shape, lambda i: (0, 0))
    out_shapes = ([jax.ShapeDtypeStruct((T, HG_DIM), BF16),
                   jax.ShapeDtypeStruct((T, HG_DIM), BF16),
                   jax.ShapeDtypeStruct((T, HG_DIM), F32),
                   jax.ShapeDtypeStruct((T, HG_DIM), BF16),
                   jax.ShapeDtypeStruct((T, HG_DIM), BF16)]
                  + [jax.ShapeDtypeStruct((T, ATT_DIM), BF16)] * 9
                  + [jax.ShapeDtypeStruct((T, ATT_DIM), BF16),
                     jax.ShapeDtypeStruct((T, 2 * D_MODEL), BF16)])
    out_specs = ([row(HG_DIM)] * 5 + [row(ATT_DIM)] * 9 + [row(ATT_DIM), row(2 * D_MODEL)])
    return pl.pallas_call(
        _inproj_kernel,
        grid=(T // tm,),
        in_specs=[row(D_MODEL), row(LANES), const((1, D_MODEL)),
                  pl.BlockSpec((D_MODEL, IN_COLS), lambda i: (0, 0), pipeline_mode=pl.Buffered(1)),
                  const((2, HG_DIM)), const((1, LANES))],
        out_specs=out_specs,
        out_shape=out_shapes,
        compiler_params=pltpu.CompilerParams(dimension_semantics=("parallel",),
                                             vmem_limit_bytes=VMEM_LIMIT_BYTES),
        name="inproj",
    )(x2, pos_b, norm_w, w_in_bf, lb_logits, invf)


def _hgrn_sum_matrix():
    C, L = HG_CHUNK, HG_LEVELS
    A = np.zeros(((L + 2) * C, C), np.float32)
    for l in range(L):
        m = C >> (l + 1)
        for t in range(C):
            start = (t // m) * m
            if t & m:
                A[l * C + t, start:t + 1] = 1.0
            else:
                A[l * C + t, t + 1:start + m] = 1.0
    for t in range(C):
        A[L * C + t, :t + 1] = 1.0
        A[(L + 1) * C + t, t + 1:] = 1.0
    return A


def _hgrn_kernel(q_ref, k_ref, lf_ref, v_ref, g_ref, a_ref, nw_ref, o_ref, state_scr, e_scr):
    C, L = HG_CHUNK, HG_LEVELS

    @pl.when(pl.program_id(1) == 0)
    def _():
        state_scr[...] = jnp.zeros_like(state_scr)

    t_idx = lax.broadcasted_iota(jnp.int32, (C, C), 0)
    s_idx = lax.broadcasted_iota(jnp.int32, (C, C), 1)
    diag_mask = t_idx == s_idx
    level_masks = []
    for l in range(L):
        m = C >> (l + 1)
        sh = (L - l)
        level_masks.append(((t_idx & m) != 0) & ((s_idx & m) == 0) & ((t_idx >> sh) == (s_idx >> sh)))
    nw = nw_ref[...]

    def chunk_body(ci, carry):
        r0 = pl.multiple_of(ci * C, C)
        rows = pl.ds(r0, C)
        lf = lf_ref[rows, :]
        hi = lf.astype(BF16)
        lo = (lf - hi.astype(F32)).astype(BF16)
        for blk in range(L + 2):
            a_blk = a_ref[blk * C:(blk + 1) * C, :]
            e_scr[blk * C:(blk + 1) * C, :] = jnp.exp(_dot(a_blk, hi) + _dot(a_blk, lo))

        for h in range(HG_HEADS):
            hl = slice(h * HG_KEY, (h + 1) * HG_KEY)
            q_bf = q_ref[rows, hl]
            k_bf = k_ref[rows, hl]
            v_bf = v_ref[rows, hl]
            q = q_bf.astype(F32)
            k = k_bf.astype(F32)
            scores = jnp.where(diag_mask, _dot_nt(q_bf, k_bf), 0.0)
            for l in range(L):
                e_l = e_scr[l * C:(l + 1) * C, hl]
                p = _dot_nt((q * e_l).astype(BF16), (k * e_l).astype(BF16))
                scores = scores + jnp.where(level_masks[l], p, 0.0)
            e_b = e_scr[L * C:(L + 1) * C, hl]
            e_suf = e_scr[(L + 1) * C:(L + 2) * C, hl]
            st = state_scr[h]
            o = _dot(scores.astype(BF16), v_bf) + _dot_nt((q * e_b).astype(BF16), st.astype(BF16))
            e_last = e_scr[(L + 1) * C - 1:(L + 1) * C, hl]
            state_scr[h] = st * e_last + _dot_tn(v_bf, (k * e_suf).astype(BF16))
            ms = jnp.mean(o * o, axis=-1, keepdims=True)
            y = o * lax.rsqrt(ms + NORM_EPS) * nw * g_ref[rows, hl].astype(F32)
            o_ref[rows, hl] = y.astype(BF16)
        return carry

    lax.fori_loop(0, HG_TS // C, chunk_body, 0)


def _hgrn(qa, ka, lf, va, ga, a_mat, hgrn_norm_w, B, S):
    ts = HG_TS
    steps = S // ts
    blk = pl.BlockSpec((ts, HG_DIM), lambda b, i: (b * steps + i, 0))
    rows_a = (HG_LEVELS + 2) * HG_CHUNK
    return pl.pallas_call(
        _hgrn_kernel,
        grid=(B, steps),
        in_specs=[blk, blk, blk, blk, blk,
                  pl.BlockSpec((rows_a, HG_CHUNK), lambda b, i: (0, 0)),
                  pl.BlockSpec((1, HG_VAL), lambda b, i: (0, 0))],
        out_specs=blk,
        out_shape=jax.ShapeDtypeStruct((B * S, HG_DIM), BF16),
        scratch_shapes=[pltpu.VMEM((HG_HEADS, HG_VAL, HG_KEY), F32),
                        pltpu.VMEM((rows_a, HG_DIM), F32)],
        compiler_params=pltpu.CompilerParams(dimension_semantics=("parallel", "arbitrary"),
                                             vmem_limit_bytes=VMEM_LIMIT_BYTES),
        name="hgrn2",
    )(qa, ka, lf, va, ga, a_mat, hgrn_norm_w)


def _attn_kernel(*refs, seq):
    n_in = 3 * sum(ATT_DILATIONS) + 1
    in_refs = refs[:n_in]
    ag_ref = in_refs[-1]
    out_ref = refs[n_in]
    o_scr = refs[n_in + 1:n_in + 1 + N_GROUPS]
    lse_scr = refs[n_in + 1 + N_GROUPS:n_in + 1 + 2 * N_GROUPS]
    n = ATT_STEPS

    lane = lax.broadcasted_iota(jnp.int32, (n, LANES), 1)
    head_a = lane < ATT_HEAD_DIM
    qi = lax.broadcasted_iota(jnp.int32, (n, n), 0)
    kj = lax.broadcasted_iota(jnp.int32, (n, n), 1)
    cur_ok = kj <= qi
    prev_ok = kj >= qi

    def attend(q, k_cur, v_cur, k_prev, v_prev, has_prev):
        zero = jnp.zeros_like(q)
        res = []
        for qh in (jnp.where(head_a, q, zero), jnp.where(head_a, zero, q)):
            s_c = jnp.where(cur_ok, _dot_nt(qh, k_cur), NEG_BIG)
            m = jnp.max(s_c, axis=-1, keepdims=True)
            if k_prev is not None:
                s_p = jnp.where(jnp.logical_and(prev_ok, has_prev), _dot_nt(qh, k_prev), NEG_BIG)
                m = jnp.maximum(m, jnp.max(s_p, axis=-1, keepdims=True))
            p_c = jnp.exp(s_c - m)
            l = jnp.sum(p_c, axis=-1, keepdims=True)
            o = _dot(p_c.astype(BF16), v_cur)
            if k_prev is not None:
                p_p = jnp.exp(s_p - m)
                l = l + jnp.sum(p_p, axis=-1, keepdims=True)
                o = o + _dot(p_p.astype(BF16), v_prev)
            res.append((o * (1.0 / l), m + jnp.log(l)))
        (o_a, lse_a), (o_b, lse_b) = res
        return jnp.where(head_a, o_a, o_b), jnp.where(head_a, lse_a, lse_b)

    pos = 0
    for g, d in enumerate(ATT_DILATIONS):
        m_len = seq // d
        nblk = m_len // n
        q_refs = in_refs[pos:pos + d]
        k_refs = in_refs[pos + d:pos + 2 * d]
        v_refs = in_refs[pos + 2 * d:pos + 3 * d]
        pos += 3 * d
        for r in range(d):
            q_ref, k_ref, v_ref = q_refs[r], k_refs[r], v_refs[r]

            def store(c, o, lse, g=g, d=d, r=r):
                if d == 1:
                    dst = pl.ds(pl.multiple_of(c * n, n), n)
                else:
                    dst = pl.ds(c * (n * d) + r, n, stride=d)
                o_scr[g][dst, :] = o
                lse_scr[g][dst, :] = lse

            if nblk == 1:
                o, lse = attend(q_ref[...], k_ref[...], v_ref[...], None, None, None)
                store(0, o, lse)
            else:
                def body(c, carry, q_ref=q_ref, k_ref=k_ref, v_ref=v_ref, store=store):
                    cur = pl.ds(pl.multiple_of(c * n, n), n)
                    prv = pl.ds(pl.multiple_of(jnp.maximum(c - 1, 0) * n, n), n)
                    o, lse = attend(q_ref[cur, :], k_ref[cur, :], v_ref[cur, :], k_ref[prv, :], v_ref[prv, :], c > 0)
                    store(c, o, lse)
                    return carry
                lax.fori_loop(0, nblk, body, 0)

    rows_per = 256

    def merge(i, carry):
        rows = pl.ds(pl.multiple_of(i * rows_per, rows_per), rows_per)
        l1, l2, l3 = lse_scr[0][rows, :], lse_scr[1][rows, :], lse_scr[2][rows, :]
        mx = jnp.maximum(jnp.maximum(l1, l2), l3)
        w1, w2, w3 = jnp.exp(l1 - mx), jnp.exp(l2 - mx), jnp.exp(l3 - mx)
        num = w1 * o_scr[0][rows, :] + w2 * o_scr[1][rows, :] + w3 * o_scr[2][rows, :]
        ob = num / (w1 + w2 + w3) * ag_ref[rows, :].astype(F32)
        out_ref[rows, :] = ob.astype(BF16)
        return carry

    lax.fori_loop(0, seq // rows_per, merge, 0)


def _attention(qkv, ag, B, S):
    pairs = ATT_DIM // LANES
    operands, in_specs = [], []
    for g, d in enumerate(ATT_DILATIONS):
        m_len = S // d
        for arr in qkv[g]:
            view = arr.reshape(B, m_len, d * ATT_DIM)
            for r in range(d):
                operands.append(view)
                in_specs.append(pl.BlockSpec((None, m_len, LANES),
                                             functools.partial(lambda b, hp, r: (b, 0, r * pairs + hp), r=r)))
    operands.append(ag.reshape(B, S, ATT_DIM))
    nat = pl.BlockSpec((None, S, LANES), lambda b, hp: (b, 0, hp))
    in_specs.append(nat)
    out = pl.pallas_call(
        functools.partial(_attn_kernel, seq=S),
        grid=(B, pairs),
        in_specs=in_specs,
        out_specs=nat,
        out_shape=jax.ShapeDtypeStruct((B, S, ATT_DIM), BF16),
        scratch_shapes=[pltpu.VMEM((S, LANES), F32)] * (2 * N_GROUPS),
        compiler_params=pltpu.CompilerParams(dimension_semantics=("parallel", "parallel"),
                                             vmem_limit_bytes=VMEM_LIMIT_BYTES),
        name="dilated_attn",
    )(*operands)
    return out.reshape(B * S, ATT_DIM)


def _out_kernel(x_ref, oa_ref, ob_ref, gates_ref, wa_ref, wb_ref, wo_ref, fnw_ref, out_ref):
    ya = _dot(oa_ref[...], wa_ref[...])
    yb = _dot(ob_ref[...], wb_ref[...])
    g_a = gates_ref[:, :D_MODEL].astype(F32)
    g_b = gates_ref[:, D_MODEL:].astype(F32)
    merged = (g_a * ya + g_b * yb).astype(BF16)
    y = x_ref[...] + _dot(merged, wo_ref[...])
    ms = jnp.mean(y * y, axis=-1, keepdims=True)
    out_ref[...] = y * lax.rsqrt(ms + NORM_EPS) * fnw_ref[...]


def _output(x2, oa, ob, gates, wa, wb, wo, fnw):
    T = x2.shape[0]
    tm = OUT_TM
    row = lambda n: pl.BlockSpec((tm, n), lambda i: (i, 0))
    const = lambda shape: pl.BlockSpec(shape, lambda i: (0, 0))
    return pl.pallas_call(
        _out_kernel,
        grid=(T // tm,),
        in_specs=[row(D_MODEL), row(HG_DIM), row(ATT_DIM), row(2 * D_MODEL),
                  const((HG_DIM, D_MODEL)), const((ATT_DIM, D_MODEL)), const((D_MODEL, D_MODEL)),
                  const((1, D_MODEL))],
        out_specs=row(D_MODEL),
        out_shape=jax.ShapeDtypeStruct((T, D_MODEL), F32),
        compiler_params=pltpu.CompilerParams(dimension_semantics=("parallel",),
                                             vmem_limit_bytes=VMEM_LIMIT_BYTES),
        name="merge_out",
    )(x2, oa, ob, gates, wa, wb, wo, fnw)


def kernel(x, positions, norm_w, w_in, lb_logits, hgrn_norm_w, w_branch_a, w_branch_b, w_out, final_norm_w):
    B, S, D = x.shape
    assert D == D_MODEL and norm_w.shape[0] == 1 and lb_logits.shape == (2, HG_DIM)
    assert S % (ATT_STEPS * max(ATT_DILATIONS)) == 0 and S % HG_TS == 0 and (B * S) % OUT_TM == 0
    T = B * S
    x2 = x.reshape(T, D)
    pos_b = jnp.broadcast_to(positions.reshape(T, 1), (T, LANES))
    inv_freq = ROPE_THETA ** (-jnp.arange(0, ATT_HEAD_DIM, 2, dtype=F32) / ATT_HEAD_DIM)
    invf = jnp.tile(inv_freq, LANES // (ATT_HEAD_DIM // 2)).reshape(1, LANES)

    outs = _inproj(x2, pos_b, norm_w.reshape(1, D), w_in[0].astype(BF16), lb_logits.astype(F32), invf)
    qa, ka, lf, va, ga = outs[:5]
    qkv = [tuple(outs[5 + 3 * g:8 + 3 * g]) for g in range(N_GROUPS)]
    ag, gates = outs[14], outs[15]

    a_mat = jnp.asarray(_hgrn_sum_matrix(), BF16)
    oa = _hgrn(qa, ka, lf, va, ga, a_mat, hgrn_norm_w.reshape(1, HG_VAL).astype(F32), B, S)
    ob = _attention(qkv, ag, B, S)
    out = _output(x2, oa, ob, gates, w_branch_a[0].astype(BF16), w_branch_b[0].astype(BF16),
                  w_out[0].astype(BF16), final_norm_w.reshape(1, D).astype(F32))
    return out.reshape(B, S, D)
```

```python
import functools

import numpy as np
import jax
import jax.numpy as jnp
from jax import lax
from jax.experimental import pallas as pl
from jax.experimental.pallas import tpu as pltpu

D_MODEL = 1024
HG_HEADS = 8
HG_KEY = 128
HG_VAL = 128
HG_DIM = HG_HEADS * HG_KEY
ATT_DILATIONS = (1, 4, 16)
ATT_STEPS = 128
N_GROUPS = 3
ATT_HEADS = 8
ATT_HEAD_DIM = 64
ATT_DIM = ATT_HEADS * ATT_HEAD_DIM
ROPE_THETA = 10000.0
NORM_EPS = 1e-6
C_HQ, C_HF, C_HI, C_HG = 0, 1024, 2048, 3072
C_AQ, C_AK, C_AV = 4096, 4096 + 1536, 4096 + 3072
C_AG = 8704
C_GATES = 9216
IN_COLS = 11264

LANES = 128
VMEM_LIMIT_BYTES = 56 * 1024 * 1024

IN_TM = 256
HG_CHUNK = 64
HG_TS = 256
HG_LEVELS = 6
OUT_TM = 512
ATT_UNROLL = 3
NEG_BIG = -1e30

F32 = jnp.float32
BF16 = jnp.bfloat16


def _dot(a, b):
    return jnp.dot(a, b, preferred_element_type=F32)


def _dot_nt(a, b):
    return lax.dot_general(a, b, (((1,), (1,)), ((), ())), preferred_element_type=F32)


def _dot_tn(a, b):
    return lax.dot_general(a, b, (((0,), (0,)), ((), ())), preferred_element_type=F32)


def _sigmoid(z):
    return 1.0 / (1.0 + jnp.exp(-z))


def _inproj_kernel(x_ref, pos_ref, nw_ref, w_ref, lbl_ref, invf_ref,
                   qa_ref, ka_ref, lf_ref, va_ref, ga_ref,
                   q1_ref, k1_ref, v1_ref, q2_ref, k2_ref, v2_ref, q3_ref, k3_ref, v3_ref,
                   ag_ref, gates_ref, perm_scr):
    tm = x_ref.shape[0]
    x = x_ref[...]
    ms = jnp.mean(x * x, axis=-1, keepdims=True)
    h = (x * lax.rsqrt(ms + NORM_EPS) * nw_ref[...]).astype(BF16)

    def proj(c0, n):
        return _dot(h, w_ref[:, c0:c0 + n])

    lbl = lbl_ref[...]
    lmax = jnp.max(lbl, axis=0, keepdims=True)
    lexp = jnp.exp(lbl - lmax)
    lb = lexp[0:1, :] / jnp.sum(lexp, axis=0, keepdims=True)

    half = 512
    for j in range(2):
        sl = slice(j * half, (j + 1) * half)
        z = proj(C_HQ + j * half, half)
        qa_ref[:, sl] = (z * _sigmoid(z)).astype(BF16)
        z = proj(C_HF + j * half, half)
        lbj = lb[:, sl]
        f = lbj + (1.0 - lbj) * _sigmoid(z)
        lf_ref[:, sl] = jnp.log(f)
        ka_ref[:, sl] = ((1.0 - lbj) * _sigmoid(-z)).astype(BF16)
        z = proj(C_HI + j * half, half)
        va_ref[:, sl] = z.astype(BF16)
        z = proj(C_HG + j * half, half)
        ga_ref[:, sl] = (z * _sigmoid(z)).astype(BF16)

    ang = pos_ref[...].astype(F32) * invf_ref[...]
    cos = jnp.cos(ang)
    sin = jnp.sin(ang)
    lane = lax.broadcasted_iota(jnp.int32, ang.shape, 1)
    first = (lane & (ATT_HEAD_DIM - 1)) < (ATT_HEAD_DIM // 2)
    sin_signed = jnp.where(first, -sin, sin)

    def rope(zs):
        rot = jnp.where(first, pltpu.roll(zs, LANES - ATT_HEAD_DIM // 2, 1), pltpu.roll(zs, ATT_HEAD_DIM // 2, 1))
        return zs * cos + rot * sin_signed

    scale = ATT_HEAD_DIM ** -0.5
    for g, (q_ref, k_ref, v_ref) in enumerate(((q1_ref, k1_ref, v1_ref), (q2_ref, k2_ref, v2_ref),
                                                (q3_ref, k3_ref, v3_ref))):
        d = ATT_DILATIONS[g]
        zq = proj(C_AQ + g * ATT_DIM, ATT_DIM)
        zk = proj(C_AK + g * ATT_DIM, ATT_DIM)
        zv = proj(C_AV + g * ATT_DIM, ATT_DIM)
        for s in range(ATT_DIM // LANES):
            ls = slice(s * LANES, (s + 1) * LANES)
            vals = (rope(zq[:, ls]) * scale, rope(zk[:, ls]), zv[:, ls])
            for a, (ref, val) in enumerate(zip((q_ref, k_ref, v_ref), vals)):
                if d == 1:
                    ref[:, ls] = val.astype(BF16)
                else:
                    perm_scr[a] = val
                    for r in range(d):
                        cols = slice(r * ATT_DIM + s * LANES, r * ATT_DIM + (s + 1) * LANES)
                        ref[:, cols] = perm_scr[a, pl.ds(r, tm // d, stride=d), :].astype(BF16)

    z = proj(C_AG, ATT_DIM)
    ag_ref[...] = (z * _sigmoid(z)).astype(BF16)
    for j in range(4):
        sl = slice(j * half, (j + 1) * half)
        gates_ref[:, sl] = _sigmoid(proj(C_GATES + j * half, half)).astype(BF16)


def _inproj(x2, pos_b, norm_w, w_in_bf, lb_logits, invf):
    T = x2.shape[0]
    tm = IN_TM
    row = lambda n: pl.BlockSpec((tm, n), lambda i: (i, 0))
    const = lambda shape: pl.BlockSpec(shape, lambda i: (0, 0))
    out_shapes = ([jax.ShapeDtypeStruct((T, HG_DIM), BF16),
                   jax.ShapeDtypeStruct((T, HG_DIM), BF16),
                   jax.ShapeDtypeStruct((T, HG_DIM), F32),
                   jax.ShapeDtypeStruct((T, HG_DIM), BF16),
                   jax.ShapeDtypeStruct((T, HG_DIM), BF16)]
                  + [jax.ShapeDtypeStruct((T // d, d * ATT_DIM), BF16) for d in ATT_DILATIONS for _ in range(3)]
                  + [jax.ShapeDtypeStruct((T, ATT_DIM), BF16),
                     jax.ShapeDtypeStruct((T, 2 * D_MODEL), BF16)])
    out_specs = ([row(HG_DIM)] * 5
                 + [pl.BlockSpec((tm // d, d * ATT_DIM), lambda i: (i, 0)) for d in ATT_DILATIONS for _ in range(3)]
                 + [row(ATT_DIM), row(2 * D_MODEL)])
    return pl.pallas_call(
        _inproj_kernel,
        grid=(T // tm,),
        in_specs=[row(D_MODEL), row(LANES), const((1, D_MODEL)),
                  pl.BlockSpec((D_MODEL, IN_COLS), lambda i: (0, 0), pipeline_mode=pl.Buffered(1)),
                  const((2, HG_DIM)), const((1, LANES))],
        out_specs=out_specs,
        out_shape=out_shapes,
        scratch_shapes=[pltpu.VMEM((3, tm, LANES), F32)],
        compiler_params=pltpu.CompilerParams(dimension_semantics=("parallel",),
                                             vmem_limit_bytes=VMEM_LIMIT_BYTES),
        name="inproj",
    )(x2, pos_b, norm_w, w_in_bf, lb_logits, invf)


def _hgrn_sum_matrix():
    C, L = HG_CHUNK, HG_LEVELS
    A = np.zeros(((L + 2) * C, C), np.float32)
    for l in range(L):
        m = C >> (l + 1)
        for t in range(C):
            start = (t // m) * m
            if t & m:
                A[l * C + t, start:t + 1] = 1.0
            else:
                A[l * C + t, t + 1:start + m] = 1.0
    for t in range(C):
        A[L * C + t, :t + 1] = 1.0
        A[(L + 1) * C + t, t + 1:] = 1.0
    return A


def _hgrn_kernel(q_ref, k_ref, lf_ref, v_ref, g_ref, a_ref, nw_ref, o_ref, state_scr, e_scr):
    C, L = HG_CHUNK, HG_LEVELS

    @pl.when(pl.program_id(1) == 0)
    def _():
        state_scr[...] = jnp.zeros_like(state_scr)

    t_idx = lax.broadcasted_iota(jnp.int32, (C, C), 0)
    s_idx = lax.broadcasted_iota(jnp.int32, (C, C), 1)
    diag_mask = t_idx == s_idx
    level_masks = []
    for l in range(L):
        m = C >> (l + 1)
        sh = (L - l)
        level_masks.append(((t_idx & m) != 0) & ((s_idx & m) == 0) & ((t_idx >> sh) == (s_idx >> sh)))
    nw = nw_ref[...]

    def chunk_body(ci, carry):
        r0 = pl.multiple_of(ci * C, C)
        rows = pl.ds(r0, C)
        lf = lf_ref[rows, :]
        hi = lf.astype(BF16)
        lo = (lf - hi.astype(F32)).astype(BF16)
        for blk in range(L + 2):
            a_blk = a_ref[blk * C:(blk + 1) * C, :]
            e_scr[blk * C:(blk + 1) * C, :] = jnp.exp(_dot(a_blk, hi) + _dot(a_blk, lo))

        for h in range(HG_HEADS):
            hl = slice(h * HG_KEY, (h + 1) * HG_KEY)
            q_bf = q_ref[rows, hl]
            k_bf = k_ref[rows, hl]
            v_bf = v_ref[rows, hl]
            q = q_bf.astype(F32)
            k = k_bf.astype(F32)
            scores = jnp.where(diag_mask, _dot_nt(q_bf, k_bf), 0.0)
            for l in range(L):
                e_l = e_scr[l * C:(l + 1) * C, hl]
                p = _dot_nt((q * e_l).astype(BF16), (k * e_l).astype(BF16))
                scores = scores + jnp.where(level_masks[l], p, 0.0)
            e_b = e_scr[L * C:(L + 1) * C, hl]
            e_suf = e_scr[(L + 1) * C:(L + 2) * C, hl]
            st = state_scr[h]
            o = _dot(scores.astype(BF16), v_bf) + _dot_nt((q * e_b).astype(BF16), st.astype(BF16))
            e_last = e_scr[(L + 1) * C - 1:(L + 1) * C, hl]
            state_scr[h] = st * e_last + _dot_tn(v_bf, (k * e_suf).astype(BF16))
            ms = jnp.mean(o * o, axis=-1, keepdims=True)
            y = o * lax.rsqrt(ms + NORM_EPS) * nw * g_ref[rows, hl].astype(F32)
            o_ref[rows, hl] = y.astype(BF16)
        return carry

    lax.fori_loop(0, HG_TS // C, chunk_body, 0)


def _hgrn(qa, ka, lf, va, ga, a_mat, hgrn_norm_w, B, S):
    ts = HG_TS
    steps = S // ts
    blk = pl.BlockSpec((ts, HG_DIM), lambda b, i: (b * steps + i, 0))
    rows_a = (HG_LEVELS + 2) * HG_CHUNK
    return pl.pallas_call(
        _hgrn_kernel,
        grid=(B, steps),
        in_specs=[blk, blk, blk, blk, blk,
                  pl.BlockSpec((rows_a, HG_CHUNK), lambda b, i: (0, 0)),
                  pl.BlockSpec((1, HG_VAL), lambda b, i: (0, 0))],
        out_specs=blk,
        out_shape=jax.ShapeDtypeStruct((B * S, HG_DIM), BF16),
        scratch_shapes=[pltpu.VMEM((HG_HEADS, HG_VAL, HG_KEY), F32),
                        pltpu.VMEM((rows_a, HG_DIM), F32)],
        compiler_params=pltpu.CompilerParams(dimension_semantics=("parallel", "arbitrary"),
                                             vmem_limit_bytes=VMEM_LIMIT_BYTES),
        name="hgrn2",
    )(qa, ka, lf, va, ga, a_mat, hgrn_norm_w)


def _attn_kernel(*refs, seq):
    n_in = 3 * sum(ATT_DILATIONS) + 1
    in_refs = refs[:n_in]
    ag_ref = in_refs[-1]
    out_ref = refs[n_in]
    o_scr = refs[n_in + 1:n_in + 1 + N_GROUPS]
    lse_scr = refs[n_in + 1 + N_GROUPS:n_in + 1 + 2 * N_GROUPS]
    n = ATT_STEPS

    lane = lax.broadcasted_iota(jnp.int32, (n, LANES), 1)
    head_a = lane < ATT_HEAD_DIM
    qi = lax.broadcasted_iota(jnp.int32, (n, 2 * n), 0)
    kj = lax.broadcasted_iota(jnp.int32, (n, 2 * n), 1)
    band_ok = (kj >= qi) & (kj <= qi + n)
    first_ok = (lax.broadcasted_iota(jnp.int32, (n, n), 1)
                <= lax.broadcasted_iota(jnp.int32, (n, n), 0))

    def row_sum_matrix(keys):
        rl = lax.broadcasted_iota(jnp.int32, (2 * keys, LANES), 0)
        ll = lax.broadcasted_iota(jnp.int32, (2 * keys, LANES), 1)
        return jnp.where((rl < keys) == (ll < ATT_HEAD_DIM), 1.0, 0.0).astype(BF16)

    ones_two = row_sum_matrix(2 * n)
    ones_one = row_sum_matrix(n)

    def attend(q, kk, vv, ok, ones):
        zero_q = jnp.zeros_like(q)
        zero_v = jnp.zeros_like(vv)
        ps, ms = [], []
        for qh in (jnp.where(head_a, q, zero_q), jnp.where(head_a, zero_q, q)):
            s = jnp.where(ok, _dot_nt(qh, kk), NEG_BIG)
            sm = s if s.shape[1] == n else jnp.maximum(s[:, :n], s[:, n:])
            m = jnp.max(sm, axis=-1, keepdims=True)
            ps.append(jnp.exp(s - m).astype(BF16))
            ms.append(m)
        p = jnp.concatenate(ps, axis=1)
        va = lax.broadcasted_iota(jnp.int32, vv.shape, 1) < ATT_HEAD_DIM
        v2 = jnp.concatenate([jnp.where(va, vv, zero_v), jnp.where(va, zero_v, vv)], axis=0)
        o = _dot(p, v2)
        l = _dot(p, ones)
        m2 = jnp.where(head_a, ms[0], ms[1])
        return o * (1.0 / l), m2 + jnp.log(l)

    pos = 0
    for g, d in enumerate(ATT_DILATIONS):
        m_len = seq // d
        nblk = m_len // n
        q_refs = in_refs[pos:pos + d]
        k_refs = in_refs[pos + d:pos + 2 * d]
        v_refs = in_refs[pos + 2 * d:pos + 3 * d]
        pos += 3 * d
        for r in range(d):
            q_ref, k_ref, v_ref = q_refs[r], k_refs[r], v_refs[r]

            def store(c, o, lse, g=g, d=d, r=r):
                if d == 1:
                    dst = pl.ds(c * n if isinstance(c, int) else pl.multiple_of(c * n, n), n)
                else:
                    dst = pl.ds(c * (n * d) + r, n, stride=d)
                o_scr[g][dst, :] = o
                lse_scr[g][dst, :] = lse

            def block(c, q_ref=q_ref, k_ref=k_ref, v_ref=v_ref, store=store):
                if isinstance(c, int):
                    cur, both = pl.ds(c * n, n), pl.ds((c - 1) * n, 2 * n)
                else:
                    cur, both = pl.ds(pl.multiple_of(c * n, n), n), pl.ds(pl.multiple_of((c - 1) * n, n), 2 * n)
                o, lse = attend(q_ref[cur, :], k_ref[both, :], v_ref[both, :], band_ok, ones_two)
                store(c, o, lse)

            o, lse = attend(q_ref[0:n, :], k_ref[0:n, :], v_ref[0:n, :], first_ok, ones_one)
            store(0, o, lse)
            rest = nblk - 1
            if 0 < rest <= ATT_UNROLL:
                for c in range(1, nblk):
                    block(c)
            elif rest > 0:
                def body(i, carry, block=block):
                    for u in range(ATT_UNROLL):
                        block(1 + i * ATT_UNROLL + u)
                    return carry
                lax.fori_loop(0, rest // ATT_UNROLL, body, 0)

    rows_per = 256

    def merge(i, carry):
        rows = pl.ds(pl.multiple_of(i * rows_per, rows_per), rows_per)
        l1, l2, l3 = lse_scr[0][rows, :], lse_scr[1][rows, :], lse_scr[2][rows, :]
        mx = jnp.maximum(jnp.maximum(l1, l2), l3)
        w1, w2, w3 = jnp.exp(l1 - mx), jnp.exp(l2 - mx), jnp.exp(l3 - mx)
        num = w1 * o_scr[0][rows, :] + w2 * o_scr[1][rows, :] + w3 * o_scr[2][rows, :]
        ob = num / (w1 + w2 + w3) * ag_ref[rows, :].astype(F32)
        out_ref[rows, :] = ob.astype(BF16)
        return carry

    lax.fori_loop(0, seq // rows_per, merge, 0)


def _attention(qkv, ag, B, S):
    pairs = ATT_DIM // LANES
    operands, in_specs = [], []
    for g, d in enumerate(ATT_DILATIONS):
        m_len = S // d
        rest = m_len // ATT_STEPS - 1
        assert rest <= ATT_UNROLL or rest % ATT_UNROLL == 0
        for arr in qkv[g]:
            view = arr.reshape(B, m_len, d * ATT_DIM)
            for r in range(d):
                operands.append(view)
                in_specs.append(pl.BlockSpec((None, m_len, LANES),
                                             functools.partial(lambda b, hp, r: (b, 0, r * pairs + hp), r=r)))
    operands.append(ag.reshape(B, S, ATT_DIM))
    nat = pl.BlockSpec((None, S, LANES), lambda b, hp: (b, 0, hp))
    in_specs.append(nat)
    out = pl.pallas_call(
        functools.partial(_attn_kernel, seq=S),
        grid=(B, pairs),
        in_specs=in_specs,
        out_specs=nat,
        out_shape=jax.ShapeDtypeStruct((B, S, ATT_DIM), BF16),
        scratch_shapes=[pltpu.VMEM((S, LANES), F32)] * (2 * N_GROUPS),
        compiler_params=pltpu.CompilerParams(dimension_semantics=("parallel", "parallel"),
                                             vmem_limit_bytes=VMEM_LIMIT_BYTES),
        name="dilated_attn",
    )(*operands)
    return out.reshape(B * S, ATT_DIM)


def _out_kernel(x_ref, oa_ref, ob_ref, gates_ref, wa_ref, wb_ref, wo_ref, fnw_ref, out_ref):
    ya = _dot(oa_ref[...], wa_ref[...])
    yb = _dot(ob_ref[...], wb_ref[...])
    g_a = gates_ref[:, :D_MODEL].astype(F32)
    g_b = gates_ref[:, D_MODEL:].astype(F32)
    merged = (g_a * ya + g_b * yb).astype(BF16)
    y = x_ref[...] + _dot(merged, wo_ref[...])
    ms = jnp.mean(y * y, axis=-1, keepdims=True)
    out_ref[...] = y * lax.rsqrt(ms + NORM_EPS) * fnw_ref[...]


def _output(x2, oa, ob, gates, wa, wb, wo, fnw):
    T = x2.shape[0]
    tm = OUT_TM
    row = lambda n: pl.BlockSpec((tm, n), lambda i: (i, 0))
    const = lambda shape: pl.BlockSpec(shape, lambda i: (0, 0))
    return pl.pallas_call(
        _out_kernel,
        grid=(T // tm,),
        in_specs=[row(D_MODEL), row(HG_DIM), row(ATT_DIM), row(2 * D_MODEL),
                  const((HG_DIM, D_MODEL)), const((ATT_DIM, D_MODEL)), const((D_MODEL, D_MODEL)),
                  const((1, D_MODEL))],
        out_specs=row(D_MODEL),
        out_shape=jax.ShapeDtypeStruct((T, D_MODEL), F32),
        compiler_params=pltpu.CompilerParams(dimension_semantics=("parallel",),
                                             vmem_limit_bytes=VMEM_LIMIT_BYTES),
        name="merge_out",
    )(x2, oa, ob, gates, wa, wb, wo, fnw)


def kernel(x, positions, norm_w, w_in, lb_logits, hgrn_norm_w, w_branch_a, w_branch_b, w_out, final_norm_w):
    B, S, D = x.shape
    assert D == D_MODEL and norm_w.shape[0] == 1 and lb_logits.shape == (2, HG_DIM)
    assert S % (ATT_STEPS * max(ATT_DILATIONS)) == 0 and S % HG_TS == 0 and (B * S) % OUT_TM == 0
    T = B * S
    x2 = x.reshape(T, D)
    pos_b = jnp.broadcast_to(positions.reshape(T, 1), (T, LANES))
    inv_freq = ROPE_THETA ** (-jnp.arange(0, ATT_HEAD_DIM, 2, dtype=F32) / ATT_HEAD_DIM)
    invf = jnp.tile(inv_freq, LANES // (ATT_HEAD_DIM // 2)).reshape(1, LANES)

    outs = _inproj(x2, pos_b, norm_w.reshape(1, D), w_in[0].astype(BF16), lb_logits.astype(F32), invf)
    qa, ka, lf, va, ga = outs[:5]
    qkv = [tuple(outs[5 + 3 * g:8 + 3 * g]) for g in range(N_GROUPS)]
    ag, gates = outs[14], outs[15]

    a_mat = jnp.asarray(_hgrn_sum_matrix(), BF16)
    oa = _hgrn(qa, ka, lf, va, ga, a_mat, hgrn_norm_w.reshape(1, HG_VAL).astype(F32), B, S)
    ob = _attention(qkv, ag, B, S)
    out = _output(x2, oa, ob, gates, w_branch_a[0].astype(BF16), w_branch_b[0].astype(BF16),
                  w_out[0].astype(BF16), final_norm_w.reshape(1, D).astype(F32))
    return out.reshape(B, S, D)
```

```python
import functools

import numpy as np
import jax
import jax.numpy as jnp
from jax import lax
from jax.experimental import pallas as pl
from jax.experimental.pallas import tpu as pltpu

D_MODEL = 1024
HG_HEADS = 8
HG_KEY = 128
HG_VAL = 128
HG_DIM = HG_HEADS * HG_KEY
ATT_DILATIONS = (1, 4, 16)
ATT_STEPS = 128
N_GROUPS = 3
ATT_HEADS = 8
ATT_HEAD_DIM = 64
ATT_DIM = ATT_HEADS * ATT_HEAD_DIM
ROPE_THETA = 10000.0
NORM_EPS = 1e-6
C_HQ, C_HF, C_HI, C_HG = 0, 1024, 2048, 3072
C_AQ, C_AK, C_AV = 4096, 4096 + 1536, 4096 + 3072
C_AG = 8704
C_GATES = 9216
IN_COLS = 11264

LANES = 128
VMEM_LIMIT_BYTES = 56 * 1024 * 1024

IN_TM = 256
HG_CHUNK = 64
HG_TS = 256
HG_LEVELS = 6
OUT_TM = 512
ATT_SKEW = 2
NEG_BIG = -1e30

F32 = jnp.float32
BF16 = jnp.bfloat16


def _dot(a, b):
    return jnp.dot(a, b, preferred_element_type=F32)


def _dot_nt(a, b):
    return lax.dot_general(a, b, (((1,), (1,)), ((), ())), preferred_element_type=F32)


def _dot_tn(a, b):
    return lax.dot_general(a, b, (((0,), (0,)), ((), ())), preferred_element_type=F32)


def _sigmoid(z):
    return 1.0 / (1.0 + jnp.exp(-z))


def _inproj_kernel(x_ref, pos_ref, nw_ref, w_ref, lbl_ref, invf_ref,
                   qa_ref, ka_ref, lf_ref, va_ref, ga_ref,
                   q1_ref, k1_ref, v1_ref, q2_ref, k2_ref, v2_ref, q3_ref, k3_ref, v3_ref,
                   ag_ref, gates_ref, perm_scr):
    tm = x_ref.shape[0]
    x = x_ref[...]
    ms = jnp.mean(x * x, axis=-1, keepdims=True)
    h = (x * lax.rsqrt(ms + NORM_EPS) * nw_ref[...]).astype(BF16)

    def proj(c0, n):
        return _dot(h, w_ref[:, c0:c0 + n])

    lbl = lbl_ref[...]
    lmax = jnp.max(lbl, axis=0, keepdims=True)
    lexp = jnp.exp(lbl - lmax)
    lb = lexp[0:1, :] / jnp.sum(lexp, axis=0, keepdims=True)

    half = 512
    for j in range(2):
        sl = slice(j * half, (j + 1) * half)
        z = proj(C_HQ + j * half, half)
        qa_ref[:, sl] = (z * _sigmoid(z)).astype(BF16)
        z = proj(C_HF + j * half, half)
        lbj = lb[:, sl]
        f = lbj + (1.0 - lbj) * _sigmoid(z)
        lf_ref[:, sl] = jnp.log(f)
        ka_ref[:, sl] = ((1.0 - lbj) * _sigmoid(-z)).astype(BF16)
        z = proj(C_HI + j * half, half)
        va_ref[:, sl] = z.astype(BF16)
        z = proj(C_HG + j * half, half)
        ga_ref[:, sl] = (z * _sigmoid(z)).astype(BF16)

    ang = pos_ref[...].astype(F32) * invf_ref[...]
    cos = jnp.cos(ang)
    sin = jnp.sin(ang)
    lane = lax.broadcasted_iota(jnp.int32, ang.shape, 1)
    first = (lane & (ATT_HEAD_DIM - 1)) < (ATT_HEAD_DIM // 2)
    sin_signed = jnp.where(first, -sin, sin)

    def rope(zs):
        rot = jnp.where(first, pltpu.roll(zs, LANES - ATT_HEAD_DIM // 2, 1), pltpu.roll(zs, ATT_HEAD_DIM // 2, 1))
        return zs * cos + rot * sin_signed

    scale = ATT_HEAD_DIM ** -0.5
    for g, (q_ref, k_ref, v_ref) in enumerate(((q1_ref, k1_ref, v1_ref), (q2_ref, k2_ref, v2_ref),
                                                (q3_ref, k3_ref, v3_ref))):
        d = ATT_DILATIONS[g]
        zq = proj(C_AQ + g * ATT_DIM, ATT_DIM)
        zk = proj(C_AK + g * ATT_DIM, ATT_DIM)
        zv = proj(C_AV + g * ATT_DIM, ATT_DIM)
        for s in range(ATT_DIM // LANES):
            ls = slice(s * LANES, (s + 1) * LANES)
            vals = (rope(zq[:, ls]) * scale, rope(zk[:, ls]), zv[:, ls])
            for a, (ref, val) in enumerate(zip((q_ref, k_ref, v_ref), vals)):
                if d == 1:
                    ref[:, ls] = val.astype(BF16)
                else:
                    perm_scr[a] = val
                    for r in range(d):
                        cols = slice(r * ATT_DIM + s * LANES, r * ATT_DIM + (s + 1) * LANES)
                        ref[:, cols] = perm_scr[a, pl.ds(r, tm // d, stride=d), :].astype(BF16)

    z = proj(C_AG, ATT_DIM)
    ag_ref[...] = (z * _sigmoid(z)).astype(BF16)
    for j in range(4):
        sl = slice(j * half, (j + 1) * half)
        gates_ref[:, sl] = _sigmoid(proj(C_GATES + j * half, half)).astype(BF16)


def _inproj(x2, pos_b, norm_w, w_in_bf, lb_logits, invf):
    T = x2.shape[0]
    tm = IN_TM
    row = lambda n: pl.BlockSpec((tm, n), lambda i: (i, 0))
    const = lambda shape: pl.BlockSpec(shape, lambda i: (0, 0))
    out_shapes = ([jax.ShapeDtypeStruct((T, HG_DIM), BF16),
                   jax.ShapeDtypeStruct((T, HG_DIM), BF16),
                   jax.ShapeDtypeStruct((T, HG_DIM), F32),
                   jax.ShapeDtypeStruct((T, HG_DIM), BF16),
                   jax.ShapeDtypeStruct((T, HG_DIM), BF16)]
                  + [jax.ShapeDtypeStruct((T // d, d * ATT_DIM), BF16) for d in ATT_DILATIONS for _ in range(3)]
                  + [jax.ShapeDtypeStruct((T, ATT_DIM), BF16),
                     jax.ShapeDtypeStruct((T, 2 * D_MODEL), BF16)])
    out_specs = ([row(HG_DIM)] * 5
                 + [pl.BlockSpec((tm // d, d * ATT_DIM), lambda i: (i, 0)) for d in ATT_DILATIONS for _ in range(3)]
                 + [row(ATT_DIM), row(2 * D_MODEL)])
    return pl.pallas_call(
        _inproj_kernel,
        grid=(T // tm,),
        in_specs=[row(D_MODEL), row(LANES), const((1, D_MODEL)),
                  pl.BlockSpec((D_MODEL, IN_COLS), lambda i: (0, 0), pipeline_mode=pl.Buffered(1)),
                  const((2, HG_DIM)), const((1, LANES))],
        out_specs=out_specs,
        out_shape=out_shapes,
        scratch_shapes=[pltpu.VMEM((3, tm, LANES), F32)],
        compiler_params=pltpu.CompilerParams(dimension_semantics=("parallel",),
                                             vmem_limit_bytes=VMEM_LIMIT_BYTES),
        name="inproj",
    )(x2, pos_b, norm_w, w_in_bf, lb_logits, invf)


def _hgrn_sum_matrix():
    C, L = HG_CHUNK, HG_LEVELS
    A = np.zeros(((L + 2) * C, C), np.float32)
    for l in range(L):
        m = C >> (l + 1)
        for t in range(C):
            start = (t // m) * m
            if t & m:
                A[l * C + t, start:t + 1] = 1.0
            else:
                A[l * C + t, t + 1:start + m] = 1.0
    for t in range(C):
        A[L * C + t, :t + 1] = 1.0
        A[(L + 1) * C + t, t + 1:] = 1.0
    return A


def _hgrn_kernel(q_ref, k_ref, lf_ref, v_ref, g_ref, a_ref, nw_ref, o_ref, state_scr, e_scr):
    C, L = HG_CHUNK, HG_LEVELS

    @pl.when(pl.program_id(1) == 0)
    def _():
        state_scr[...] = jnp.zeros_like(state_scr)

    t_idx = lax.broadcasted_iota(jnp.int32, (C, C), 0)
    s_idx = lax.broadcasted_iota(jnp.int32, (C, C), 1)
    diag_mask = t_idx == s_idx
    level_masks = []
    for l in range(L):
        m = C >> (l + 1)
        sh = (L - l)
        level_masks.append(((t_idx & m) != 0) & ((s_idx & m) == 0) & ((t_idx >> sh) == (s_idx >> sh)))
    nw = nw_ref[...]

    def chunk_body(ci, carry):
        r0 = pl.multiple_of(ci * C, C)
        rows = pl.ds(r0, C)
        lf = lf_ref[rows, :]
        hi = lf.astype(BF16)
        lo = (lf - hi.astype(F32)).astype(BF16)
        for blk in range(L + 2):
            a_blk = a_ref[blk * C:(blk + 1) * C, :]
            e_scr[blk * C:(blk + 1) * C, :] = jnp.exp(_dot(a_blk, hi) + _dot(a_blk, lo))

        for h in range(HG_HEADS):
            hl = slice(h * HG_KEY, (h + 1) * HG_KEY)
            q_bf = q_ref[rows, hl]
            k_bf = k_ref[rows, hl]
            v_bf = v_ref[rows, hl]
            q = q_bf.astype(F32)
            k = k_bf.astype(F32)
            scores = jnp.where(diag_mask, _dot_nt(q_bf, k_bf), 0.0)
            for l in range(L):
                e_l = e_scr[l * C:(l + 1) * C, hl]
                p = _dot_nt((q * e_l).astype(BF16), (k * e_l).astype(BF16))
                scores = scores + jnp.where(level_masks[l], p, 0.0)
            e_b = e_scr[L * C:(L + 1) * C, hl]
            e_suf = e_scr[(L + 1) * C:(L + 2) * C, hl]
            st = state_scr[h]
            o = _dot(scores.astype(BF16), v_bf) + _dot_nt((q * e_b).astype(BF16), st.astype(BF16))
            e_last = e_scr[(L + 1) * C - 1:(L + 1) * C, hl]
            state_scr[h] = st * e_last + _dot_tn(v_bf, (k * e_suf).astype(BF16))
            ms = jnp.mean(o * o, axis=-1, keepdims=True)
            y = o * lax.rsqrt(ms + NORM_EPS) * nw * g_ref[rows, hl].astype(F32)
            o_ref[rows, hl] = y.astype(BF16)
        return carry

    lax.fori_loop(0, HG_TS // C, chunk_body, 0)


def _hgrn(qa, ka, lf, va, ga, a_mat, hgrn_norm_w, B, S):
    ts = HG_TS
    steps = S // ts
    blk = pl.BlockSpec((ts, HG_DIM), lambda b, i: (b * steps + i, 0))
    rows_a = (HG_LEVELS + 2) * HG_CHUNK
    return pl.pallas_call(
        _hgrn_kernel,
        grid=(B, steps),
        in_specs=[blk, blk, blk, blk, blk,
                  pl.BlockSpec((rows_a, HG_CHUNK), lambda b, i: (0, 0)),
                  pl.BlockSpec((1, HG_VAL), lambda b, i: (0, 0))],
        out_specs=blk,
        out_shape=jax.ShapeDtypeStruct((B * S, HG_DIM), BF16),
        scratch_shapes=[pltpu.VMEM((HG_HEADS, HG_VAL, HG_KEY), F32),
                        pltpu.VMEM((rows_a, HG_DIM), F32)],
        compiler_params=pltpu.CompilerParams(dimension_semantics=("parallel", "arbitrary"),
                                             vmem_limit_bytes=VMEM_LIMIT_BYTES),
        name="hgrn2",
    )(qa, ka, lf, va, ga, a_mat, hgrn_norm_w)


def _attn_kernel(*refs, seq):
    n_in = 3 * sum(ATT_DILATIONS) + 1
    in_refs = refs[:n_in]
    ag_ref = in_refs[-1]
    out_ref = refs[n_in]
    o_scr = refs[n_in + 1:n_in + 1 + N_GROUPS]
    lse_scr = refs[n_in + 1 + N_GROUPS:n_in + 1 + 2 * N_GROUPS]
    n = ATT_STEPS

    lane = lax.broadcasted_iota(jnp.int32, (n, LANES), 1)
    head_a = lane < ATT_HEAD_DIM
    qi = lax.broadcasted_iota(jnp.int32, (n, 2 * n), 0)
    kj = lax.broadcasted_iota(jnp.int32, (n, 2 * n), 1)
    band_ok = (kj >= qi) & (kj <= qi + n)
    first_ok = (lax.broadcasted_iota(jnp.int32, (n, n), 1)
                <= lax.broadcasted_iota(jnp.int32, (n, n), 0))

    def row_sum_matrix(keys):
        rl = lax.broadcasted_iota(jnp.int32, (2 * keys, LANES), 0)
        ll = lax.broadcasted_iota(jnp.int32, (2 * keys, LANES), 1)
        return jnp.where((rl < keys) == (ll < ATT_HEAD_DIM), 1.0, 0.0).astype(BF16)

    ones_two = row_sum_matrix(2 * n)
    ones_one = row_sum_matrix(n)

    def scores(q_ref, k_ref, c):
        q = q_ref[c * n:(c + 1) * n, :]
        kk = k_ref[max(c - 1, 0) * n:(c + 1) * n, :]
        zero_q = jnp.zeros_like(q)
        return [_dot_nt(jnp.where(head_a, q, zero_q), kk), _dot_nt(jnp.where(head_a, zero_q, q), kk)]

    def finish(g, d, r, c, v_ref, s_pair):
        ok, ones = (first_ok, ones_one) if c == 0 else (band_ok, ones_two)
        vv = v_ref[max(c - 1, 0) * n:(c + 1) * n, :]
        ps, ms = [], []
        for s in s_pair:
            s = jnp.where(ok, s, NEG_BIG)
            sm = s if c == 0 else jnp.maximum(s[:, :n], s[:, n:])
            m = jnp.max(sm, axis=-1, keepdims=True)
            ps.append(jnp.exp(s - m).astype(BF16))
            ms.append(m)
        p = jnp.concatenate(ps, axis=1)
        zero_v = jnp.zeros_like(vv)
        va = lax.broadcasted_iota(jnp.int32, vv.shape, 1) < ATT_HEAD_DIM
        v2 = jnp.concatenate([jnp.where(va, vv, zero_v), jnp.where(va, zero_v, vv)], axis=0)
        ol = _dot(p, jnp.concatenate([v2, ones], axis=1))
        l = ol[:, LANES:]
        o = ol[:, :LANES] * (1.0 / l)
        lse = jnp.where(head_a, ms[0], ms[1]) + jnp.log(l)
        dst = pl.ds(c * n, n) if d == 1 else pl.ds(c * (n * d) + r, n, stride=d)
        o_scr[g][dst, :] = o
        lse_scr[g][dst, :] = lse

    pending = []
    pos = 0
    for g, d in enumerate(ATT_DILATIONS):
        q_refs = in_refs[pos:pos + d]
        k_refs = in_refs[pos + d:pos + 2 * d]
        v_refs = in_refs[pos + 2 * d:pos + 3 * d]
        pos += 3 * d
        for r in range(d):
            for c in range(seq // d // n):
                pending.append((g, d, r, c, v_refs[r], scores(q_refs[r], k_refs[r], c)))
                if len(pending) > ATT_SKEW:
                    finish(*pending.pop(0))
    while pending:
        finish(*pending.pop(0))

    rows_per = 256

    def merge(i, carry):
        rows = pl.ds(pl.multiple_of(i * rows_per, rows_per), rows_per)
        l1, l2, l3 = lse_scr[0][rows, :], lse_scr[1][rows, :], lse_scr[2][rows, :]
        mx = jnp.maximum(jnp.maximum(l1, l2), l3)
        w1, w2, w3 = jnp.exp(l1 - mx), jnp.exp(l2 - mx), jnp.exp(l3 - mx)
        num = w1 * o_scr[0][rows, :] + w2 * o_scr[1][rows, :] + w3 * o_scr[2][rows, :]
        ob = num / (w1 + w2 + w3) * ag_ref[rows, :].astype(F32)
        out_ref[rows, :] = ob.astype(BF16)
        return carry

    lax.fori_loop(0, seq // rows_per, merge, 0)


def _attention(qkv, ag, B, S):
    pairs = ATT_DIM // LANES
    operands, in_specs = [], []
    for g, d in enumerate(ATT_DILATIONS):
        m_len = S // d
        for arr in qkv[g]:
            view = arr.reshape(B, m_len, d * ATT_DIM)
            for r in range(d):
                operands.append(view)
                in_specs.append(pl.BlockSpec((None, m_len, LANES),
                                             functools.partial(lambda b, hp, r: (b, 0, r * pairs + hp), r=r)))
    operands.append(ag.reshape(B, S, ATT_DIM))
    nat = pl.BlockSpec((None, S, LANES), lambda b, hp: (b, 0, hp))
    in_specs.append(nat)
    out = pl.pallas_call(
        functools.partial(_attn_kernel, seq=S),
        grid=(B, pairs),
        in_specs=in_specs,
        out_specs=nat,
        out_shape=jax.ShapeDtypeStruct((B, S, ATT_DIM), BF16),
        scratch_shapes=[pltpu.VMEM((S, LANES), F32)] * (2 * N_GROUPS),
        compiler_params=pltpu.CompilerParams(dimension_semantics=("parallel", "parallel"),
                                             vmem_limit_bytes=VMEM_LIMIT_BYTES),
        name="dilated_attn",
    )(*operands)
    return out.reshape(B * S, ATT_DIM)


def _out_kernel(x_ref, oa_ref, ob_ref, gates_ref, wa_ref, wb_ref, wo_ref, fnw_ref, out_ref):
    ya = _dot(oa_ref[...], wa_ref[...])
    yb = _dot(ob_ref[...], wb_ref[...])
    g_a = gates_ref[:, :D_MODEL].astype(F32)
    g_b = gates_ref[:, D_MODEL:].astype(F32)
    merged = (g_a * ya + g_b * yb).astype(BF16)
    y = x_ref[...] + _dot(merged, wo_ref[...])
    ms = jnp.mean(y * y, axis=-1, keepdims=True)
    out_ref[...] = y * lax.rsqrt(ms + NORM_EPS) * fnw_ref[...]


def _output(x2, oa, ob, gates, wa, wb, wo, fnw):
    T = x2.shape[0]
    tm = OUT_TM
    row = lambda n: pl.BlockSpec((tm, n), lambda i: (i, 0))
    const = lambda shape: pl.BlockSpec(shape, lambda i: (0, 0))
    return pl.pallas_call(
        _out_kernel,
        grid=(T // tm,),
        in_specs=[row(D_MODEL), row(HG_DIM), row(ATT_DIM), row(2 * D_MODEL),
                  const((HG_DIM, D_MODEL)), const((ATT_DIM, D_MODEL)), const((D_MODEL, D_MODEL)),
                  const((1, D_MODEL))],
        out_specs=row(D_MODEL),
        out_shape=jax.ShapeDtypeStruct((T, D_MODEL), F32),
        compiler_params=pltpu.CompilerParams(dimension_semantics=("parallel",),
                                             vmem_limit_bytes=VMEM_LIMIT_BYTES),
        name="merge_out",
    )(x2, oa, ob, gates, wa, wb, wo, fnw)


def kernel(x, positions, norm_w, w_in, lb_logits, hgrn_norm_w, w_branch_a, w_branch_b, w_out, final_norm_w):
    B, S, D = x.shape
    assert D == D_MODEL and norm_w.shape[0] == 1 and lb_logits.shape == (2, HG_DIM)
    assert S % (ATT_STEPS * max(ATT_DILATIONS)) == 0 and S % HG_TS == 0 and (B * S) % OUT_TM == 0
    T = B * S
    x2 = x.reshape(T, D)
    pos_b = jnp.broadcast_to(positions.reshape(T, 1), (T, LANES))
    inv_freq = ROPE_THETA ** (-jnp.arange(0, ATT_HEAD_DIM, 2, dtype=F32) / ATT_HEAD_DIM)
    invf = jnp.tile(inv_freq, LANES // (ATT_HEAD_DIM // 2)).reshape(1, LANES)

    outs = _inproj(x2, pos_b, norm_w.reshape(1, D), w_in[0].astype(BF16), lb_logits.astype(F32), invf)
    qa, ka, lf, va, ga = outs[:5]
    qkv = [tuple(outs[5 + 3 * g:8 + 3 * g]) for g in range(N_GROUPS)]
    ag, gates = outs[14], outs[15]

    a_mat = jnp.asarray(_hgrn_sum_matrix(), BF16)
    oa = _hgrn(qa, ka, lf, va, ga, a_mat, hgrn_norm_w.reshape(1, HG_VAL).astype(F32), B, S)
    ob = _attention(qkv, ag, B, S)
    out = _output(x2, oa, ob, gates, w_branch_a[0].astype(BF16), w_branch_b[0].astype(BF16),
                  w_out[0].astype(BF16), final_norm_w.reshape(1, D).astype(F32))
    return out.reshape(B, S, D)
```

```python
import functools

import numpy as np
import jax
import jax.numpy as jnp
from jax import lax
from jax.experimental import pallas as pl
from jax.experimental.pallas import tpu as pltpu

D_MODEL = 1024
HG_HEADS = 8
HG_KEY = 128
HG_VAL = 128
HG_DIM = HG_HEADS * HG_KEY
ATT_DILATIONS = (1, 4, 16)
ATT_STEPS = 128
N_GROUPS = 3
ATT_HEADS = 8
ATT_HEAD_DIM = 64
ATT_DIM = ATT_HEADS * ATT_HEAD_DIM
ROPE_THETA = 10000.0
NORM_EPS = 1e-6
LOG2_E = 1.4426950408889634
C_HQ, C_HF, C_HI, C_HG = 0, 1024, 2048, 3072
C_AQ, C_AK, C_AV = 4096, 4096 + 1536, 4096 + 3072
C_AG = 8704
C_GATES = 9216
IN_COLS = 11264

LANES = 128
VMEM_LIMIT_BYTES = 56 * 1024 * 1024

IN_TM = 256
HG_CHUNK = 64
HG_TS = 256
HG_LEVEL_SIZES = tuple(HG_CHUNK >> (l + 1) for l in range(HG_CHUNK.bit_length() - 1))
HG_ROW_TILE = 8
HG_SKEW = 1
OUT_TM = 512
ATT_SKEW = 2
NEG_BIG = -1e30

F32 = jnp.float32
BF16 = jnp.bfloat16


def _dot(a, b):
    return jnp.dot(a, b, preferred_element_type=F32)


def _dot_nt(a, b):
    return lax.dot_general(a, b, (((1,), (1,)), ((), ())), preferred_element_type=F32)


def _dot_tn(a, b):
    return lax.dot_general(a, b, (((0,), (0,)), ((), ())), preferred_element_type=F32)


def _sigmoid(z):
    return 1.0 / (1.0 + jnp.exp(-z))


def _inproj_kernel(x_ref, pos_ref, nw_ref, w_ref, lbl_ref, invf_ref,
                   qa_ref, ka_ref, lf_ref, va_ref, ga_ref,
                   q1_ref, k1_ref, v1_ref, q2_ref, k2_ref, v2_ref, q3_ref, k3_ref, v3_ref,
                   ag_ref, gates_ref, perm_scr):
    tm = x_ref.shape[0]
    x = x_ref[...]
    ms = jnp.mean(x * x, axis=-1, keepdims=True)
    h = (x * lax.rsqrt(ms + NORM_EPS) * nw_ref[...]).astype(BF16)

    def proj(c0, n):
        return _dot(h, w_ref[:, c0:c0 + n])

    lbl = lbl_ref[...]
    lmax = jnp.max(lbl, axis=0, keepdims=True)
    lexp = jnp.exp(lbl - lmax)
    lb = lexp[0:1, :] / jnp.sum(lexp, axis=0, keepdims=True)

    half = 512
    for j in range(2):
        sl = slice(j * half, (j + 1) * half)
        z = proj(C_HQ + j * half, half)
        qa_ref[:, sl] = (z * _sigmoid(z)).astype(BF16)
        z = proj(C_HF + j * half, half)
        lbj = lb[:, sl]
        f = lbj + (1.0 - lbj) * _sigmoid(z)
        lf_ref[:, sl] = jnp.log(f) * LOG2_E
        ka_ref[:, sl] = ((1.0 - lbj) * _sigmoid(-z)).astype(BF16)
        z = proj(C_HI + j * half, half)
        va_ref[:, sl] = z.astype(BF16)
        z = proj(C_HG + j * half, half)
        ga_ref[:, sl] = (z * _sigmoid(z)).astype(BF16)

    ang = pos_ref[...].astype(F32) * invf_ref[...]
    cos = jnp.cos(ang)
    sin = jnp.sin(ang)
    lane = lax.broadcasted_iota(jnp.int32, ang.shape, 1)
    first = (lane & (ATT_HEAD_DIM - 1)) < (ATT_HEAD_DIM // 2)
    sin_signed = jnp.where(first, -sin, sin)

    def rope(zs):
        rot = jnp.where(first, pltpu.roll(zs, LANES - ATT_HEAD_DIM // 2, 1), pltpu.roll(zs, ATT_HEAD_DIM // 2, 1))
        return zs * cos + rot * sin_signed

    scale = ATT_HEAD_DIM ** -0.5
    for g, (q_ref, k_ref, v_ref) in enumerate(((q1_ref, k1_ref, v1_ref), (q2_ref, k2_ref, v2_ref),
                                                (q3_ref, k3_ref, v3_ref))):
        d = ATT_DILATIONS[g]
        zq = proj(C_AQ + g * ATT_DIM, ATT_DIM)
        zk = proj(C_AK + g * ATT_DIM, ATT_DIM)
        zv = proj(C_AV + g * ATT_DIM, ATT_DIM)
        for s in range(ATT_DIM // LANES):
            ls = slice(s * LANES, (s + 1) * LANES)
            vals = (rope(zq[:, ls]) * scale, rope(zk[:, ls]), zv[:, ls])
            for a, (ref, val) in enumerate(zip((q_ref, k_ref, v_ref), vals)):
                if d == 1:
                    ref[:, ls] = val.astype(BF16)
                else:
                    perm_scr[a] = val
                    for r in range(d):
                        cols = slice(r * ATT_DIM + s * LANES, r * ATT_DIM + (s + 1) * LANES)
                        ref[:, cols] = perm_scr[a, pl.ds(r, tm // d, stride=d), :].astype(BF16)

    z = proj(C_AG, ATT_DIM)
    ag_ref[...] = (z * _sigmoid(z)).astype(BF16)
    for j in range(4):
        sl = slice(j * half, (j + 1) * half)
        gates_ref[:, sl] = _sigmoid(proj(C_GATES + j * half, half)).astype(BF16)


def _inproj(x2, pos_b, norm_w, w_in_bf, lb_logits, invf):
    T = x2.shape[0]
    tm = IN_TM
    row = lambda n: pl.BlockSpec((tm, n), lambda i: (i, 0))
    const = lambda shape: pl.BlockSpec(shape, lambda i: (0, 0))
    out_shapes = ([jax.ShapeDtypeStruct((T, HG_DIM), BF16),
                   jax.ShapeDtypeStruct((T, HG_DIM), BF16),
                   jax.ShapeDtypeStruct((T, HG_DIM), F32),
                   jax.ShapeDtypeStruct((T, HG_DIM), BF16),
                   jax.ShapeDtypeStruct((T, HG_DIM), BF16)]
                  + [jax.ShapeDtypeStruct((T // d, d * ATT_DIM), BF16) for d in ATT_DILATIONS for _ in range(3)]
                  + [jax.ShapeDtypeStruct((T, ATT_DIM), BF16),
                     jax.ShapeDtypeStruct((T, 2 * D_MODEL), BF16)])
    out_specs = ([row(HG_DIM)] * 5
                 + [pl.BlockSpec((tm // d, d * ATT_DIM), lambda i: (i, 0)) for d in ATT_DILATIONS for _ in range(3)]
                 + [row(ATT_DIM), row(2 * D_MODEL)])
    return pl.pallas_call(
        _inproj_kernel,
        grid=(T // tm,),
        in_specs=[row(D_MODEL), row(LANES), const((1, D_MODEL)),
                  pl.BlockSpec((D_MODEL, IN_COLS), lambda i: (0, 0), pipeline_mode=pl.Buffered(1)),
                  const((2, HG_DIM)), const((1, LANES))],
        out_specs=out_specs,
        out_shape=out_shapes,
        scratch_shapes=[pltpu.VMEM((3, tm, LANES), F32)],
        compiler_params=pltpu.CompilerParams(dimension_semantics=("parallel",),
                                             vmem_limit_bytes=VMEM_LIMIT_BYTES),
        name="inproj",
    )(x2, pos_b, norm_w, w_in_bf, lb_logits, invf)


def _hgrn_sum_matrix():
    C = HG_CHUNK
    fine = [m for m in HG_LEVEL_SIZES if m < HG_ROW_TILE]
    A = np.zeros(((1 + len(fine)) * C, C), np.float32)
    for t in range(C):
        A[t, :t + 1] = 1.0
    for i, m in enumerate(fine):
        for t in range(C):
            start = (t // m) * m
            if t & m:
                A[(1 + i) * C + t, start:t + 1] = 1.0
            else:
                A[(1 + i) * C + t, t + 1:start + m] = 1.0
    return np.concatenate([A, A], axis=1)


def _hgrn_kernel(q_ref, k_ref, lf_ref, v_ref, g_ref, a_ref, nw_ref, o_ref, state_scr, e_scr):
    C = HG_CHUNK
    n_lev = len(HG_LEVEL_SIZES)
    n_chunks = HG_TS // C

    @pl.when(pl.program_id(1) == 0)
    def _():
        state_scr[...] = jnp.zeros_like(state_scr)

    t_idx = lax.broadcasted_iota(jnp.int32, (C, 2 * C), 0)
    s_idx = lax.broadcasted_iota(jnp.int32, (C, 2 * C), 1) & (C - 1)
    diag_mask = t_idx == s_idx
    level_masks = []
    for m in HG_LEVEL_SIZES:
        sh = (2 * m).bit_length() - 1
        level_masks.append(((t_idx & m) != 0) & ((s_idx & m) == 0) & ((t_idx >> sh) == (s_idx >> sh)))
    nw = nw_ref[...]
    row_idx = lax.broadcasted_iota(jnp.int32, (C, HG_KEY), 0)
    right_rows = [(row_idx & m) != 0 for m in HG_LEVEL_SIZES]

    def exponents(ci):
        slot = ci % 2
        lf = lf_ref[ci * C:(ci + 1) * C, :]
        hi = lf.astype(BF16)
        lo = (lf - hi.astype(F32)).astype(BF16)
        sums = _dot(a_ref[...], jnp.concatenate([hi, lo], axis=0))
        b = sums[0:C, :]

        def put(i, arg):
            e_scr[slot, i * C:(i + 1) * C, :] = jnp.exp2(arg)

        i_fine = 1
        for i, m in enumerate(HG_LEVEL_SIZES):
            if m >= HG_ROW_TILE:
                ref = jnp.concatenate([jnp.broadcast_to(b[j * 2 * m + m - 1:j * 2 * m + m, :], (2 * m, HG_DIM))
                                       for j in range(C // (2 * m))], axis=0)
                put(i, -jnp.abs(b - ref))
            else:
                put(i, sums[i_fine * C:(i_fine + 1) * C, :])
                i_fine += 1
        put(n_lev, b)
        put(n_lev + 1, b[C - 1:C, :] - b)

    def block_diag(x_a, x_b):
        return jnp.concatenate([jnp.concatenate([x_a, jnp.zeros_like(x_a)], axis=1),
                                jnp.concatenate([jnp.zeros_like(x_b), x_b], axis=1)], axis=0)

    def pair_scores(ci, hp):
        slot = ci % 2
        rows = slice(ci * C, (ci + 1) * C)
        lanes = [slice(h * HG_KEY, (h + 1) * HG_KEY) for h in (2 * hp, 2 * hp + 1)]
        q_bf = [q_ref[rows, hl] for hl in lanes]
        k_bf = [k_ref[rows, hl] for hl in lanes]
        q = [x.astype(F32) for x in q_bf]
        k = [x.astype(F32) for x in k_bf]
        scores = jnp.where(diag_mask, _dot_nt(jnp.concatenate(q_bf, axis=1), block_diag(*k_bf)), 0.0)
        for i, m in enumerate(HG_LEVEL_SIZES):
            zs = []
            for a, hl in enumerate(lanes):
                if m >= HG_ROW_TILE:
                    w = jnp.concatenate([(q[a] if j % 2 else k[a])[j * m:(j + 1) * m, :] for j in range(C // m)],
                                        axis=0)
                else:
                    w = jnp.where(right_rows[i], q[a], k[a])
                zs.append((w * e_scr[slot, i * C:(i + 1) * C, hl]).astype(BF16))
            scores = jnp.where(level_masks[i], _dot_nt(jnp.concatenate(zs, axis=1), block_diag(*zs)), scores)
        return scores

    def finish(ci, hp, scores):
        slot = ci % 2
        rows = slice(ci * C, (ci + 1) * C)
        heads = (2 * hp, 2 * hp + 1)
        lanes = [slice(h * HG_KEY, (h + 1) * HG_KEY) for h in heads]
        v_bf = [v_ref[rows, hl] for hl in lanes]
        o_pair = _dot(scores.astype(BF16), block_diag(*v_bf))
        for a, (h, hl) in enumerate(zip(heads, lanes)):
            q = q_ref[rows, hl].astype(F32)
            k = k_ref[rows, hl].astype(F32)
            e_b = e_scr[slot, n_lev * C:(n_lev + 1) * C, hl]
            e_suf = e_scr[slot, (n_lev + 1) * C:(n_lev + 2) * C, hl]
            e_last = e_scr[slot, (n_lev + 1) * C - 1:(n_lev + 1) * C, hl]
            st = state_scr[h]
            o = o_pair[:, a * HG_VAL:(a + 1) * HG_VAL] + _dot_nt((q * e_b).astype(BF16), st.astype(BF16))
            state_scr[h] = st * e_last + _dot_tn(v_bf[a], (k * e_suf).astype(BF16))
            ms = jnp.mean(o * o, axis=-1, keepdims=True)
            y = o * lax.rsqrt(ms + NORM_EPS) * nw * g_ref[rows, hl].astype(F32)
            o_ref[rows, hl] = y.astype(BF16)

    n_pairs = HG_HEADS // 2
    exponents(0)
    pending = []
    for ci in range(n_chunks):
        for hp in range(n_pairs):
            if hp == n_pairs // 2 and ci + 1 < n_chunks:
                exponents(ci + 1)
            pending.append((ci, hp, pair_scores(ci, hp)))
            if len(pending) > HG_SKEW:
                finish(*pending.pop(0))
    while pending:
        finish(*pending.pop(0))


def _hgrn(qa, ka, lf, va, ga, a_mat, hgrn_norm_w, B, S):
    ts = HG_TS
    steps = S // ts
    blk = pl.BlockSpec((ts, HG_DIM), lambda b, i: (b * steps + i, 0))
    rows_e = (len(HG_LEVEL_SIZES) + 2) * HG_CHUNK
    return pl.pallas_call(
        _hgrn_kernel,
        grid=(B, steps),
        in_specs=[blk, blk, blk, blk, blk,
                  pl.BlockSpec(a_mat.shape, lambda b, i: (0, 0)),
                  pl.BlockSpec((1, HG_VAL), lambda b, i: (0, 0))],
        out_specs=blk,
        out_shape=jax.ShapeDtypeStruct((B * S, HG_DIM), BF16),
        scratch_shapes=[pltpu.VMEM((HG_HEADS, HG_VAL, HG_KEY), F32),
                        pltpu.VMEM((2, rows_e, HG_DIM), F32)],
        compiler_params=pltpu.CompilerParams(dimension_semantics=("parallel", "arbitrary"),
                                             vmem_limit_bytes=VMEM_LIMIT_BYTES),
        name="hgrn2",
    )(qa, ka, lf, va, ga, a_mat, hgrn_norm_w)


def _attn_kernel(*refs, seq):
    n_in = 3 * sum(ATT_DILATIONS) + 1
    in_refs = refs[:n_in]
    ag_ref = in_refs[-1]
    out_ref = refs[n_in]
    o_scr = refs[n_in + 1:n_in + 1 + N_GROUPS]
    lse_scr = refs[n_in + 1 + N_GROUPS:n_in + 1 + 2 * N_GROUPS]
    n = ATT_STEPS

    lane = lax.broadcasted_iota(jnp.int32, (n, LANES), 1)
    head_a = lane < ATT_HEAD_DIM
    qi = lax.broadcasted_iota(jnp.int32, (n, 2 * n), 0)
    kj = lax.broadcasted_iota(jnp.int32, (n, 2 * n), 1)
    band_ok = (kj >= qi) & (kj <= qi + n)
    first_ok = (lax.broadcasted_iota(jnp.int32, (n, n), 1)
                <= lax.broadcasted_iota(jnp.int32, (n, n), 0))

    def row_sum_matrix(keys):
        rl = lax.broadcasted_iota(jnp.int32, (2 * keys, LANES), 0)
        ll = lax.broadcasted_iota(jnp.int32, (2 * keys, LANES), 1)
        return jnp.where((rl < keys) == (ll < ATT_HEAD_DIM), 1.0, 0.0).astype(BF16)

    ones_two = row_sum_matrix(2 * n)
    ones_one = row_sum_matrix(n)

    def scores(q_ref, k_ref, c):
        q = q_ref[c * n:(c + 1) * n, :]
        kk = k_ref[max(c - 1, 0) * n:(c + 1) * n, :]
        zero_q = jnp.zeros_like(q)
        return [_dot_nt(jnp.where(head_a, q, zero_q), kk), _dot_nt(jnp.where(head_a, zero_q, q), kk)]

    def finish(g, d, r, c, v_ref, s_pair):
        ok, ones = (first_ok, ones_one) if c == 0 else (band_ok, ones_two)
        vv = v_ref[max(c - 1, 0) * n:(c + 1) * n, :]
        ps, ms = [], []
        for s in s_pair:
            s = jnp.where(ok, s, NEG_BIG)
            sm = s if c == 0 else jnp.maximum(s[:, :n], s[:, n:])
            m = jnp.max(sm, axis=-1, keepdims=True)
            ps.append(jnp.exp(s - m).astype(BF16))
            ms.append(m)
        p = jnp.concatenate(ps, axis=1)
        zero_v = jnp.zeros_like(vv)
        va = lax.broadcasted_iota(jnp.int32, vv.shape, 1) < ATT_HEAD_DIM
        v2 = jnp.concatenate([jnp.where(va, vv, zero_v), jnp.where(va, zero_v, vv)], axis=0)
        ol = _dot(p, jnp.concatenate([v2, ones], axis=1))
        l = ol[:, LANES:]
        o = ol[:, :LANES] * (1.0 / l)
        lse = jnp.where(head_a, ms[0], ms[1]) + jnp.log(l)
        dst = pl.ds(c * n, n) if d == 1 else pl.ds(c * (n * d) + r, n, stride=d)
        o_scr[g][dst, :] = o
        lse_scr[g][dst, :] = lse

    pending = []
    pos = 0
    for g, d in enumerate(ATT_DILATIONS):
        q_refs = in_refs[pos:pos + d]
        k_refs = in_refs[pos + d:pos + 2 * d]
        v_refs = in_refs[pos + 2 * d:pos + 3 * d]
        pos += 3 * d
        for r in range(d):
            for c in range(seq // d // n):
                pending.append((g, d, r, c, v_refs[r], scores(q_refs[r], k_refs[r], c)))
                if len(pending) > ATT_SKEW:
                    finish(*pending.pop(0))
    while pending:
        finish(*pending.pop(0))

    rows_per = 256

    def merge(i, carry):
        rows = pl.ds(pl.multiple_of(i * rows_per, rows_per), rows_per)
        l1, l2, l3 = lse_scr[0][rows, :], lse_scr[1][rows, :], lse_scr[2][rows, :]
        mx = jnp.maximum(jnp.maximum(l1, l2), l3)
        w1, w2, w3 = jnp.exp(l1 - mx), jnp.exp(l2 - mx), jnp.exp(l3 - mx)
        num = w1 * o_scr[0][rows, :] + w2 * o_scr[1][rows, :] + w3 * o_scr[2][rows, :]
        ob = num / (w1 + w2 + w3) * ag_ref[rows, :].astype(F32)
        out_ref[rows, :] = ob.astype(BF16)
        return carry

    lax.fori_loop(0, seq // rows_per, merge, 0)


def _attention(qkv, ag, B, S):
    pairs = ATT_DIM // LANES
    operands, in_specs = [], []
    for g, d in enumerate(ATT_DILATIONS):
        m_len = S // d
        for arr in qkv[g]:
            view = arr.reshape(B, m_len, d * ATT_DIM)
            for r in range(d):
                operands.append(view)
                in_specs.append(pl.BlockSpec((None, m_len, LANES),
                                             functools.partial(lambda b, hp, r: (b, 0, r * pairs + hp), r=r)))
    operands.append(ag.reshape(B, S, ATT_DIM))
    nat = pl.BlockSpec((None, S, LANES), lambda b, hp: (b, 0, hp))
    in_specs.append(nat)
    out = pl.pallas_call(
        functools.partial(_attn_kernel, seq=S),
        grid=(B, pairs),
        in_specs=in_specs,
        out_specs=nat,
        out_shape=jax.ShapeDtypeStruct((B, S, ATT_DIM), BF16),
        scratch_shapes=[pltpu.VMEM((S, LANES), F32)] * (2 * N_GROUPS),
        compiler_params=pltpu.CompilerParams(dimension_semantics=("parallel", "parallel"),
                                             vmem_limit_bytes=VMEM_LIMIT_BYTES),
        name="dilated_attn",
    )(*operands)
    return out.reshape(B * S, ATT_DIM)


def _out_kernel(x_ref, oa_ref, ob_ref, gates_ref, wa_ref, wb_ref, wo_ref, fnw_ref, out_ref):
    ya = _dot(oa_ref[...], wa_ref[...])
    yb = _dot(ob_ref[...], wb_ref[...])
    g_a = gates_ref[:, :D_MODEL].astype(F32)
    g_b = gates_ref[:, D_MODEL:].astype(F32)
    merged = (g_a * ya + g_b * yb).astype(BF16)
    y = x_ref[...] + _dot(merged, wo_ref[...])
    ms = jnp.mean(y * y, axis=-1, keepdims=True)
    out_ref[...] = y * lax.rsqrt(ms + NORM_EPS) * fnw_ref[...]


def _output(x2, oa, ob, gates, wa, wb, wo, fnw):
    T = x2.shape[0]
    tm = OUT_TM
    row = lambda n: pl.BlockSpec((tm, n), lambda i: (i, 0))
    const = lambda shape: pl.BlockSpec(shape, lambda i: (0, 0))
    return pl.pallas_call(
        _out_kernel,
        grid=(T // tm,),
        in_specs=[row(D_MODEL), row(HG_DIM), row(ATT_DIM), row(2 * D_MODEL),
                  const((HG_DIM, D_MODEL)), const((ATT_DIM, D_MODEL)), const((D_MODEL, D_MODEL)),
                  const((1, D_MODEL))],
        out_specs=row(D_MODEL),
        out_shape=jax.ShapeDtypeStruct((T, D_MODEL), F32),
        compiler_params=pltpu.CompilerParams(dimension_semantics=("parallel",),
                                             vmem_limit_bytes=VMEM_LIMIT_BYTES),
        name="merge_out",
    )(x2, oa, ob, gates, wa, wb, wo, fnw)


def kernel(x, positions, norm_w, w_in, lb_logits, hgrn_norm_w, w_branch_a, w_branch_b, w_out, final_norm_w):
    B, S, D = x.shape
    assert D == D_MODEL and norm_w.shape[0] == 1 and lb_logits.shape == (2, HG_DIM)
    assert S % (ATT_STEPS * max(ATT_DILATIONS)) == 0 and S % HG_TS == 0 and (B * S) % OUT_TM == 0
    T = B * S
    x2 = x.reshape(T, D)
    pos_b = jnp.broadcast_to(positions.reshape(T, 1), (T, LANES))
    inv_freq = ROPE_THETA ** (-jnp.arange(0, ATT_HEAD_DIM, 2, dtype=F32) / ATT_HEAD_DIM)
    invf = jnp.tile(inv_freq, LANES // (ATT_HEAD_DIM // 2)).reshape(1, LANES)

    outs = _inproj(x2, pos_b, norm_w.reshape(1, D), w_in[0].astype(BF16), lb_logits.astype(F32), invf)
    qa, ka, lf, va, ga = outs[:5]
    qkv = [tuple(outs[5 + 3 * g:8 + 3 * g]) for g in range(N_GROUPS)]
    ag, gates = outs[14], outs[15]

    a_mat = jnp.asarray(_hgrn_sum_matrix(), BF16)
    oa = _hgrn(qa, ka, lf, va, ga, a_mat, hgrn_norm_w.reshape(1, HG_VAL).astype(F32), B, S)
    ob = _attention(qkv, ag, B, S)
    out = _output(x2, oa, ob, gates, w_branch_a[0].astype(BF16), w_branch_b[0].astype(BF16),
                  w_out[0].astype(BF16), final_norm_w.reshape(1, D).astype(F32))
    return out.reshape(B, S, D)
```

```python
import functools

import numpy as np
import jax
import jax.numpy as jnp
from jax import lax
from jax.experimental import pallas as pl
from jax.experimental.pallas import tpu as pltpu

D_MODEL = 1024
HG_HEADS = 8
HG_KEY = 128
HG_VAL = 128
HG_DIM = HG_HEADS * HG_KEY
ATT_DILATIONS = (1, 4, 16)
ATT_STEPS = 128
N_GROUPS = 3
ATT_HEADS = 8
ATT_HEAD_DIM = 64
ATT_DIM = ATT_HEADS * ATT_HEAD_DIM
ROPE_THETA = 10000.0
NORM_EPS = 1e-6
LOG2_E = 1.4426950408889634
C_HQ, C_HF, C_HI, C_HG = 0, 1024, 2048, 3072
C_AQ, C_AK, C_AV = 4096, 4096 + 1536, 4096 + 3072
C_AG = 8704
C_GATES = 9216
IN_COLS = 11264

LANES = 128
VMEM_LIMIT_BYTES = 56 * 1024 * 1024

IN_TM = 256
IN_SEG = 512
PERM_STRIDE = 4
HG_CHUNK = 64
HG_TS = 256
HG_LEVEL_SIZES = tuple(HG_CHUNK >> (l + 1) for l in range(HG_CHUNK.bit_length() - 1))
HG_ROW_TILE = 8
HG_SKEW = 1
OUT_TM = 512
ATT_SKEW = 2
NEG_BIG = -1e30

F32 = jnp.float32
BF16 = jnp.bfloat16


def _dot(a, b):
    return jnp.dot(a, b, preferred_element_type=F32)


def _dot_nt(a, b):
    return lax.dot_general(a, b, (((1,), (1,)), ((), ())), preferred_element_type=F32)


def _dot_tn(a, b):
    return lax.dot_general(a, b, (((0,), (0,)), ((), ())), preferred_element_type=F32)


def _sigmoid_pair(z):
    half_t = 0.5 * jnp.tanh(0.5 * z)
    return 0.5 + half_t, 0.5 - half_t


def _sigmoid(z):
    return _sigmoid_pair(z)[0]


def _inproj_kernel(x0_ref, xn_ref, pos_ref, nw_ref, w_ref, lbl_ref, invf_ref,
                   qa_ref, ka_ref, lf_ref, va_ref, ga_ref,
                   q1_ref, k1_ref, v1_ref, q2_ref, k2_ref, v2_ref, q3_ref, k3_ref, v3_ref,
                   ag_ref, gates_ref, perm_scr, h_scr):
    tm = xn_ref.shape[0]
    step = pl.program_id(0)
    slot = step % 2

    def normed(x_ref):
        x = x_ref[...]
        ms = jnp.mean(x * x, axis=-1, keepdims=True)
        return (x * lax.rsqrt(ms + NORM_EPS) * nw_ref[...]).astype(BF16)

    @pl.when(step == 0)
    def _():
        h_scr[0] = normed(x0_ref)

    def proj(c0, n):
        return _dot(h_scr[slot], w_ref[:, c0:c0 + n])

    lbl = lbl_ref[...]
    lmax = jnp.max(lbl, axis=0, keepdims=True)
    lexp = jnp.exp(lbl - lmax)
    lb = lexp[0:1, :] / jnp.sum(lexp, axis=0, keepdims=True)

    ang = pos_ref[...].astype(F32) * invf_ref[...]
    cos = jnp.cos(ang)
    sin = jnp.sin(ang)
    lane = lax.broadcasted_iota(jnp.int32, ang.shape, 1)
    first = (lane & (ATT_HEAD_DIM - 1)) < (ATT_HEAD_DIM // 2)
    sin_signed = jnp.where(first, -sin, sin)

    def rope(zs):
        rot = jnp.where(first, pltpu.roll(zs, LANES - ATT_HEAD_DIM // 2, 1), pltpu.roll(zs, ATT_HEAD_DIM // 2, 1))
        return zs * cos + rot * sin_signed

    def silu_to(ref, sl):
        def epi(z):
            ref[:, sl] = (z * _sigmoid(z)).astype(BF16)
        return epi

    def cast_to(ref, sl):
        def epi(z):
            ref[:, sl] = z.astype(BF16)
        return epi

    def sigmoid_to(ref, sl):
        def epi(z):
            ref[:, sl] = _sigmoid(z).astype(BF16)
        return epi

    def forget_to(sl):
        def epi(z):
            lbj = lb[:, sl]
            sig, sig_neg = _sigmoid_pair(z)
            f = lbj + (1.0 - lbj) * sig
            lf_ref[:, sl] = jnp.log(f) * LOG2_E
            ka_ref[:, sl] = ((1.0 - lbj) * sig_neg).astype(BF16)
        return epi

    def attn_to(ref, d, slab, rotate, scale):
        def epi(z):
            for s in range(ATT_DIM // LANES):
                ls = slice(s * LANES, (s + 1) * LANES)
                val = z[:, ls]
                if rotate:
                    val = rope(val)
                if scale != 1.0:
                    val = val * scale
                if d == 1:
                    ref[:, ls] = val.astype(BF16)
                    continue
                perm_scr[0, slab] = val
                if d > PERM_STRIDE:
                    for r1 in range(PERM_STRIDE):
                        perm_scr[1, slab, r1 * (tm // PERM_STRIDE):(r1 + 1) * (tm // PERM_STRIDE), :] = (
                            perm_scr[0, slab, pl.ds(r1, tm // PERM_STRIDE, stride=PERM_STRIDE), :])
                for r in range(d):
                    cols = slice(r * ATT_DIM + s * LANES, r * ATT_DIM + (s + 1) * LANES)
                    if d > PERM_STRIDE:
                        r1, r2 = r % PERM_STRIDE, r // PERM_STRIDE
                        rows = pl.ds(r1 * (tm // PERM_STRIDE) + r2, tm // d, stride=PERM_STRIDE)
                        ref[:, cols] = perm_scr[1, slab, rows, :].astype(BF16)
                    else:
                        ref[:, cols] = perm_scr[0, slab, pl.ds(r, tm // d, stride=d), :].astype(BF16)
        return epi

    halves = [slice(j * IN_SEG, (j + 1) * IN_SEG) for j in range(HG_DIM // IN_SEG)]
    segments = []
    for j, sl in enumerate(halves):
        segments += [(C_HQ + j * IN_SEG, silu_to(qa_ref, sl)), (C_HF + j * IN_SEG, forget_to(sl)),
                     (C_HG + j * IN_SEG, silu_to(ga_ref, sl))]
    for j in range(2 * D_MODEL // IN_SEG):
        segments.append((C_GATES + j * IN_SEG, sigmoid_to(gates_ref, slice(j * IN_SEG, (j + 1) * IN_SEG))))
    segments.append((C_AG, silu_to(ag_ref, slice(0, ATT_DIM))))
    att_refs = ((q1_ref, k1_ref, v1_ref), (q2_ref, k2_ref, v2_ref), (q3_ref, k3_ref, v3_ref))
    for g in reversed(range(N_GROUPS)):
        d = ATT_DILATIONS[g]
        segments += [(C_AQ + g * ATT_DIM, attn_to(att_refs[g][0], d, 0, True, ATT_HEAD_DIM ** -0.5)),
                     (C_AK + g * ATT_DIM, attn_to(att_refs[g][1], d, 1, True, 1.0)),
                     (C_AV + g * ATT_DIM, attn_to(att_refs[g][2], d, 2, False, 1.0))]
    for j, sl in enumerate(halves):
        segments.append((C_HI + j * IN_SEG, cast_to(va_ref, sl)))

    z = proj(segments[0][0], IN_SEG)
    h_scr[1 - slot] = normed(xn_ref)
    for i, (_, epilogue) in enumerate(segments):
        z_next = proj(segments[i + 1][0], IN_SEG) if i + 1 < len(segments) else None
        epilogue(z)
        z = z_next


def _inproj(x2, pos_b, norm_w, w_in_bf, lb_logits, invf):
    T = x2.shape[0]
    tm = IN_TM
    steps = T // tm
    row = lambda n: pl.BlockSpec((tm, n), lambda i: (i, 0))
    const = lambda shape: pl.BlockSpec(shape, lambda i: (0, 0))
    out_shapes = ([jax.ShapeDtypeStruct((T, HG_DIM), BF16),
                   jax.ShapeDtypeStruct((T, HG_DIM), BF16),
                   jax.ShapeDtypeStruct((T, HG_DIM), F32),
                   jax.ShapeDtypeStruct((T, HG_DIM), BF16),
                   jax.ShapeDtypeStruct((T, HG_DIM), BF16)]
                  + [jax.ShapeDtypeStruct((T // d, d * ATT_DIM), BF16) for d in ATT_DILATIONS for _ in range(3)]
                  + [jax.ShapeDtypeStruct((T, ATT_DIM), BF16),
                     jax.ShapeDtypeStruct((T, 2 * D_MODEL), BF16)])
    out_specs = ([row(HG_DIM)] * 5
                 + [pl.BlockSpec((tm // d, d * ATT_DIM), lambda i: (i, 0)) for d in ATT_DILATIONS for _ in range(3)]
                 + [row(ATT_DIM), row(2 * D_MODEL)])
    return pl.pallas_call(
        _inproj_kernel,
        grid=(steps,),
        in_specs=[const((tm, D_MODEL)),
                  pl.BlockSpec((tm, D_MODEL), lambda i: (jnp.minimum(i + 1, steps - 1), 0)),
                  row(LANES), const((1, D_MODEL)),
                  pl.BlockSpec((D_MODEL, IN_COLS), lambda i: (0, 0), pipeline_mode=pl.Buffered(1)),
                  const((2, HG_DIM)), const((1, LANES))],
        out_specs=out_specs,
        out_shape=out_shapes,
        scratch_shapes=[pltpu.VMEM((2, 3, tm, LANES), F32), pltpu.VMEM((2, tm, D_MODEL), BF16)],
        compiler_params=pltpu.CompilerParams(dimension_semantics=("arbitrary",),
                                             vmem_limit_bytes=VMEM_LIMIT_BYTES),
        name="inproj",
    )(x2, x2, pos_b, norm_w, w_in_bf, lb_logits, invf)


def _hgrn_sum_matrix():
    C = HG_CHUNK
    fine = [m for m in HG_LEVEL_SIZES if m < HG_ROW_TILE]
    A = np.zeros(((1 + len(fine)) * C, C), np.float32)
    for t in range(C):
        A[t, :t + 1] = 1.0
    for i, m in enumerate(fine):
        for t in range(C):
            start = (t // m) * m
            if t & m:
                A[(1 + i) * C + t, start:t + 1] = 1.0
            else:
                A[(1 + i) * C + t, t + 1:start + m] = 1.0
    return np.concatenate([A, A], axis=1)


def _hgrn_kernel(q_ref, k_ref, lf_ref, v_ref, g_ref, a_ref, nw_ref, o_ref, state_scr, e_scr):
    C = HG_CHUNK
    n_lev = len(HG_LEVEL_SIZES)
    n_chunks = HG_TS // C

    @pl.when(pl.program_id(1) == 0)
    def _():
        state_scr[...] = jnp.zeros_like(state_scr)

    t_idx = lax.broadcasted_iota(jnp.int32, (C, 2 * C), 0)
    s_idx = lax.broadcasted_iota(jnp.int32, (C, 2 * C), 1) & (C - 1)
    diag_mask = t_idx == s_idx
    level_masks = []
    for m in HG_LEVEL_SIZES:
        sh = (2 * m).bit_length() - 1
        level_masks.append(((t_idx & m) != 0) & ((s_idx & m) == 0) & ((t_idx >> sh) == (s_idx >> sh)))
    nw = nw_ref[...]
    row_idx = lax.broadcasted_iota(jnp.int32, (C, HG_KEY), 0)
    right_rows = [(row_idx & m) != 0 for m in HG_LEVEL_SIZES]

    def exponents(ci):
        slot = ci % 2
        lf = lf_ref[ci * C:(ci + 1) * C, :]
        hi = lf.astype(BF16)
        lo = (lf - hi.astype(F32)).astype(BF16)
        sums = _dot(a_ref[...], jnp.concatenate([hi, lo], axis=0))
        b = sums[0:C, :]

        def put(i, arg):
            e_scr[slot, i * C:(i + 1) * C, :] = jnp.exp2(arg)

        i_fine = 1
        for i, m in enumerate(HG_LEVEL_SIZES):
            if m >= HG_ROW_TILE:
                ref = jnp.concatenate([jnp.broadcast_to(b[j * 2 * m + m - 1:j * 2 * m + m, :], (2 * m, HG_DIM))
                                       for j in range(C // (2 * m))], axis=0)
                put(i, -jnp.abs(b - ref))
            else:
                put(i, sums[i_fine * C:(i_fine + 1) * C, :])
                i_fine += 1
        put(n_lev, b)
        put(n_lev + 1, b[C - 1:C, :] - b)

    def block_diag(x_a, x_b):
        return jnp.concatenate([jnp.concatenate([x_a, jnp.zeros_like(x_a)], axis=1),
                                jnp.concatenate([jnp.zeros_like(x_b), x_b], axis=1)], axis=0)

    def pair_scores(ci, hp):
        slot = ci % 2
        rows = slice(ci * C, (ci + 1) * C)
        lanes = [slice(h * HG_KEY, (h + 1) * HG_KEY) for h in (2 * hp, 2 * hp + 1)]
        q_bf = [q_ref[rows, hl] for hl in lanes]
        k_bf = [k_ref[rows, hl] for hl in lanes]
        q = [x.astype(F32) for x in q_bf]
        k = [x.astype(F32) for x in k_bf]
        scores = jnp.where(diag_mask, _dot_nt(jnp.concatenate(q_bf, axis=1), block_diag(*k_bf)), 0.0)
        for i, m in enumerate(HG_LEVEL_SIZES):
            zs = []
            for a, hl in enumerate(lanes):
                if m >= HG_ROW_TILE:
                    w = jnp.concatenate([(q[a] if j % 2 else k[a])[j * m:(j + 1) * m, :] for j in range(C // m)],
                                        axis=0)
                else:
                    w = jnp.where(right_rows[i], q[a], k[a])
                zs.append((w * e_scr[slot, i * C:(i + 1) * C, hl]).astype(BF16))
            scores = jnp.where(level_masks[i], _dot_nt(jnp.concatenate(zs, axis=1), block_diag(*zs)), scores)
        return scores

    def finish(ci, hp, scores):
        slot = ci % 2
        rows = slice(ci * C, (ci + 1) * C)
        heads = (2 * hp, 2 * hp + 1)
        lanes = [slice(h * HG_KEY, (h + 1) * HG_KEY) for h in heads]
        v_bf = [v_ref[rows, hl] for hl in lanes]
        o_pair = _dot(scores.astype(BF16), block_diag(*v_bf))
        for a, (h, hl) in enumerate(zip(heads, lanes)):
            q = q_ref[rows, hl].astype(F32)
            k = k_ref[rows, hl].astype(F32)
            e_b = e_scr[slot, n_lev * C:(n_lev + 1) * C, hl]
            e_suf = e_scr[slot, (n_lev + 1) * C:(n_lev + 2) * C, hl]
            e_last = e_scr[slot, (n_lev + 1) * C - 1:(n_lev + 1) * C, hl]
            st = state_scr[h]
            o = o_pair[:, a * HG_VAL:(a + 1) * HG_VAL] + _dot_nt((q * e_b).astype(BF16), st.astype(BF16))
            state_scr[h] = st * e_last + _dot_tn(v_bf[a], (k * e_suf).astype(BF16))
            ms = jnp.mean(o * o, axis=-1, keepdims=True)
            y = o * lax.rsqrt(ms + NORM_EPS) * nw * g_ref[rows, hl].astype(F32)
            o_ref[rows, hl] = y.astype(BF16)

    n_pairs = HG_HEADS // 2
    exponents(0)
    pending = []
    for ci in range(n_chunks):
        for hp in range(n_pairs):
            if hp == n_pairs // 2 and ci + 1 < n_chunks:
                exponents(ci + 1)
            pending.append((ci, hp, pair_scores(ci, hp)))
            if len(pending) > HG_SKEW:
                finish(*pending.pop(0))
    while pending:
        finish(*pending.pop(0))


def _hgrn(qa, ka, lf, va, ga, a_mat, hgrn_norm_w, B, S):
    ts = HG_TS
    steps = S // ts
    blk = pl.BlockSpec((ts, HG_DIM), lambda b, i: (b * steps + i, 0))
    rows_e = (len(HG_LEVEL_SIZES) + 2) * HG_CHUNK
    return pl.pallas_call(
        _hgrn_kernel,
        grid=(B, steps),
        in_specs=[blk, blk, blk, blk, blk,
                  pl.BlockSpec(a_mat.shape, lambda b, i: (0, 0)),
                  pl.BlockSpec((1, HG_VAL), lambda b, i: (0, 0))],
        out_specs=blk,
        out_shape=jax.ShapeDtypeStruct((B * S, HG_DIM), BF16),
        scratch_shapes=[pltpu.VMEM((HG_HEADS, HG_VAL, HG_KEY), F32),
                        pltpu.VMEM((2, rows_e, HG_DIM), F32)],
        compiler_params=pltpu.CompilerParams(dimension_semantics=("parallel", "arbitrary"),
                                             vmem_limit_bytes=VMEM_LIMIT_BYTES),
        name="hgrn2",
    )(qa, ka, lf, va, ga, a_mat, hgrn_norm_w)


def _attn_kernel(*refs, seq):
    n_in = 3 * sum(ATT_DILATIONS) + 1
    in_refs = refs[:n_in]
    ag_ref = in_refs[-1]
    out_ref = refs[n_in]
    o_scr = refs[n_in + 1:n_in + 1 + N_GROUPS]
    lse_scr = refs[n_in + 1 + N_GROUPS:n_in + 1 + 2 * N_GROUPS]
    n = ATT_STEPS

    lane = lax.broadcasted_iota(jnp.int32, (n, LANES), 1)
    head_a = lane < ATT_HEAD_DIM
    qi = lax.broadcasted_iota(jnp.int32, (n, 2 * n), 0)
    kj = lax.broadcasted_iota(jnp.int32, (n, 2 * n), 1)
    band_ok = (kj >= qi) & (kj <= qi + n)
    first_ok = (lax.broadcasted_iota(jnp.int32, (n, n), 1)
                <= lax.broadcasted_iota(jnp.int32, (n, n), 0))

    def row_sum_matrix(keys):
        rl = lax.broadcasted_iota(jnp.int32, (2 * keys, LANES), 0)
        ll = lax.broadcasted_iota(jnp.int32, (2 * keys, LANES), 1)
        return jnp.where((rl < keys) == (ll < ATT_HEAD_DIM), 1.0, 0.0).astype(BF16)

    ones_two = row_sum_matrix(2 * n)
    ones_one = row_sum_matrix(n)

    def scores(q_ref, k_ref, c):
        q = q_ref[c * n:(c + 1) * n, :]
        kk = k_ref[max(c - 1, 0) * n:(c + 1) * n, :]
        zero_q = jnp.zeros_like(q)
        return [_dot_nt(jnp.where(head_a, q, zero_q), kk), _dot_nt(jnp.where(head_a, zero_q, q), kk)]

    def finish(g, d, r, c, v_ref, s_pair):
        ok, ones = (first_ok, ones_one) if c == 0 else (band_ok, ones_two)
        vv = v_ref[max(c - 1, 0) * n:(c + 1) * n, :]
        ps, ms = [], []
        for s in s_pair:
            s = jnp.where(ok, s, NEG_BIG)
            sm = s if c == 0 else jnp.maximum(s[:, :n], s[:, n:])
            m = jnp.max(sm, axis=-1, keepdims=True)
            ps.append(jnp.exp(s - m).astype(BF16))
            ms.append(m)
        p = jnp.concatenate(ps, axis=1)
        zero_v = jnp.zeros_like(vv)
        va = lax.broadcasted_iota(jnp.int32, vv.shape, 1) < ATT_HEAD_DIM
        v2 = jnp.concatenate([jnp.where(va, vv, zero_v), jnp.where(va, zero_v, vv)], axis=0)
        ol = _dot(p, jnp.concatenate([v2, ones], axis=1))
        l = ol[:, LANES:]
        o = ol[:, :LANES] * (1.0 / l)
        lse = jnp.where(head_a, ms[0], ms[1]) + jnp.log(l)
        dst = pl.ds(c * n, n) if d == 1 else pl.ds(c * (n * d) + r, n, stride=d)
        o_scr[g][dst, :] = o
        lse_scr[g][dst, :] = lse

    pending = []
    pos = 0
    for g, d in enumerate(ATT_DILATIONS):
        q_refs = in_refs[pos:pos + d]
        k_refs = in_refs[pos + d:pos + 2 * d]
        v_refs = in_refs[pos + 2 * d:pos + 3 * d]
        pos += 3 * d
        for r in range(d):
            for c in range(seq // d // n):
                pending.append((g, d, r, c, v_refs[r], scores(q_refs[r], k_refs[r], c)))
                if len(pending) > ATT_SKEW:
                    finish(*pending.pop(0))
    while pending:
        finish(*pending.pop(0))

    rows_per = 256

    def merge(i, carry):
        rows = pl.ds(pl.multiple_of(i * rows_per, rows_per), rows_per)
        l1, l2, l3 = lse_scr[0][rows, :], lse_scr[1][rows, :], lse_scr[2][rows, :]
        mx = jnp.maximum(jnp.maximum(l1, l2), l3)
        w1, w2, w3 = jnp.exp(l1 - mx), jnp.exp(l2 - mx), jnp.exp(l3 - mx)
        num = w1 * o_scr[0][rows, :] + w2 * o_scr[1][rows, :] + w3 * o_scr[2][rows, :]
        ob = num / (w1 + w2 + w3) * ag_ref[rows, :].astype(F32)
        out_ref[rows, :] = ob.astype(BF16)
        return carry

    lax.fori_loop(0, seq // rows_per, merge, 0)


def _attention(qkv, ag, B, S):
    pairs = ATT_DIM // LANES
    operands, in_specs = [], []
    for g, d in enumerate(ATT_DILATIONS):
        m_len = S // d
        for arr in qkv[g]:
            view = arr.reshape(B, m_len, d * ATT_DIM)
            for r in range(d):
                operands.append(view)
                in_specs.append(pl.BlockSpec((None, m_len, LANES),
                                             functools.partial(lambda b, hp, r: (b, 0, r * pairs + hp), r=r)))
    operands.append(ag.reshape(B, S, ATT_DIM))
    nat = pl.BlockSpec((None, S, LANES), lambda b, hp: (b, 0, hp))
    in_specs.append(nat)
    out = pl.pallas_call(
        functools.partial(_attn_kernel, seq=S),
        grid=(B, pairs),
        in_specs=in_specs,
        out_specs=nat,
        out_shape=jax.ShapeDtypeStruct((B, S, ATT_DIM), BF16),
        scratch_shapes=[pltpu.VMEM((S, LANES), F32)] * (2 * N_GROUPS),
        compiler_params=pltpu.CompilerParams(dimension_semantics=("parallel", "parallel"),
                                             vmem_limit_bytes=VMEM_LIMIT_BYTES),
        name="dilated_attn",
    )(*operands)
    return out.reshape(B * S, ATT_DIM)


def _out_kernel(x_ref, oa_ref, ob_ref, gates_ref, wa_ref, wb_ref, wo_ref, fnw_ref, out_ref):
    ya = _dot(oa_ref[...], wa_ref[...])
    yb = _dot(ob_ref[...], wb_ref[...])
    g_a = gates_ref[:, :D_MODEL].astype(F32)
    g_b = gates_ref[:, D_MODEL:].astype(F32)
    merged = (g_a * ya + g_b * yb).astype(BF16)
    y = x_ref[...] + _dot(merged, wo_ref[...])
    ms = jnp.mean(y * y, axis=-1, keepdims=True)
    out_ref[...] = y * lax.rsqrt(ms + NORM_EPS) * fnw_ref[...]


def _output(x2, oa, ob, gates, wa, wb, wo, fnw):
    T = x2.shape[0]
    tm = OUT_TM
    row = lambda n: pl.BlockSpec((tm, n), lambda i: (i, 0))
    const = lambda shape: pl.BlockSpec(shape, lambda i: (0, 0))
    return pl.pallas_call(
        _out_kernel,
        grid=(T // tm,),
        in_specs=[row(D_MODEL), row(HG_DIM), row(ATT_DIM), row(2 * D_MODEL),
                  const((HG_DIM, D_MODEL)), const((ATT_DIM, D_MODEL)), const((D_MODEL, D_MODEL)),
                  const((1, D_MODEL))],
        out_specs=row(D_MODEL),
        out_shape=jax.ShapeDtypeStruct((T, D_MODEL), F32),
        compiler_params=pltpu.CompilerParams(dimension_semantics=("parallel",),
                                             vmem_limit_bytes=VMEM_LIMIT_BYTES),
        name="merge_out",
    )(x2, oa, ob, gates, wa, wb, wo, fnw)


def kernel(x, positions, norm_w, w_in, lb_logits, hgrn_norm_w, w_branch_a, w_branch_b, w_out, final_norm_w):
    B, S, D = x.shape
    assert D == D_MODEL and norm_w.shape[0] == 1 and lb_logits.shape == (2, HG_DIM)
    assert S % (ATT_STEPS * max(ATT_DILATIONS)) == 0 and S % HG_TS == 0 and (B * S) % OUT_TM == 0
    T = B * S
    x2 = x.reshape(T, D)
    pos_b = jnp.broadcast_to(positions.reshape(T, 1), (T, LANES))
    inv_freq = ROPE_THETA ** (-jnp.arange(0, ATT_HEAD_DIM, 2, dtype=F32) / ATT_HEAD_DIM)
    invf = jnp.tile(inv_freq, LANES // (ATT_HEAD_DIM // 2)).reshape(1, LANES)

    outs = _inproj(x2, pos_b, norm_w.reshape(1, D), w_in[0].astype(BF16), lb_logits.astype(F32), invf)
    qa, ka, lf, va, ga = outs[:5]
    qkv = [tuple(outs[5 + 3 * g:8 + 3 * g]) for g in range(N_GROUPS)]
    ag, gates = outs[14], outs[15]

    a_mat = jnp.asarray(_hgrn_sum_matrix(), BF16)
    oa = _hgrn(qa, ka, lf, va, ga, a_mat, hgrn_norm_w.reshape(1, HG_VAL).astype(F32), B, S)
    ob = _attention(qkv, ag, B, S)
    out = _output(x2, oa, ob, gates, w_branch_a[0].astype(BF16), w_branch_b[0].astype(BF16),
                  w_out[0].astype(BF16), final_norm_w.reshape(1, D).astype(F32))
    return out.reshape(B, S, D)
```

```python
import functools

import numpy as np
import jax
import jax.numpy as jnp
from jax import lax
from jax.experimental import pallas as pl
from jax.experimental.pallas import tpu as pltpu

D_MODEL = 1024
HG_HEADS = 8
HG_KEY = 128
HG_VAL = 128
HG_DIM = HG_HEADS * HG_KEY
ATT_DILATIONS = (1, 4, 16)
ATT_STEPS = 128
N_GROUPS = 3
ATT_HEADS = 8
ATT_HEAD_DIM = 64
ATT_DIM = ATT_HEADS * ATT_HEAD_DIM
ROPE_THETA = 10000.0
NORM_EPS = 1e-6
LOG2_E = 1.4426950408889634
C_HQ, C_HF, C_HI, C_HG = 0, 1024, 2048, 3072
C_AQ, C_AK, C_AV = 4096, 4096 + 1536, 4096 + 3072
C_AG = 8704
C_GATES = 9216
IN_COLS = 11264

LANES = 128
VMEM_LIMIT_BYTES = 56 * 1024 * 1024

PAR_NORM_W, PAR_LB_LOGITS, PAR_INV_FREQ, PAR_ROWS = 0, 1, 3, 8

IN_TM = 256
IN_SEG = 512
PERM_STRIDE = 4
HG_CHUNK = 64
HG_TS = 256
HG_LEVEL_SIZES = tuple(HG_CHUNK >> (l + 1) for l in range(HG_CHUNK.bit_length() - 1))
HG_ROW_TILE = 8
HG_SKEW = 1
OUT_TM = 512
ATT_SKEW = 3
NEG_BIG = -1e30

F32 = jnp.float32
BF16 = jnp.bfloat16


def _dot(a, b):
    return jnp.dot(a, b, preferred_element_type=F32)


def _dot_nt(a, b):
    return lax.dot_general(a, b, (((1,), (1,)), ((), ())), preferred_element_type=F32)


def _dot_tn(a, b):
    return lax.dot_general(a, b, (((0,), (0,)), ((), ())), preferred_element_type=F32)


def _sigmoid_pair(z):
    half_t = 0.5 * jnp.tanh(0.5 * z)
    return 0.5 + half_t, 0.5 - half_t


def _sigmoid(z):
    return _sigmoid_pair(z)[0]


def _inproj_kernel(x0_ref, xn_ref, pos_ref, w_ref, par_ref,
                   qa_ref, ka_ref, lf_ref, va_ref, ga_ref,
                   q1_ref, k1_ref, v1_ref, q2_ref, k2_ref, v2_ref, q3_ref, k3_ref, v3_ref,
                   ag_ref, gates_ref, perm_scr, h_scr):
    tm = xn_ref.shape[0]
    nw_ref = par_ref.at[PAR_NORM_W:PAR_NORM_W + 1, :]
    lbl_ref = par_ref.at[PAR_LB_LOGITS:PAR_LB_LOGITS + 2, :]
    invf_ref = par_ref.at[PAR_INV_FREQ:PAR_INV_FREQ + 1, 0:LANES]
    step = pl.program_id(0)
    slot = step % 2

    def normed(x_ref):
        x = x_ref[...]
        ms = jnp.mean(x * x, axis=-1, keepdims=True)
        return (x * lax.rsqrt(ms + NORM_EPS) * nw_ref[...]).astype(BF16)

    @pl.when(step == 0)
    def _():
        h_scr[0] = normed(x0_ref)

    def proj(c0, n):
        return _dot(h_scr[slot], w_ref[:, c0:c0 + n])

    lbl = lbl_ref[...]
    lmax = jnp.max(lbl, axis=0, keepdims=True)
    lexp = jnp.exp(lbl - lmax)
    lb = lexp[0:1, :] / jnp.sum(lexp, axis=0, keepdims=True)

    ang = pos_ref[...].astype(F32) * invf_ref[...]
    cos = jnp.cos(ang)
    sin = jnp.sin(ang)
    lane = lax.broadcasted_iota(jnp.int32, ang.shape, 1)
    first = (lane & (ATT_HEAD_DIM - 1)) < (ATT_HEAD_DIM // 2)
    sin_signed = jnp.where(first, -sin, sin)

    def rope(zs):
        rot = jnp.where(first, pltpu.roll(zs, LANES - ATT_HEAD_DIM // 2, 1), pltpu.roll(zs, ATT_HEAD_DIM // 2, 1))
        return zs * cos + rot * sin_signed

    def silu_to(ref, sl):
        def epi(z):
            ref[:, sl] = (z * _sigmoid(z)).astype(BF16)
        return epi

    def cast_to(ref, sl):
        def epi(z):
            ref[:, sl] = z.astype(BF16)
        return epi

    def sigmoid_to(ref, sl):
        def epi(z):
            ref[:, sl] = _sigmoid(z).astype(BF16)
        return epi

    def forget_to(sl):
        def epi(z):
            lbj = lb[:, sl]
            sig, sig_neg = _sigmoid_pair(z)
            f = lbj + (1.0 - lbj) * sig
            lf_ref[:, sl] = jnp.log(f) * LOG2_E
            ka_ref[:, sl] = ((1.0 - lbj) * sig_neg).astype(BF16)
        return epi

    def attn_to(ref, d, slab, rotate, scale):
        def epi(z):
            for s in range(ATT_DIM // LANES):
                ls = slice(s * LANES, (s + 1) * LANES)
                val = z[:, ls]
                if rotate:
                    val = rope(val)
                if scale != 1.0:
                    val = val * scale
                if d == 1:
                    ref[:, ls] = val.astype(BF16)
                    continue
                perm_scr[0, slab] = val
                if d > PERM_STRIDE:
                    for r1 in range(PERM_STRIDE):
                        perm_scr[1, slab, r1 * (tm // PERM_STRIDE):(r1 + 1) * (tm // PERM_STRIDE), :] = (
                            perm_scr[0, slab, pl.ds(r1, tm // PERM_STRIDE, stride=PERM_STRIDE), :])
                for r in range(d):
                    cols = slice(r * ATT_DIM + s * LANES, r * ATT_DIM + (s + 1) * LANES)
                    if d > PERM_STRIDE:
                        r1, r2 = r % PERM_STRIDE, r // PERM_STRIDE
                        rows = pl.ds(r1 * (tm // PERM_STRIDE) + r2, tm // d, stride=PERM_STRIDE)
                        ref[:, cols] = perm_scr[1, slab, rows, :].astype(BF16)
                    else:
                        ref[:, cols] = perm_scr[0, slab, pl.ds(r, tm // d, stride=d), :].astype(BF16)
        return epi

    halves = [slice(j * IN_SEG, (j + 1) * IN_SEG) for j in range(HG_DIM // IN_SEG)]
    segments = []
    for j, sl in enumerate(halves):
        segments += [(C_HQ + j * IN_SEG, silu_to(qa_ref, sl)), (C_HF + j * IN_SEG, forget_to(sl)),
                     (C_HG + j * IN_SEG, silu_to(ga_ref, sl))]
    for j in range(2 * D_MODEL // IN_SEG):
        segments.append((C_GATES + j * IN_SEG, sigmoid_to(gates_ref, slice(j * IN_SEG, (j + 1) * IN_SEG))))
    segments.append((C_AG, silu_to(ag_ref, slice(0, ATT_DIM))))
    att_refs = ((q1_ref, k1_ref, v1_ref), (q2_ref, k2_ref, v2_ref), (q3_ref, k3_ref, v3_ref))
    for g in reversed(range(N_GROUPS)):
        d = ATT_DILATIONS[g]
        segments += [(C_AQ + g * ATT_DIM, attn_to(att_refs[g][0], d, 0, True, ATT_HEAD_DIM ** -0.5 * LOG2_E)),
                     (C_AK + g * ATT_DIM, attn_to(att_refs[g][1], d, 1, True, 1.0)),
                     (C_AV + g * ATT_DIM, attn_to(att_refs[g][2], d, 2, False, 1.0))]
    for j, sl in enumerate(halves):
        segments.append((C_HI + j * IN_SEG, cast_to(va_ref, sl)))

    z = proj(segments[0][0], IN_SEG)
    h_scr[1 - slot] = normed(xn_ref)
    for i, (_, epilogue) in enumerate(segments):
        z_next = proj(segments[i + 1][0], IN_SEG) if i + 1 < len(segments) else None
        epilogue(z)
        z = z_next


def _inproj(x2, pos_b, w_in_bf, params):
    T = x2.shape[0]
    tm = IN_TM
    steps = T // tm
    row = lambda n: pl.BlockSpec((tm, n), lambda i: (i, 0))
    const = lambda shape: pl.BlockSpec(shape, lambda i: (0, 0))
    out_shapes = ([jax.ShapeDtypeStruct((T, HG_DIM), BF16),
                   jax.ShapeDtypeStruct((T, HG_DIM), BF16),
                   jax.ShapeDtypeStruct((T, HG_DIM), F32),
                   jax.ShapeDtypeStruct((T, HG_DIM), BF16),
                   jax.ShapeDtypeStruct((T, HG_DIM), BF16)]
                  + [jax.ShapeDtypeStruct((T // d, d * ATT_DIM), BF16) for d in ATT_DILATIONS for _ in range(3)]
                  + [jax.ShapeDtypeStruct((T, ATT_DIM), BF16),
                     jax.ShapeDtypeStruct((T, 2 * D_MODEL), BF16)])
    out_specs = ([row(HG_DIM)] * 5
                 + [pl.BlockSpec((tm // d, d * ATT_DIM), lambda i: (i, 0)) for d in ATT_DILATIONS for _ in range(3)]
                 + [row(ATT_DIM), row(2 * D_MODEL)])
    return pl.pallas_call(
        _inproj_kernel,
        grid=(steps,),
        in_specs=[const((tm, D_MODEL)),
                  pl.BlockSpec((tm, D_MODEL), lambda i: (jnp.minimum(i + 1, steps - 1), 0)),
                  row(LANES),
                  pl.BlockSpec((D_MODEL, IN_COLS), lambda i: (0, 0), pipeline_mode=pl.Buffered(1)),
                  const((PAR_ROWS, D_MODEL))],
        out_specs=out_specs,
        out_shape=out_shapes,
        scratch_shapes=[pltpu.VMEM((2, 3, tm, LANES), F32), pltpu.VMEM((2, tm, D_MODEL), BF16)],
        compiler_params=pltpu.CompilerParams(dimension_semantics=("arbitrary",),
                                             vmem_limit_bytes=VMEM_LIMIT_BYTES),
        name="inproj",
    )(x2, x2, pos_b, w_in_bf, params)


def _hgrn_sum_matrix():
    C = HG_CHUNK
    fine = [m for m in HG_LEVEL_SIZES if m < HG_ROW_TILE]
    A = np.zeros(((1 + len(fine)) * C, C), np.float32)
    for t in range(C):
        A[t, :t + 1] = 1.0
    for i, m in enumerate(fine):
        for t in range(C):
            start = (t // m) * m
            if t & m:
                A[(1 + i) * C + t, start:t + 1] = 1.0
            else:
                A[(1 + i) * C + t, t + 1:start + m] = 1.0
    return np.concatenate([A, A], axis=1)


def _hgrn_kernel(q_ref, k_ref, lf_ref, v_ref, g_ref, a_ref, nw_ref, o_ref, state_scr, e_scr):
    C = HG_CHUNK
    n_lev = len(HG_LEVEL_SIZES)
    n_chunks = HG_TS // C

    @pl.when(pl.program_id(1) == 0)
    def _():
        state_scr[...] = jnp.zeros_like(state_scr)

    t_idx = lax.broadcasted_iota(jnp.int32, (C, 2 * C), 0)
    s_idx = lax.broadcasted_iota(jnp.int32, (C, 2 * C), 1) & (C - 1)
    diag_mask = t_idx == s_idx
    level_masks = []
    for m in HG_LEVEL_SIZES:
        sh = (2 * m).bit_length() - 1
        level_masks.append(((t_idx & m) != 0) & ((s_idx & m) == 0) & ((t_idx >> sh) == (s_idx >> sh)))
    row_idx = lax.broadcasted_iota(jnp.int32, (C, HG_KEY), 0)
    right_rows = [(row_idx & m) != 0 for m in HG_LEVEL_SIZES]

    def exponents(ci):
        slot = ci % 2
        lf = lf_ref[ci * C:(ci + 1) * C, :]
        hi = lf.astype(BF16)
        lo = (lf - hi.astype(F32)).astype(BF16)
        sums = _dot(a_ref[...], jnp.concatenate([hi, lo], axis=0))
        b = sums[0:C, :]

        def put(i, arg):
            e_scr[slot, i * C:(i + 1) * C, :] = jnp.exp2(arg)

        i_fine = 1
        for i, m in enumerate(HG_LEVEL_SIZES):
            if m >= HG_ROW_TILE:
                parts = []
                for j in range(C // (2 * m)):
                    ref = jnp.broadcast_to(b[j * 2 * m + m - 1:j * 2 * m + m, :], (m, HG_DIM))
                    parts += [ref - b[j * 2 * m:j * 2 * m + m, :], b[j * 2 * m + m:(j + 1) * 2 * m, :] - ref]
                put(i, jnp.concatenate(parts, axis=0))
            else:
                put(i, sums[i_fine * C:(i_fine + 1) * C, :])
                i_fine += 1
        put(n_lev, b)
        put(n_lev + 1, b[C - 1:C, :] - b)

    def block_diag(x_a, x_b):
        return jnp.concatenate([jnp.concatenate([x_a, jnp.zeros_like(x_a)], axis=1),
                                jnp.concatenate([jnp.zeros_like(x_b), x_b], axis=1)], axis=0)

    def pair_scores(ci, hp):
        slot = ci % 2
        rows = slice(ci * C, (ci + 1) * C)
        lanes = [slice(h * HG_KEY, (h + 1) * HG_KEY) for h in (2 * hp, 2 * hp + 1)]
        q_bf = [q_ref[rows, hl] for hl in lanes]
        k_bf = [k_ref[rows, hl] for hl in lanes]
        q = [x.astype(F32) for x in q_bf]
        k = [x.astype(F32) for x in k_bf]
        scores = jnp.where(diag_mask, _dot_nt(jnp.concatenate(q_bf, axis=1), block_diag(*k_bf)), 0.0)
        for i, m in enumerate(HG_LEVEL_SIZES):
            zs = []
            for a, hl in enumerate(lanes):
                if m >= HG_ROW_TILE:
                    w = jnp.concatenate([(q[a] if j % 2 else k[a])[j * m:(j + 1) * m, :] for j in range(C // m)],
                                        axis=0)
                else:
                    w = jnp.where(right_rows[i], q[a], k[a])
                zs.append((w * e_scr[slot, i * C:(i + 1) * C, hl]).astype(BF16))
            scores = jnp.where(level_masks[i], _dot_nt(jnp.concatenate(zs, axis=1), block_diag(*zs)), scores)
        q_dec = [(q[a] * e_scr[slot, n_lev * C:(n_lev + 1) * C, hl]).astype(BF16) for a, hl in enumerate(lanes)]
        k_dec = [(k[a] * e_scr[slot, (n_lev + 1) * C:(n_lev + 2) * C, hl]).astype(BF16) for a, hl in enumerate(lanes)]
        return scores, q_dec, k_dec

    def finish(ci, hp, stage1):
        scores, q_dec, k_dec = stage1
        slot = ci % 2
        rows = slice(ci * C, (ci + 1) * C)
        heads = (2 * hp, 2 * hp + 1)
        lanes = [slice(h * HG_KEY, (h + 1) * HG_KEY) for h in heads]
        v_bf = [v_ref[rows, hl] for hl in lanes]
        o_pair = _dot(scores.astype(BF16), block_diag(*v_bf))
        for a, (h, hl) in enumerate(zip(heads, lanes)):
            e_last = e_scr[slot, (n_lev + 1) * C - 1:(n_lev + 1) * C, hl]
            st = state_scr[h]
            o = o_pair[:, a * HG_VAL:(a + 1) * HG_VAL] + _dot_nt(q_dec[a], st.astype(BF16))
            state_scr[h] = st * e_last + _dot_tn(v_bf[a], k_dec[a])
            ms = jnp.mean(o * o, axis=-1, keepdims=True)
            y = o * lax.rsqrt(ms + NORM_EPS) * nw_ref[0:1, 0:HG_VAL] * g_ref[rows, hl].astype(F32)
            o_ref[rows, hl] = y.astype(BF16)

    n_pairs = HG_HEADS // 2
    exponents(0)
    pending = []
    for ci in range(n_chunks):
        for hp in range(n_pairs):
            if hp == n_pairs // 2 and ci + 1 < n_chunks:
                exponents(ci + 1)
            pending.append((ci, hp, pair_scores(ci, hp)))
            if len(pending) > HG_SKEW:
                finish(*pending.pop(0))
    while pending:
        finish(*pending.pop(0))


def _hgrn(qa, ka, lf, va, ga, a_mat, hgrn_w, B, S):
    ts = HG_TS
    steps = S // ts
    blk = pl.BlockSpec((ts, HG_DIM), lambda b, i: (b * steps + i, 0))
    rows_e = (len(HG_LEVEL_SIZES) + 2) * HG_CHUNK
    return pl.pallas_call(
        _hgrn_kernel,
        grid=(B, steps),
        in_specs=[blk, blk, blk, blk, blk,
                  pl.BlockSpec(a_mat.shape, lambda b, i: (0, 0)),
                  pl.BlockSpec((PAR_ROWS, HG_DIM), lambda b, i: (0, 0))],
        out_specs=blk,
        out_shape=jax.ShapeDtypeStruct((B * S, HG_DIM), BF16),
        scratch_shapes=[pltpu.VMEM((HG_HEADS, HG_VAL, HG_KEY), F32),
                        pltpu.VMEM((2, rows_e, HG_DIM), F32)],
        compiler_params=pltpu.CompilerParams(dimension_semantics=("parallel", "arbitrary"),
                                             vmem_limit_bytes=VMEM_LIMIT_BYTES),
        name="hgrn2",
    )(qa, ka, lf, va, ga, a_mat, hgrn_w)


def _attn_kernel(*refs, seq):
    n_in = 3 * sum(ATT_DILATIONS) + 1
    in_refs = refs[:n_in]
    ag_ref = in_refs[-1]
    out_ref = refs[n_in]
    o_scr = refs[n_in + 1:n_in + 1 + N_GROUPS]
    lse_scr = refs[n_in + 1 + N_GROUPS:n_in + 1 + 2 * N_GROUPS]
    n = ATT_STEPS

    lane = lax.broadcasted_iota(jnp.int32, (n, LANES), 1)
    head_a = lane < ATT_HEAD_DIM
    qi = lax.broadcasted_iota(jnp.int32, (n, 2 * n), 0)
    kj = lax.broadcasted_iota(jnp.int32, (n, 2 * n), 1)
    band_ok = (kj >= qi) & (kj <= qi + n)
    first_ok = (lax.broadcasted_iota(jnp.int32, (n, n), 1)
                <= lax.broadcasted_iota(jnp.int32, (n, n), 0))

    def scores(q_ref, k_ref, c):
        q = q_ref[c * n:(c + 1) * n, :]
        kk = k_ref[max(c - 1, 0) * n:(c + 1) * n, :]
        zero_q = jnp.zeros_like(q)
        return [_dot_nt(jnp.where(head_a, q, zero_q), kk), _dot_nt(jnp.where(head_a, zero_q, q), kk)]

    def finish(g, d, r, c, v_ref, s_pair):
        ok = first_ok if c == 0 else band_ok
        vv = v_ref[max(c - 1, 0) * n:(c + 1) * n, :]
        ps, ms = [], []
        for s in s_pair:
            s = jnp.where(ok, s, NEG_BIG)
            sm = s if c == 0 else jnp.maximum(s[:, :n], s[:, n:])
            m = jnp.max(sm, axis=-1, keepdims=True)
            ps.append(jnp.exp2(s - m).astype(BF16))
            ms.append(m)
        ol = _dot(jnp.concatenate(ps, axis=0), jnp.concatenate([vv, jnp.ones_like(vv)], axis=1))
        l = jnp.where(head_a, ol[:n, LANES:], ol[n:, LANES:])
        o = jnp.where(head_a, ol[:n, :LANES], ol[n:, :LANES]) * (1.0 / l)
        lse = jnp.where(head_a, ms[0], ms[1]) + jnp.log2(l)
        dst = pl.ds(c * n, n) if d == 1 else pl.ds(c * (n * d) + r, n, stride=d)
        o_scr[g][dst, :] = o
        lse_scr[g][dst, :] = lse

    pending = []
    pos = 0
    for g, d in enumerate(ATT_DILATIONS):
        q_refs = in_refs[pos:pos + d]
        k_refs = in_refs[pos + d:pos + 2 * d]
        v_refs = in_refs[pos + 2 * d:pos + 3 * d]
        pos += 3 * d
        for r in range(d):
            for c in range(seq // d // n):
                pending.append((g, d, r, c, v_refs[r], scores(q_refs[r], k_refs[r], c)))
                if len(pending) > ATT_SKEW:
                    finish(*pending.pop(0))
    while pending:
        finish(*pending.pop(0))

    rows_per = 256

    def merge(i, carry):
        rows = pl.ds(pl.multiple_of(i * rows_per, rows_per), rows_per)
        l1, l2, l3 = lse_scr[0][rows, :], lse_scr[1][rows, :], lse_scr[2][rows, :]
        mx = jnp.maximum(jnp.maximum(l1, l2), l3)
        w1, w2, w3 = jnp.exp2(l1 - mx), jnp.exp2(l2 - mx), jnp.exp2(l3 - mx)
        num = w1 * o_scr[0][rows, :] + w2 * o_scr[1][rows, :] + w3 * o_scr[2][rows, :]
        ob = num / (w1 + w2 + w3) * ag_ref[rows, :].astype(F32)
        out_ref[rows, :] = ob.astype(BF16)
        return carry

    lax.fori_loop(0, seq // rows_per, merge, 0)


def _attention(qkv, ag, B, S):
    pairs = ATT_DIM // LANES
    operands, in_specs = [], []
    for g, d in enumerate(ATT_DILATIONS):
        m_len = S // d
        for arr in qkv[g]:
            view = arr.reshape(B, m_len, d * ATT_DIM)
            for r in range(d):
                operands.append(view)
                in_specs.append(pl.BlockSpec((None, m_len, LANES),
                                             functools.partial(lambda b, hp, r: (b, 0, r * pairs + hp), r=r)))
    operands.append(ag.reshape(B, S, ATT_DIM))
    nat = pl.BlockSpec((None, S, LANES), lambda b, hp: (b, 0, hp))
    in_specs.append(nat)
    out = pl.pallas_call(
        functools.partial(_attn_kernel, seq=S),
        grid=(B, pairs),
        in_specs=in_specs,
        out_specs=nat,
        out_shape=jax.ShapeDtypeStruct((B, S, ATT_DIM), BF16),
        scratch_shapes=[pltpu.VMEM((S, LANES), F32)] * (2 * N_GROUPS),
        compiler_params=pltpu.CompilerParams(dimension_semantics=("parallel", "parallel"),
                                             vmem_limit_bytes=VMEM_LIMIT_BYTES),
        name="dilated_attn",
    )(*operands)
    return out.reshape(B * S, ATT_DIM)


def _out_kernel(x_ref, oa_ref, ob_ref, gates_ref, wa_ref, wb_ref, wo_ref, fnw_ref, out_ref):
    ya = _dot(oa_ref[...], wa_ref[...])
    yb = _dot(ob_ref[...], wb_ref[...])
    g_a = gates_ref[:, :D_MODEL].astype(F32)
    g_b = gates_ref[:, D_MODEL:].astype(F32)
    merged = (g_a * ya + g_b * yb).astype(BF16)
    y = x_ref[...] + _dot(merged, wo_ref[...])
    ms = jnp.mean(y * y, axis=-1, keepdims=True)
    out_ref[...] = y * lax.rsqrt(ms + NORM_EPS) * fnw_ref[0:1, :]


def _output(x2, oa, ob, gates, wa, wb, wo, fnw):
    T = x2.shape[0]
    tm = OUT_TM
    row = lambda n: pl.BlockSpec((tm, n), lambda i: (i, 0))
    const = lambda shape: pl.BlockSpec(shape, lambda i: (0, 0))
    return pl.pallas_call(
        _out_kernel,
        grid=(T // tm,),
        in_specs=[row(D_MODEL), row(HG_DIM), row(ATT_DIM), row(2 * D_MODEL),
                  const((HG_DIM, D_MODEL)), const((ATT_DIM, D_MODEL)), const((D_MODEL, D_MODEL)),
                  const((PAR_ROWS, D_MODEL))],
        out_specs=row(D_MODEL),
        out_shape=jax.ShapeDtypeStruct((T, D_MODEL), F32),
        compiler_params=pltpu.CompilerParams(dimension_semantics=("parallel",),
                                             vmem_limit_bytes=VMEM_LIMIT_BYTES),
        name="merge_out",
    )(x2, oa, ob, gates, wa, wb, wo, fnw)


def kernel(x, positions, norm_w, w_in, lb_logits, hgrn_norm_w, w_branch_a, w_branch_b, w_out, final_norm_w):
    B, S, D = x.shape
    assert D == D_MODEL and norm_w.shape[0] == 1 and lb_logits.shape == (2, HG_DIM)
    assert S % (ATT_STEPS * max(ATT_DILATIONS)) == 0 and S % HG_TS == 0 and (B * S) % OUT_TM == 0
    T = B * S
    x2 = x.reshape(T, D)
    pos_b = jnp.broadcast_to(positions.reshape(T, 1), (T, LANES))
    inv_freq = ROPE_THETA ** (-jnp.arange(0, ATT_HEAD_DIM, 2, dtype=F32) / ATT_HEAD_DIM)
    invf = jnp.tile(inv_freq, D_MODEL // (ATT_HEAD_DIM // 2)).reshape(1, D_MODEL)

    params = jnp.concatenate([norm_w.reshape(1, D).astype(F32), lb_logits.astype(F32), invf,
                              jnp.zeros((PAR_ROWS - 4, D), F32)], axis=0)
    outs = _inproj(x2, pos_b, w_in[0].astype(BF16), params)
    qa, ka, lf, va, ga = outs[:5]
    qkv = [tuple(outs[5 + 3 * g:8 + 3 * g]) for g in range(N_GROUPS)]
    ag, gates = outs[14], outs[15]

    a_mat = jnp.asarray(_hgrn_sum_matrix(), BF16)
    hgrn_w = jnp.broadcast_to(jnp.tile(hgrn_norm_w.reshape(1, HG_VAL).astype(F32), (1, HG_HEADS)), (PAR_ROWS, HG_DIM))
    oa = _hgrn(qa, ka, lf, va, ga, a_mat, hgrn_w, B, S)
    ob = _attention(qkv, ag, B, S)
    out = _output(x2, oa, ob, gates, w_branch_a[0].astype(BF16), w_branch_b[0].astype(BF16),
                  w_out[0].astype(BF16), jnp.broadcast_to(final_norm_w.reshape(1, D).astype(F32), (PAR_ROWS, D)))
    return out.reshape(B, S, D)
```

```python
import functools

import numpy as np
import jax
import jax.numpy as jnp
from jax import lax
from jax.experimental import pallas as pl
from jax.experimental.pallas import tpu as pltpu

D_MODEL = 1024
HG_HEADS = 8
HG_KEY = 128
HG_VAL = 128
HG_DIM = HG_HEADS * HG_KEY
ATT_DILATIONS = (1, 4, 16)
ATT_STEPS = 128
N_GROUPS = 3
ATT_HEADS = 8
ATT_HEAD_DIM = 64
ATT_DIM = ATT_HEADS * ATT_HEAD_DIM
ROPE_THETA = 10000.0
NORM_EPS = 1e-6
LOG2_E = 1.4426950408889634
C_HQ, C_HF, C_HI, C_HG = 0, 1024, 2048, 3072
C_AQ, C_AK, C_AV = 4096, 4096 + 1536, 4096 + 3072
C_AG = 8704
C_GATES = 9216
IN_COLS = 11264

LANES = 128
VMEM_LIMIT_BYTES = 56 * 1024 * 1024

PAR_ROWS = 8

IN_TM = 256
IN_SEG = 512
PERM_STRIDE = 4
HG_CHUNK = 64
HG_TS = 256
HG_LEVEL_SIZES = tuple(HG_CHUNK >> (l + 1) for l in range(HG_CHUNK.bit_length() - 1))
HG_ROW_TILE = 8
HG_SKEW = 1
OUT_TM = 512
ATT_SKEW = 3
NEG_BIG = -1e30

F32 = jnp.float32
BF16 = jnp.bfloat16


def _dot(a, b):
    return jnp.dot(a, b, preferred_element_type=F32)


def _dot_nt(a, b):
    return lax.dot_general(a, b, (((1,), (1,)), ((), ())), preferred_element_type=F32)


def _dot_tn(a, b):
    return lax.dot_general(a, b, (((0,), (0,)), ((), ())), preferred_element_type=F32)


def _sigmoid_pair(z):
    half_t = 0.5 * jnp.tanh(0.5 * z)
    return 0.5 + half_t, 0.5 - half_t


def _sigmoid(z):
    return _sigmoid_pair(z)[0]


def _inproj_kernel(x0_ref, xn_ref, pos_ref, nw_ref, w_ref, lbl_ref, invf_ref,
                   qa_ref, ka_ref, lf_ref, va_ref, ga_ref,
                   q1_ref, k1_ref, v1_ref, q2_ref, k2_ref, v2_ref, q3_ref, k3_ref, v3_ref,
                   ag_ref, gates_ref, perm_scr, h_scr):
    tm = xn_ref.shape[0]
    step = pl.program_id(0)
    slot = step % 2

    def normed(x_ref):
        x = x_ref[...]
        ms = jnp.mean(x * x, axis=-1, keepdims=True)
        return (x * lax.rsqrt(ms + NORM_EPS) * nw_ref[...]).astype(BF16)

    @pl.when(step == 0)
    def _():
        h_scr[0] = normed(x0_ref)

    def proj(c0, n):
        return _dot(h_scr[slot], w_ref[:, c0:c0 + n])

    lbl = lbl_ref[...]
    lmax = jnp.max(lbl, axis=0, keepdims=True)
    lexp = jnp.exp(lbl - lmax)
    lb = lexp[0:1, :] / jnp.sum(lexp, axis=0, keepdims=True)

    ang = pos_ref[...].astype(F32) * invf_ref[...]
    cos = jnp.cos(ang)
    sin = jnp.sin(ang)
    lane = lax.broadcasted_iota(jnp.int32, ang.shape, 1)
    first = (lane & (ATT_HEAD_DIM - 1)) < (ATT_HEAD_DIM // 2)
    sin_signed = jnp.where(first, -sin, sin)

    def rope(zs):
        rot = jnp.where(first, pltpu.roll(zs, LANES - ATT_HEAD_DIM // 2, 1), pltpu.roll(zs, ATT_HEAD_DIM // 2, 1))
        return zs * cos + rot * sin_signed

    def silu_to(ref, sl):
        def epi(z):
            ref[:, sl] = (z * _sigmoid(z)).astype(BF16)
        return epi

    def cast_to(ref, sl):
        def epi(z):
            ref[:, sl] = z.astype(BF16)
        return epi

    def sigmoid_to(ref, sl):
        def epi(z):
            ref[:, sl] = _sigmoid(z).astype(BF16)
        return epi

    def forget_to(sl):
        def epi(z):
            lbj = lb[:, sl]
            sig, sig_neg = _sigmoid_pair(z)
            f = lbj + (1.0 - lbj) * sig
            lf_ref[:, sl] = jnp.log(f) * LOG2_E
            ka_ref[:, sl] = ((1.0 - lbj) * sig_neg).astype(BF16)
        return epi

    def attn_to(ref, d, slab, rotate, scale):
        def epi(z):
            for s in range(ATT_DIM // LANES):
                ls = slice(s * LANES, (s + 1) * LANES)
                val = z[:, ls]
                if rotate:
                    val = rope(val)
                if scale != 1.0:
                    val = val * scale
                if d == 1:
                    ref[:, ls] = val.astype(BF16)
                    continue
                perm_scr[0, slab] = val
                if d > PERM_STRIDE:
                    for r1 in range(PERM_STRIDE):
                        perm_scr[1, slab, r1 * (tm // PERM_STRIDE):(r1 + 1) * (tm // PERM_STRIDE), :] = (
                            perm_scr[0, slab, pl.ds(r1, tm // PERM_STRIDE, stride=PERM_STRIDE), :])
                for r in range(d):
                    cols = slice(r * ATT_DIM + s * LANES, r * ATT_DIM + (s + 1) * LANES)
                    if d > PERM_STRIDE:
                        r1, r2 = r % PERM_STRIDE, r // PERM_STRIDE
                        rows = pl.ds(r1 * (tm // PERM_STRIDE) + r2, tm // d, stride=PERM_STRIDE)
                        ref[:, cols] = perm_scr[1, slab, rows, :].astype(BF16)
                    else:
                        ref[:, cols] = perm_scr[0, slab, pl.ds(r, tm // d, stride=d), :].astype(BF16)
        return epi

    halves = [slice(j * IN_SEG, (j + 1) * IN_SEG) for j in range(HG_DIM // IN_SEG)]
    segments = []
    for j, sl in enumerate(halves):
        segments += [(C_HQ + j * IN_SEG, silu_to(qa_ref, sl)), (C_HF + j * IN_SEG, forget_to(sl)),
                     (C_HG + j * IN_SEG, silu_to(ga_ref, sl))]
    for j in range(2 * D_MODEL // IN_SEG):
        segments.append((C_GATES + j * IN_SEG, sigmoid_to(gates_ref, slice(j * IN_SEG, (j + 1) * IN_SEG))))
    segments.append((C_AG, silu_to(ag_ref, slice(0, ATT_DIM))))
    att_refs = ((q1_ref, k1_ref, v1_ref), (q2_ref, k2_ref, v2_ref), (q3_ref, k3_ref, v3_ref))
    for g in reversed(range(N_GROUPS)):
        d = ATT_DILATIONS[g]
        segments += [(C_AQ + g * ATT_DIM, attn_to(att_refs[g][0], d, 0, True, ATT_HEAD_DIM ** -0.5 * LOG2_E)),
                     (C_AK + g * ATT_DIM, attn_to(att_refs[g][1], d, 1, True, 1.0)),
                     (C_AV + g * ATT_DIM, attn_to(att_refs[g][2], d, 2, False, 1.0))]
    for j, sl in enumerate(halves):
        segments.append((C_HI + j * IN_SEG, cast_to(va_ref, sl)))

    z = proj(segments[0][0], IN_SEG)
    h_scr[1 - slot] = normed(xn_ref)
    for i, (_, epilogue) in enumerate(segments):
        z_next = proj(segments[i + 1][0], IN_SEG) if i + 1 < len(segments) else None
        epilogue(z)
        z = z_next


def _inproj(x2, pos_b, norm_w, w_in_bf, lb_logits, invf):
    T = x2.shape[0]
    tm = IN_TM
    steps = T // tm
    row = lambda n: pl.BlockSpec((tm, n), lambda i: (i, 0))
    const = lambda shape: pl.BlockSpec(shape, lambda i: (0, 0))
    out_shapes = ([jax.ShapeDtypeStruct((T, HG_DIM), BF16),
                   jax.ShapeDtypeStruct((T, HG_DIM), BF16),
                   jax.ShapeDtypeStruct((T, HG_DIM), F32),
                   jax.ShapeDtypeStruct((T, HG_DIM), BF16),
                   jax.ShapeDtypeStruct((T, HG_DIM), BF16)]
                  + [jax.ShapeDtypeStruct((T // d, d * ATT_DIM), BF16) for d in ATT_DILATIONS for _ in range(3)]
                  + [jax.ShapeDtypeStruct((T, ATT_DIM), BF16),
                     jax.ShapeDtypeStruct((T, 2 * D_MODEL), BF16)])
    out_specs = ([row(HG_DIM)] * 5
                 + [pl.BlockSpec((tm // d, d * ATT_DIM), lambda i: (i, 0)) for d in ATT_DILATIONS for _ in range(3)]
                 + [row(ATT_DIM), row(2 * D_MODEL)])
    return pl.pallas_call(
        _inproj_kernel,
        grid=(steps,),
        in_specs=[const((tm, D_MODEL)),
                  pl.BlockSpec((tm, D_MODEL), lambda i: (jnp.minimum(i + 1, steps - 1), 0)),
                  row(LANES), const((1, D_MODEL)),
                  pl.BlockSpec((D_MODEL, IN_COLS), lambda i: (0, 0), pipeline_mode=pl.Buffered(1)),
                  const((2, HG_DIM)), const((1, LANES))],
        out_specs=out_specs,
        out_shape=out_shapes,
        scratch_shapes=[pltpu.VMEM((2, 3, tm, LANES), F32), pltpu.VMEM((2, tm, D_MODEL), BF16)],
        compiler_params=pltpu.CompilerParams(dimension_semantics=("arbitrary",),
                                             vmem_limit_bytes=VMEM_LIMIT_BYTES),
        name="inproj",
    )(x2, x2, pos_b, norm_w, w_in_bf, lb_logits, invf)


def _hgrn_sum_matrix():
    C = HG_CHUNK
    fine = [m for m in HG_LEVEL_SIZES if m < HG_ROW_TILE]
    A = np.zeros(((1 + len(fine)) * C, C), np.float32)
    for t in range(C):
        A[t, :t + 1] = 1.0
    for i, m in enumerate(fine):
        for t in range(C):
            start = (t // m) * m
            if t & m:
                A[(1 + i) * C + t, start:t + 1] = 1.0
            else:
                A[(1 + i) * C + t, t + 1:start + m] = 1.0
    return np.concatenate([A, A], axis=1)


def _hgrn_kernel(q_ref, k_ref, lf_ref, v_ref, g_ref, a_ref, nw_ref, o_ref, state_scr, e_scr):
    C = HG_CHUNK
    n_lev = len(HG_LEVEL_SIZES)
    n_chunks = HG_TS // C

    @pl.when(pl.program_id(1) == 0)
    def _():
        state_scr[...] = jnp.zeros_like(state_scr)

    t_idx = lax.broadcasted_iota(jnp.int32, (C, 2 * C), 0)
    s_idx = lax.broadcasted_iota(jnp.int32, (C, 2 * C), 1) & (C - 1)
    diag_mask = t_idx == s_idx
    level_masks = []
    for m in HG_LEVEL_SIZES:
        sh = (2 * m).bit_length() - 1
        level_masks.append(((t_idx & m) != 0) & ((s_idx & m) == 0) & ((t_idx >> sh) == (s_idx >> sh)))
    row_idx = lax.broadcasted_iota(jnp.int32, (C, HG_KEY), 0)
    right_rows = [(row_idx & m) != 0 for m in HG_LEVEL_SIZES]

    def exponents(ci):
        slot = ci % 2
        lf = lf_ref[ci * C:(ci + 1) * C, :]
        hi = lf.astype(BF16)
        lo = (lf - hi.astype(F32)).astype(BF16)
        sums = _dot(a_ref[...], jnp.concatenate([hi, lo], axis=0))
        b = sums[0:C, :]

        def put(i, arg):
            e_scr[slot, i * C:(i + 1) * C, :] = jnp.exp2(arg)

        i_fine = 1
        for i, m in enumerate(HG_LEVEL_SIZES):
            if m >= HG_ROW_TILE:
                parts = []
                for j in range(C // (2 * m)):
                    ref = jnp.broadcast_to(b[j * 2 * m + m - 1:j * 2 * m + m, :], (m, HG_DIM))
                    parts += [ref - b[j * 2 * m:j * 2 * m + m, :], b[j * 2 * m + m:(j + 1) * 2 * m, :] - ref]
                put(i, jnp.concatenate(parts, axis=0))
            else:
                put(i, sums[i_fine * C:(i_fine + 1) * C, :])
                i_fine += 1
        put(n_lev, b)
        put(n_lev + 1, b[C - 1:C, :] - b)

    def block_diag(x_a, x_b):
        return jnp.concatenate([jnp.concatenate([x_a, jnp.zeros_like(x_a)], axis=1),
                                jnp.concatenate([jnp.zeros_like(x_b), x_b], axis=1)], axis=0)

    def pair_scores(ci, hp):
        slot = ci % 2
        rows = slice(ci * C, (ci + 1) * C)
        lanes = [slice(h * HG_KEY, (h + 1) * HG_KEY) for h in (2 * hp, 2 * hp + 1)]
        q_bf = [q_ref[rows, hl] for hl in lanes]
        k_bf = [k_ref[rows, hl] for hl in lanes]
        q = [x.astype(F32) for x in q_bf]
        k = [x.astype(F32) for x in k_bf]
        scores = jnp.where(diag_mask, _dot_nt(jnp.concatenate(q_bf, axis=1), block_diag(*k_bf)), 0.0)
        for i, m in enumerate(HG_LEVEL_SIZES):
            zs = []
            for a, hl in enumerate(lanes):
                if m >= HG_ROW_TILE:
                    w = jnp.concatenate([(q[a] if j % 2 else k[a])[j * m:(j + 1) * m, :] for j in range(C // m)],
                                        axis=0)
                else:
                    w = jnp.where(right_rows[i], q[a], k[a])
                zs.append((w * e_scr[slot, i * C:(i + 1) * C, hl]).astype(BF16))
            scores = jnp.where(level_masks[i], _dot_nt(jnp.concatenate(zs, axis=1), block_diag(*zs)), scores)
        q_dec = [(q[a] * e_scr[slot, n_lev * C:(n_lev + 1) * C, hl]).astype(BF16) for a, hl in enumerate(lanes)]
        k_dec = [(k[a] * e_scr[slot, (n_lev + 1) * C:(n_lev + 2) * C, hl]).astype(BF16) for a, hl in enumerate(lanes)]
        return scores, q_dec, k_dec

    def finish(ci, hp, stage1):
        scores, q_dec, k_dec = stage1
        slot = ci % 2
        rows = slice(ci * C, (ci + 1) * C)
        heads = (2 * hp, 2 * hp + 1)
        lanes = [slice(h * HG_KEY, (h + 1) * HG_KEY) for h in heads]
        v_bf = [v_ref[rows, hl] for hl in lanes]
        o_pair = _dot(scores.astype(BF16), block_diag(*v_bf))
        for a, (h, hl) in enumerate(zip(heads, lanes)):
            e_last = e_scr[slot, (n_lev + 1) * C - 1:(n_lev + 1) * C, hl]
            st = state_scr[h]
            o = o_pair[:, a * HG_VAL:(a + 1) * HG_VAL] + _dot_nt(q_dec[a], st.astype(BF16))
            state_scr[h] = st * e_last + _dot_tn(v_bf[a], k_dec[a])
            ms = jnp.mean(o * o, axis=-1, keepdims=True)
            y = o * lax.rsqrt(ms + NORM_EPS) * nw_ref[0:1, 0:HG_VAL] * g_ref[rows, hl].astype(F32)
            o_ref[rows, hl] = y.astype(BF16)

    n_pairs = HG_HEADS // 2
    exponents(0)
    pending = []
    for ci in range(n_chunks):
        for hp in range(n_pairs):
            if hp == n_pairs // 2 and ci + 1 < n_chunks:
                exponents(ci + 1)
            pending.append((ci, hp, pair_scores(ci, hp)))
            if len(pending) > HG_SKEW:
                finish(*pending.pop(0))
    while pending:
        finish(*pending.pop(0))


def _hgrn(qa, ka, lf, va, ga, a_mat, hgrn_w, B, S):
    ts = HG_TS
    steps = S // ts
    blk = pl.BlockSpec((ts, HG_DIM), lambda b, i: (b * steps + i, 0))
    rows_e = (len(HG_LEVEL_SIZES) + 2) * HG_CHUNK
    return pl.pallas_call(
        _hgrn_kernel,
        grid=(B, steps),
        in_specs=[blk, blk, blk, blk, blk,
                  pl.BlockSpec(a_mat.shape, lambda b, i: (0, 0)),
                  pl.BlockSpec((PAR_ROWS, HG_DIM), lambda b, i: (0, 0))],
        out_specs=blk,
        out_shape=jax.ShapeDtypeStruct((B * S, HG_DIM), BF16),
        scratch_shapes=[pltpu.VMEM((HG_HEADS, HG_VAL, HG_KEY), F32),
                        pltpu.VMEM((2, rows_e, HG_DIM), F32)],
        compiler_params=pltpu.CompilerParams(dimension_semantics=("parallel", "arbitrary"),
                                             vmem_limit_bytes=VMEM_LIMIT_BYTES),
        name="hgrn2",
    )(qa, ka, lf, va, ga, a_mat, hgrn_w)


def _attn_kernel(*refs, seq):
    n_in = 3 * sum(ATT_DILATIONS) + 1
    in_refs = refs[:n_in]
    ag_ref = in_refs[-1]
    out_ref = refs[n_in]
    o_scr = refs[n_in + 1:n_in + 1 + N_GROUPS]
    lse_scr = refs[n_in + 1 + N_GROUPS:n_in + 1 + 2 * N_GROUPS]
    n = ATT_STEPS

    lane = lax.broadcasted_iota(jnp.int32, (n, LANES), 1)
    head_a = lane < ATT_HEAD_DIM
    qi = lax.broadcasted_iota(jnp.int32, (n, 2 * n), 0)
    kj = lax.broadcasted_iota(jnp.int32, (n, 2 * n), 1)
    band_ok = (kj >= qi) & (kj <= qi + n)
    first_ok = (lax.broadcasted_iota(jnp.int32, (n, n), 1)
                <= lax.broadcasted_iota(jnp.int32, (n, n), 0))

    def scores(q_ref, k_ref, c):
        q = q_ref[c * n:(c + 1) * n, :]
        kk = k_ref[max(c - 1, 0) * n:(c + 1) * n, :]
        zero_q = jnp.zeros_like(q)
        return [_dot_nt(jnp.where(head_a, q, zero_q), kk), _dot_nt(jnp.where(head_a, zero_q, q), kk)]

    def finish(g, d, r, c, v_ref, s_pair):
        ok = first_ok if c == 0 else band_ok
        vv = v_ref[max(c - 1, 0) * n:(c + 1) * n, :]
        ps, ms = [], []
        for s in s_pair:
            s = jnp.where(ok, s, NEG_BIG)
            sm = s if c == 0 else jnp.maximum(s[:, :n], s[:, n:])
            m = jnp.max(sm, axis=-1, keepdims=True)
            ps.append(jnp.exp2(s - m).astype(BF16))
            ms.append(m)
        ol = _dot(jnp.concatenate(ps, axis=0), jnp.concatenate([vv, jnp.ones_like(vv)], axis=1))
        l = jnp.where(head_a, ol[:n, LANES:], ol[n:, LANES:])
        o = jnp.where(head_a, ol[:n, :LANES], ol[n:, :LANES]) * (1.0 / l)
        lse = jnp.where(head_a, ms[0], ms[1]) + jnp.log2(l)
        dst = pl.ds(c * n, n) if d == 1 else pl.ds(c * (n * d) + r, n, stride=d)
        o_scr[g][dst, :] = o
        lse_scr[g][dst, :] = lse

    pending = []
    pos = 0
    for g, d in enumerate(ATT_DILATIONS):
        q_refs = in_refs[pos:pos + d]
        k_refs = in_refs[pos + d:pos + 2 * d]
        v_refs = in_refs[pos + 2 * d:pos + 3 * d]
        pos += 3 * d
        for r in range(d):
            for c in range(seq // d // n):
                pending.append((g, d, r, c, v_refs[r], scores(q_refs[r], k_refs[r], c)))
                if len(pending) > ATT_SKEW:
                    finish(*pending.pop(0))
    while pending:
        finish(*pending.pop(0))

    rows_per = 256

    def merge(i, carry):
        rows = pl.ds(pl.multiple_of(i * rows_per, rows_per), rows_per)
        l1, l2, l3 = lse_scr[0][rows, :], lse_scr[1][rows, :], lse_scr[2][rows, :]
        mx = jnp.maximum(jnp.maximum(l1, l2), l3)
        w1, w2, w3 = jnp.exp2(l1 - mx), jnp.exp2(l2 - mx), jnp.exp2(l3 - mx)
        num = w1 * o_scr[0][rows, :] + w2 * o_scr[1][rows, :] + w3 * o_scr[2][rows, :]
        ob = num / (w1 + w2 + w3) * ag_ref[rows, :].astype(F32)
        out_ref[rows, :] = ob.astype(BF16)
        return carry

    lax.fori_loop(0, seq // rows_per, merge, 0)


def _attention(qkv, ag, B, S):
    pairs = ATT_DIM // LANES
    operands, in_specs = [], []
    for g, d in enumerate(ATT_DILATIONS):
        m_len = S // d
        for arr in qkv[g]:
            view = arr.reshape(B, m_len, d * ATT_DIM)
            for r in range(d):
                operands.append(view)
                in_specs.append(pl.BlockSpec((None, m_len, LANES),
                                             functools.partial(lambda b, hp, r: (b, 0, r * pairs + hp), r=r)))
    operands.append(ag.reshape(B, S, ATT_DIM))
    nat = pl.BlockSpec((None, S, LANES), lambda b, hp: (b, 0, hp))
    in_specs.append(nat)
    out = pl.pallas_call(
        functools.partial(_attn_kernel, seq=S),
        grid=(B, pairs),
        in_specs=in_specs,
        out_specs=nat,
        out_shape=jax.ShapeDtypeStruct((B, S, ATT_DIM), BF16),
        scratch_shapes=[pltpu.VMEM((S, LANES), F32)] * (2 * N_GROUPS),
        compiler_params=pltpu.CompilerParams(dimension_semantics=("parallel", "parallel"),
                                             vmem_limit_bytes=VMEM_LIMIT_BYTES),
        name="dilated_attn",
    )(*operands)
    return out.reshape(B * S, ATT_DIM)


def _out_kernel(x_ref, oa_ref, ob_ref, gates_ref, wa_ref, wb_ref, wo_ref, fnw_ref, out_ref):
    ya = _dot(oa_ref[...], wa_ref[...])
    yb = _dot(ob_ref[...], wb_ref[...])
    g_a = gates_ref[:, :D_MODEL].astype(F32)
    g_b = gates_ref[:, D_MODEL:].astype(F32)
    merged = (g_a * ya + g_b * yb).astype(BF16)
    y = x_ref[...] + _dot(merged, wo_ref[...])
    ms = jnp.mean(y * y, axis=-1, keepdims=True)
    out_ref[...] = y * lax.rsqrt(ms + NORM_EPS) * fnw_ref[0:1, :]


def _output(x2, oa, ob, gates, wa, wb, wo, fnw):
    T = x2.shape[0]
    tm = OUT_TM
    row = lambda n: pl.BlockSpec((tm, n), lambda i: (i, 0))
    const = lambda shape: pl.BlockSpec(shape, lambda i: (0, 0))
    return pl.pallas_call(
        _out_kernel,
        grid=(T // tm,),
        in_specs=[row(D_MODEL), row(HG_DIM), row(ATT_DIM), row(2 * D_MODEL),
                  const((HG_DIM, D_MODEL)), const((ATT_DIM, D_MODEL)), const((D_MODEL, D_MODEL)),
                  const((PAR_ROWS, D_MODEL))],
        out_specs=row(D_MODEL),
        out_shape=jax.ShapeDtypeStruct((T, D_MODEL), F32),
        compiler_params=pltpu.CompilerParams(dimension_semantics=("parallel",),
                                             vmem_limit_bytes=VMEM_LIMIT_BYTES),
        name="merge_out",
    )(x2, oa, ob, gates, wa, wb, wo, fnw)


def kernel(x, positions, norm_w, w_in, lb_logits, hgrn_norm_w, w_branch_a, w_branch_b, w_out, final_norm_w):
    B, S, D = x.shape
    assert D == D_MODEL and norm_w.shape[0] == 1 and lb_logits.shape == (2, HG_DIM)
    assert S % (ATT_STEPS * max(ATT_DILATIONS)) == 0 and S % HG_TS == 0 and (B * S) % OUT_TM == 0
    T = B * S
    x2 = x.reshape(T, D)
    pos_b = jnp.broadcast_to(positions.reshape(T, 1), (T, LANES))
    inv_freq = ROPE_THETA ** (-jnp.arange(0, ATT_HEAD_DIM, 2, dtype=F32) / ATT_HEAD_DIM)
    invf = jnp.tile(inv_freq, LANES // (ATT_HEAD_DIM // 2)).reshape(1, LANES)

    outs = _inproj(x2, pos_b, norm_w.reshape(1, D), w_in[0].astype(BF16), lb_logits.astype(F32), invf)
    qa, ka, lf, va, ga = outs[:5]
    qkv = [tuple(outs[5 + 3 * g:8 + 3 * g]) for g in range(N_GROUPS)]
    ag, gates = outs[14], outs[15]

    a_mat = jnp.asarray(_hgrn_sum_matrix(), BF16)
    hgrn_w = jnp.broadcast_to(jnp.tile(hgrn_norm_w.reshape(1, HG_VAL).astype(F32), (1, HG_HEADS)), (PAR_ROWS, HG_DIM))
    oa = _hgrn(qa, ka, lf, va, ga, a_mat, hgrn_w, B, S)
    ob = _attention(qkv, ag, B, S)
    out = _output(x2, oa, ob, gates, w_branch_a[0].astype(BF16), w_branch_b[0].astype(BF16),
                  w_out[0].astype(BF16), jnp.broadcast_to(final_norm_w.reshape(1, D).astype(F32), (PAR_ROWS, D)))
    return out.reshape(B, S, D)
```

```python
import functools

import numpy as np
import jax
import jax.numpy as jnp
from jax import lax
from jax.experimental import pallas as pl
from jax.experimental.pallas import tpu as pltpu

D_MODEL = 1024
HG_HEADS = 8
HG_KEY = 128
HG_VAL = 128
HG_DIM = HG_HEADS * HG_KEY
ATT_DILATIONS = (1, 4, 16)
ATT_STEPS = 128
N_GROUPS = 3
ATT_HEADS = 8
ATT_HEAD_DIM = 64
ATT_DIM = ATT_HEADS * ATT_HEAD_DIM
ROPE_THETA = 10000.0
NORM_EPS = 1e-6
LOG2_E = 1.4426950408889634
C_HQ, C_HF, C_HI, C_HG = 0, 1024, 2048, 3072
C_AQ, C_AK, C_AV = 4096, 4096 + 1536, 4096 + 3072
C_AG = 8704
C_GATES = 9216
IN_COLS = 11264

LANES = 128
VMEM_LIMIT_BYTES = 56 * 1024 * 1024

PAR_ROWS = 8

IN_TM = 256
IN_SEG = 512
PERM_STRIDE = 4
HG_CHUNK = 64
HG_TS = 1024
HG_LEVEL_SIZES = tuple(HG_CHUNK >> (l + 1) for l in range(HG_CHUNK.bit_length() - 1))
HG_ROW_TILE = 8
HG_GROUP = 2
HG_SKEW = 1
OUT_TM = 1024
OUT_SUB = 256
ATT_SKEW = 3
NEG_BIG = -1e30

F32 = jnp.float32
BF16 = jnp.bfloat16


def _dot(a, b):
    return jnp.dot(a, b, preferred_element_type=F32)


def _dot_nt(a, b):
    return lax.dot_general(a, b, (((1,), (1,)), ((), ())), preferred_element_type=F32)


def _dot_tn(a, b):
    return lax.dot_general(a, b, (((0,), (0,)), ((), ())), preferred_element_type=F32)


def _sigmoid(z):
    return 0.5 + 0.5 * jnp.tanh(0.5 * z)


def _inproj_kernel(x0_ref, xn_ref, pos0_ref, posn_ref, nw_ref, w_ref, lbl_ref, invf_ref,
                   qa_ref, ka_ref, lf_ref, va_ref, ga_ref,
                   q1_ref, k1_ref, v1_ref, q2_ref, k2_ref, v2_ref, q3_ref, k3_ref, v3_ref,
                   ag_ref, gates_ref, perm_scr, h_scr, rope_scr):
    tm = xn_ref.shape[0]
    step = pl.program_id(0)
    slot = step % 2
    lane = lax.broadcasted_iota(jnp.int32, (tm, LANES), 1)
    first = (lane & (ATT_HEAD_DIM - 1)) < (ATT_HEAD_DIM // 2)

    def normed(x_ref):
        x = x_ref[...]
        ms = jnp.mean(x * x, axis=-1, keepdims=True)
        return (x * lax.rsqrt(ms + NORM_EPS) * nw_ref[...]).astype(BF16)

    def put_rope_tables(pos_ref, dst):
        ang = pos_ref[...].astype(F32) * invf_ref[...]
        cos = jnp.cos(ang)
        sin = jnp.sin(ang)
        sin_signed = jnp.where(first, -sin, sin)
        q_factor = ATT_HEAD_DIM ** -0.5 * LOG2_E
        rope_scr[dst, 0] = cos
        rope_scr[dst, 1] = sin_signed
        rope_scr[dst, 2] = cos * q_factor
        rope_scr[dst, 3] = sin_signed * q_factor

    @pl.when(step == 0)
    def _():
        h_scr[0] = normed(x0_ref)
        put_rope_tables(pos0_ref, 0)

    def proj(c0, n):
        return _dot(h_scr[slot], w_ref[:, c0:c0 + n])

    lbl = lbl_ref[...]
    lmax = jnp.max(lbl, axis=0, keepdims=True)
    lexp = jnp.exp(lbl - lmax)
    lb = lexp[0:1, :] / jnp.sum(lexp, axis=0, keepdims=True)

    def rope(zs, tables):
        rot = jnp.where(first, pltpu.roll(zs, LANES - ATT_HEAD_DIM // 2, 1), pltpu.roll(zs, ATT_HEAD_DIM // 2, 1))
        return zs * rope_scr[slot, tables] + rot * rope_scr[slot, tables + 1]

    def silu_to(ref, sl):
        def epi(z):
            hz = 0.5 * z
            ref[:, sl] = (hz + hz * jnp.tanh(hz)).astype(BF16)
        return epi

    def cast_to(ref, sl):
        def epi(z):
            ref[:, sl] = z.astype(BF16)
        return epi

    def sigmoid_to(ref, sl):
        def epi(z):
            ref[:, sl] = _sigmoid(z).astype(BF16)
        return epi

    def forget_to(sl):
        def epi(z):
            c = 0.5 - 0.5 * lb[:, sl]
            ct = c * jnp.tanh(0.5 * z)
            lf_ref[:, sl] = jnp.log2((lb[:, sl] + c) + ct)
            ka_ref[:, sl] = (c - ct).astype(BF16)
        return epi

    def attn_to(ref, d, slab, tables):
        def epi(z):
            for s in range(ATT_DIM // LANES):
                ls = slice(s * LANES, (s + 1) * LANES)
                val = z[:, ls]
                if tables is not None:
                    val = rope(val, tables)
                if d == 1:
                    ref[:, ls] = val.astype(BF16)
                    continue
                perm_scr[0, slab] = val
                if d > PERM_STRIDE:
                    for r1 in range(PERM_STRIDE):
                        perm_scr[1, slab, r1 * (tm // PERM_STRIDE):(r1 + 1) * (tm // PERM_STRIDE), :] = (
                            perm_scr[0, slab, pl.ds(r1, tm // PERM_STRIDE, stride=PERM_STRIDE), :])
                for r in range(d):
                    cols = slice(r * ATT_DIM + s * LANES, r * ATT_DIM + (s + 1) * LANES)
                    if d > PERM_STRIDE:
                        r1, r2 = r % PERM_STRIDE, r // PERM_STRIDE
                        rows = pl.ds(r1 * (tm // PERM_STRIDE) + r2, tm // d, stride=PERM_STRIDE)
                        ref[:, cols] = perm_scr[1, slab, rows, :].astype(BF16)
                    else:
                        ref[:, cols] = perm_scr[0, slab, pl.ds(r, tm // d, stride=d), :].astype(BF16)
        return epi

    halves = [slice(j * IN_SEG, (j + 1) * IN_SEG) for j in range(HG_DIM // IN_SEG)]
    segments = []
    for j, sl in enumerate(halves):
        segments += [(C_HQ + j * IN_SEG, IN_SEG, silu_to(qa_ref, sl)), (C_HF + j * IN_SEG, IN_SEG, forget_to(sl)),
                     (C_HG + j * IN_SEG, IN_SEG, silu_to(ga_ref, sl))]
    for j in range(2 * D_MODEL // IN_SEG):
        segments.append((C_GATES + j * IN_SEG, IN_SEG,
                         sigmoid_to(gates_ref, slice(j * IN_SEG, (j + 1) * IN_SEG))))
    segments.append((C_AG, ATT_DIM, silu_to(ag_ref, slice(0, ATT_DIM))))
    att_refs = ((q1_ref, k1_ref, v1_ref), (q2_ref, k2_ref, v2_ref), (q3_ref, k3_ref, v3_ref))
    for g in reversed(range(N_GROUPS)):
        d = ATT_DILATIONS[g]
        segments += [(C_AQ + g * ATT_DIM, ATT_DIM, attn_to(att_refs[g][0], d, 0, 2)),
                     (C_AK + g * ATT_DIM, ATT_DIM, attn_to(att_refs[g][1], d, 1, 0)),
                     (C_AV + g * ATT_DIM, ATT_DIM, attn_to(att_refs[g][2], d, 2, None))]
    for j, sl in enumerate(halves):
        segments.append((C_HI + j * IN_SEG, IN_SEG, cast_to(va_ref, sl)))

    z = proj(*segments[0][:2])
    h_scr[1 - slot] = normed(xn_ref)
    put_rope_tables(posn_ref, 1 - slot)
    for i, (_, _, epilogue) in enumerate(segments):
        z_next = proj(*segments[i + 1][:2]) if i + 1 < len(segments) else None
        epilogue(z)
        z = z_next


def _inproj(x2, pos_b, norm_w, w_in_bf, lb_logits, invf):
    T = x2.shape[0]
    tm = IN_TM
    steps = T // tm
    row = lambda n: pl.BlockSpec((tm, n), lambda i: (i, 0))
    const = lambda shape: pl.BlockSpec(shape, lambda i: (0, 0))
    nxt = lambda n: pl.BlockSpec((tm, n), lambda i: (jnp.minimum(i + 1, steps - 1), 0))
    out_shapes = ([jax.ShapeDtypeStruct((T, HG_DIM), BF16),
                   jax.ShapeDtypeStruct((T, HG_DIM), BF16),
                   jax.ShapeDtypeStruct((T, HG_DIM), F32),
                   jax.ShapeDtypeStruct((T, HG_DIM), BF16),
                   jax.ShapeDtypeStruct((T, HG_DIM), BF16)]
                  + [jax.ShapeDtypeStruct((T // d, d * ATT_DIM), BF16) for d in ATT_DILATIONS for _ in range(3)]
                  + [jax.ShapeDtypeStruct((T, ATT_DIM), BF16),
                     jax.ShapeDtypeStruct((T, 2 * D_MODEL), BF16)])
    out_specs = ([row(HG_DIM)] * 5
                 + [pl.BlockSpec((tm // d, d * ATT_DIM), lambda i: (i, 0)) for d in ATT_DILATIONS for _ in range(3)]
                 + [row(ATT_DIM), row(2 * D_MODEL)])
    return pl.pallas_call(
        _inproj_kernel,
        grid=(steps,),
        in_specs=[const((tm, D_MODEL)), nxt(D_MODEL), const((tm, 1)), nxt(1), const((1, D_MODEL)),
                  pl.BlockSpec((D_MODEL, IN_COLS), lambda i: (0, 0), pipeline_mode=pl.Buffered(1)),
                  const((2, HG_DIM)), const((1, LANES))],
        out_specs=out_specs,
        out_shape=out_shapes,
        scratch_shapes=[pltpu.VMEM((2, 3, tm, LANES), F32), pltpu.VMEM((2, tm, D_MODEL), BF16),
                        pltpu.VMEM((2, 4, tm, LANES), F32)],
        compiler_params=pltpu.CompilerParams(dimension_semantics=("arbitrary",),
                                             vmem_limit_bytes=VMEM_LIMIT_BYTES),
        name="inproj",
    )(x2, x2, pos_b, pos_b, norm_w, w_in_bf, lb_logits, invf)


def _hgrn_sum_matrix():
    C = HG_CHUNK
    fine = [m for m in HG_LEVEL_SIZES if m < HG_ROW_TILE]
    A = np.zeros(((1 + len(fine)) * C, C), np.float32)
    for t in range(C):
        A[t, :t + 1] = 1.0
    for i, m in enumerate(fine):
        for t in range(C):
            start = (t // m) * m
            if t & m:
                A[(1 + i) * C + t, start:t + 1] = 1.0
            else:
                A[(1 + i) * C + t, t + 1:start + m] = 1.0
    return np.concatenate([A, A], axis=1)


def _hgrn_kernel(q_ref, k_ref, lf_ref, v_ref, g_ref, a_ref, nw_ref, o_ref, state_scr, e_scr):
    C = HG_CHUNK
    n_lev = len(HG_LEVEL_SIZES)
    n_chunks = HG_TS // C

    @pl.when(pl.program_id(1) == 0)
    def _():
        state_scr[...] = jnp.zeros_like(state_scr)

    t_idx = lax.broadcasted_iota(jnp.int32, (C, HG_GROUP * C), 0)
    s_idx = lax.broadcasted_iota(jnp.int32, (C, HG_GROUP * C), 1) & (C - 1)
    diag_mask = t_idx == s_idx
    level_masks = []
    for m in HG_LEVEL_SIZES:
        sh = (2 * m).bit_length() - 1
        level_masks.append(((t_idx & m) != 0) & ((s_idx & m) == 0) & ((t_idx >> sh) == (s_idx >> sh)))
    row_idx = lax.broadcasted_iota(jnp.int32, (C, HG_KEY), 0)
    right_rows = [(row_idx & m) != 0 for m in HG_LEVEL_SIZES]

    def exponents(ci):
        slot = ci % 2
        lf = lf_ref[ci * C:(ci + 1) * C, :]
        hi = lf.astype(BF16)
        lo = (lf - hi.astype(F32)).astype(BF16)
        sums = _dot(a_ref[...], jnp.concatenate([hi, lo], axis=0))
        b = sums[0:C, :]

        def put(i, arg):
            e_scr[slot, i * C:(i + 1) * C, :] = jnp.exp2(arg)

        i_fine = 1
        for i, m in enumerate(HG_LEVEL_SIZES):
            if m >= HG_ROW_TILE:
                parts = []
                for j in range(C // (2 * m)):
                    ref = jnp.broadcast_to(b[j * 2 * m + m - 1:j * 2 * m + m, :], (m, HG_DIM))
                    parts += [ref - b[j * 2 * m:j * 2 * m + m, :], b[j * 2 * m + m:(j + 1) * 2 * m, :] - ref]
                put(i, jnp.concatenate(parts, axis=0))
            else:
                put(i, sums[i_fine * C:(i_fine + 1) * C, :])
                i_fine += 1
        put(n_lev, b)
        put(n_lev + 1, b[C - 1:C, :] - b)

    def block_diag(xs):
        zero = jnp.zeros_like(xs[0])
        return jnp.concatenate([jnp.concatenate([x if b == a else zero for b in range(len(xs))], axis=1)
                                for a, x in enumerate(xs)], axis=0)

    def pair_scores(ci, hp):
        slot = ci % 2
        rows = slice(ci * C, (ci + 1) * C)
        lanes = [slice(h * HG_KEY, (h + 1) * HG_KEY) for h in range(HG_GROUP * hp, HG_GROUP * (hp + 1))]
        q_bf =[q_ref[rows, hl] for hl in lanes]
        k_bf = [k_ref[rows, hl] for hl in lanes]
        q = [x.astype(F32) for x in q_bf]
        k = [x.astype(F32) for x in k_bf]
        scores = jnp.where(diag_mask, _dot_nt(jnp.concatenate(q_bf, axis=1), block_diag(k_bf)), 0.0)
        for i, m in enumerate(HG_LEVEL_SIZES):
            zs = []
            for a, hl in enumerate(lanes):
                if m >= HG_ROW_TILE:
                    w = jnp.concatenate([(q[a] if j % 2 else k[a])[j * m:(j + 1) * m, :] for j in range(C // m)],
                                        axis=0)
                else:
                    w = jnp.where(right_rows[i], q[a], k[a])
                zs.append((w * e_scr[slot, i * C:(i + 1) * C, hl]).astype(BF16))
            scores = jnp.where(level_masks[i], _dot_nt(jnp.concatenate(zs, axis=1), block_diag(zs)), scores)
        q_dec = [(q[a] * e_scr[slot, n_lev * C:(n_lev + 1) * C, hl]).astype(BF16) for a, hl in enumerate(lanes)]
        k_dec = [(k[a] * e_scr[slot, (n_lev + 1) * C:(n_lev + 2) * C, hl]).astype(BF16) for a, hl in enumerate(lanes)]
        return scores, q_dec, k_dec

    def finish(ci, hp, stage1):
        scores, q_dec, k_dec = stage1
        slot = ci % 2
        rows = slice(ci * C, (ci + 1) * C)
        heads = tuple(range(HG_GROUP * hp, HG_GROUP * (hp + 1)))
        lanes = [slice(h * HG_KEY, (h + 1) * HG_KEY) for h in heads]
        v_bf = [v_ref[rows, hl] for hl in lanes]
        o_pair = _dot(scores.astype(BF16), block_diag(v_bf))
        for a, (h, hl) in enumerate(zip(heads, lanes)):
            e_last = e_scr[slot, (n_lev + 1) * C - 1:(n_lev + 1) * C, hl]
            st = state_scr[h]
            o = o_pair[:, a * HG_VAL:(a + 1) * HG_VAL] + _dot_nt(q_dec[a], st.astype(BF16))
            state_scr[h] = st * e_last + _dot_tn(v_bf[a], k_dec[a])
            ms = jnp.mean(o * o, axis=-1, keepdims=True)
            y = o * lax.rsqrt(ms + NORM_EPS) * nw_ref[0:1, 0:HG_VAL] * g_ref[rows, hl].astype(F32)
            o_ref[rows, hl] = y.astype(BF16)

    n_pairs = HG_HEADS // HG_GROUP
    exponents(0)
    pending = []
    for ci in range(n_chunks):
        for hp in range(n_pairs):
            if hp == n_pairs // 2 and ci + 1 < n_chunks:
                exponents(ci + 1)
            pending.append((ci, hp, pair_scores(ci, hp)))
            if len(pending) > HG_SKEW:
                finish(*pending.pop(0))
    while pending:
        finish(*pending.pop(0))


def _hgrn(qa, ka, lf, va, ga, a_mat, hgrn_w, B, S):
    ts = HG_TS
    steps = S // ts
    blk = pl.BlockSpec((ts, HG_DIM), lambda b, i: (b * steps + i, 0))
    rows_e = (len(HG_LEVEL_SIZES) + 2) * HG_CHUNK
    return pl.pallas_call(
        _hgrn_kernel,
        grid=(B, steps),
        in_specs=[blk, blk, blk, blk, blk,
                  pl.BlockSpec(a_mat.shape, lambda b, i: (0, 0)),
                  pl.BlockSpec((PAR_ROWS, HG_DIM), lambda b, i: (0, 0))],
        out_specs=blk,
        out_shape=jax.ShapeDtypeStruct((B * S, HG_DIM), BF16),
        scratch_shapes=[pltpu.VMEM((HG_HEADS, HG_VAL, HG_KEY), F32),
                        pltpu.VMEM((2, rows_e, HG_DIM), F32)],
        compiler_params=pltpu.CompilerParams(dimension_semantics=("parallel", "arbitrary"),
                                             vmem_limit_bytes=VMEM_LIMIT_BYTES),
        name="hgrn2",
    )(qa, ka, lf, va, ga, a_mat, hgrn_w)


def _attn_kernel(*refs, seq):
    n_in = 3 * sum(ATT_DILATIONS) + 1
    in_refs = refs[:n_in]
    ag_ref = in_refs[-1]
    out_ref = refs[n_in]
    o_scr = refs[n_in + 1:n_in + 1 + N_GROUPS]
    lse_scr = refs[n_in + 1 + N_GROUPS:n_in + 1 + 2 * N_GROUPS]
    n = ATT_STEPS

    lane = lax.broadcasted_iota(jnp.int32, (n, LANES), 1)
    head_a = lane < ATT_HEAD_DIM
    qi = lax.broadcasted_iota(jnp.int32, (n, 2 * n), 0)
    kj = lax.broadcasted_iota(jnp.int32, (n, 2 * n), 1)
    band_ok = (kj >= qi) & (kj <= qi + n)
    first_ok = (lax.broadcasted_iota(jnp.int32, (n, n), 1)
                <= lax.broadcasted_iota(jnp.int32, (n, n), 0))

    def scores(q_ref, k_ref, c):
        q = q_ref[c * n:(c + 1) * n, :]
        kk = k_ref[max(c - 1, 0) * n:(c + 1) * n, :]
        zero_q = jnp.zeros_like(q)
        return [_dot_nt(jnp.where(head_a, q, zero_q), kk), _dot_nt(jnp.where(head_a, zero_q, q), kk)]

    def finish(g, d, r, c, v_ref, s_pair):
        ok = first_ok if c == 0 else band_ok
        vv = v_ref[max(c - 1, 0) * n:(c + 1) * n, :]
        ps, ms = [], []
        for s in s_pair:
            s = jnp.where(ok, s, NEG_BIG)
            sm = s if c == 0 else jnp.maximum(s[:, :n], s[:, n:])
            m = jnp.max(sm, axis=-1, keepdims=True)
            ps.append(jnp.exp2(s - m).astype(BF16))
            ms.append(m)
        ol = _dot(jnp.concatenate(ps, axis=0), jnp.concatenate([vv, jnp.ones_like(vv)], axis=1))
        l = jnp.where(head_a, ol[:n, LANES:], ol[n:, LANES:])
        o = jnp.where(head_a, ol[:n, :LANES], ol[n:, :LANES]) * (1.0 / l)
        lse = jnp.where(head_a, ms[0], ms[1]) + jnp.log2(l)
        dst = pl.ds(c * n, n) if d == 1 else pl.ds(c * (n * d) + r, n, stride=d)
        o_scr[g][dst, :] = o
        lse_scr[g][dst, :] = lse

    pending = []
    pos = 0
    for g, d in enumerate(ATT_DILATIONS):
        q_refs = in_refs[pos:pos + d]
        k_refs = in_refs[pos + d:pos + 2 * d]
        v_refs = in_refs[pos + 2 * d:pos + 3 * d]
        pos += 3 * d
        for r in range(d):
            for c in range(seq // d // n):
                pending.append((g, d, r, c, v_refs[r], scores(q_refs[r], k_refs[r], c)))
                if len(pending) > ATT_SKEW:
                    finish(*pending.pop(0))
    while pending:
        finish(*pending.pop(0))

    rows_per = 256

    def merge(i, carry):
        rows = pl.ds(pl.multiple_of(i * rows_per, rows_per), rows_per)
        l1, l2, l3 = lse_scr[0][rows, :], lse_scr[1][rows, :], lse_scr[2][rows, :]
        mx = jnp.maximum(jnp.maximum(l1, l2), l3)
        w1, w2, w3 = jnp.exp2(l1 - mx), jnp.exp2(l2 - mx), jnp.exp2(l3 - mx)
        num = w1 * o_scr[0][rows, :] + w2 * o_scr[1][rows, :] + w3 * o_scr[2][rows, :]
        ob = num / (w1 + w2 + w3) * ag_ref[rows, :].astype(F32)
        out_ref[rows, :] = ob.astype(BF16)
        return carry

    lax.fori_loop(0, seq // rows_per, merge, 0)


def _attention(qkv, ag, B, S):
    pairs = ATT_DIM // LANES
    operands, in_specs = [], []
    for g, d in enumerate(ATT_DILATIONS):
        m_len = S // d
        for arr in qkv[g]:
            view = arr.reshape(B, m_len, d * ATT_DIM)
            for r in range(d):
                operands.append(view)
                in_specs.append(pl.BlockSpec((None, m_len, LANES),
                                             functools.partial(lambda b, hp, r: (b, 0, r * pairs + hp), r=r)))
    operands.append(ag.reshape(B, S, ATT_DIM))
    nat = pl.BlockSpec((None, S, LANES), lambda b, hp: (b, 0, hp))
    in_specs.append(nat)
    out = pl.pallas_call(
        functools.partial(_attn_kernel, seq=S),
        grid=(B, pairs),
        in_specs=in_specs,
        out_specs=nat,
        out_shape=jax.ShapeDtypeStruct((B, S, ATT_DIM), BF16),
        scratch_shapes=[pltpu.VMEM((S, LANES), F32)] * (2 * N_GROUPS),
        compiler_params=pltpu.CompilerParams(dimension_semantics=("parallel", "parallel"),
                                             vmem_limit_bytes=VMEM_LIMIT_BYTES),
        name="dilated_attn",
    )(*operands)
    return out.reshape(B * S, ATT_DIM)


def _out_kernel(x_ref, oa_ref, ob_ref, gates_ref, wa_ref, wb_ref, wo_ref, fnw_ref, out_ref):
    def branches(rows):
        return _dot(oa_ref[rows, :], wa_ref[...]), _dot(ob_ref[rows, :], wb_ref[...])

    def finish(rows, ya, yb):
        g_a = gates_ref[rows, :D_MODEL].astype(F32)
        g_b = gates_ref[rows, D_MODEL:].astype(F32)
        merged = (g_a * ya + g_b * yb).astype(BF16)
        y = x_ref[rows, :] + _dot(merged, wo_ref[...])
        ms = jnp.mean(y * y, axis=-1, keepdims=True)
        out_ref[rows, :] = y * lax.rsqrt(ms + NORM_EPS) * fnw_ref[0:1, :]

    pending = None
    for r in range(x_ref.shape[0] // OUT_SUB):
        rows = slice(r * OUT_SUB, (r + 1) * OUT_SUB)
        nxt = (rows, *branches(rows))
        if pending is not None:
            finish(*pending)
        pending = nxt
    finish(*pending)


def _output(x2, oa, ob, gates, wa, wb, wo, fnw):
    T = x2.shape[0]
    tm = OUT_TM
    row = lambda n: pl.BlockSpec((tm, n), lambda i: (i, 0))
    const = lambda shape: pl.BlockSpec(shape, lambda i: (0, 0))
    return pl.pallas_call(
        _out_kernel,
        grid=(T // tm,),
        in_specs=[row(D_MODEL), row(HG_DIM), row(ATT_DIM), row(2 * D_MODEL),
                  const((HG_DIM, D_MODEL)), const((ATT_DIM, D_MODEL)), const((D_MODEL, D_MODEL)),
                  const((PAR_ROWS, D_MODEL))],
        out_specs=row(D_MODEL),
        out_shape=jax.ShapeDtypeStruct((T, D_MODEL), F32),
        compiler_params=pltpu.CompilerParams(dimension_semantics=("parallel",),
                                             vmem_limit_bytes=VMEM_LIMIT_BYTES),
        name="merge_out",
    )(x2, oa, ob, gates, wa, wb, wo, fnw)


def kernel(x, positions, norm_w, w_in, lb_logits, hgrn_norm_w, w_branch_a, w_branch_b, w_out, final_norm_w):
    B, S, D = x.shape
    assert D == D_MODEL and norm_w.shape[0] == 1 and lb_logits.shape == (2, HG_DIM)
    assert S % (ATT_STEPS * max(ATT_DILATIONS)) == 0 and S % HG_TS == 0 and (B * S) % OUT_TM == 0
    T = B * S
    x2 = x.reshape(T, D)
    pos_b = positions.reshape(T, 1)
    inv_freq = ROPE_THETA ** (-jnp.arange(0, ATT_HEAD_DIM, 2, dtype=F32) / ATT_HEAD_DIM)
    invf = jnp.tile(inv_freq, LANES // (ATT_HEAD_DIM // 2)).reshape(1, LANES)

    outs = _inproj(x2, pos_b, norm_w.reshape(1, D), w_in[0].astype(BF16), lb_logits.astype(F32), invf)
    qa, ka, lf, va, ga = outs[:5]
    qkv = [tuple(outs[5 + 3 * g:8 + 3 * g]) for g in range(N_GROUPS)]
    ag, gates = outs[14], outs[15]

    a_mat = jnp.asarray(_hgrn_sum_matrix(), BF16)
    hgrn_w = jnp.broadcast_to(jnp.tile(hgrn_norm_w.reshape(1, HG_VAL).astype(F32), (1, HG_HEADS)), (PAR_ROWS, HG_DIM))
    oa = _hgrn(qa, ka, lf, va, ga, a_mat, hgrn_w, B, S)
    ob = _attention(qkv, ag, B, S)
    out = _output(x2, oa, ob, gates, w_branch_a[0].astype(BF16), w_branch_b[0].astype(BF16),
                  w_out[0].astype(BF16), jnp.broadcast_to(final_norm_w.reshape(1, D).astype(F32), (PAR_ROWS, D)))
    return out.reshape(B, S, D)
```

```python
import functools

import numpy as np
import jax
import jax.numpy as jnp
from jax import lax
from jax.experimental import pallas as pl
from jax.experimental.pallas import tpu as pltpu

D_MODEL = 1024
HG_HEADS = 8
HG_KEY = 128
HG_VAL = 128
HG_DIM = HG_HEADS * HG_KEY
ATT_DILATIONS = (1, 4, 16)
ATT_STEPS = 128
N_GROUPS = 3
ATT_HEADS = 8
ATT_HEAD_DIM = 64
ATT_DIM = ATT_HEADS * ATT_HEAD_DIM
ROPE_THETA = 10000.0
ROPE_FREQS = ATT_HEAD_DIM // 2
ROPE_PACK = 4
NORM_EPS = 1e-6
LOG2_E = 1.4426950408889634
C_HQ, C_HF, C_HI, C_HG = 0, 1024, 2048, 3072
C_AQ, C_AK, C_AV = 4096, 4096 + 1536, 4096 + 3072
C_AG = 8704
C_GATES = 9216
IN_COLS = 11264

LANES = 128
VMEM_LIMIT_BYTES = 56 * 1024 * 1024

PAR_ROWS = 8

IN_TM = 256
IN_SEG = 512
PERM_STRIDE = 4
HG_CHUNK = 64
HG_TS = 1024
HG_LEVEL_SIZES = tuple(HG_CHUNK >> (l + 1) for l in range(HG_CHUNK.bit_length() - 1))
HG_ROW_TILE = 8
HG_GROUP = 2
HG_SKEW = 1
OUT_TM = 1024
OUT_SUB = 256
ATT_PAIRS = 2
ATT_SKEW = 3
NEG_BIG = -1e30

F32 = jnp.float32
BF16 = jnp.bfloat16


def _dot(a, b):
    return jnp.dot(a, b, preferred_element_type=F32)


def _dot_nt(a, b):
    return lax.dot_general(a, b, (((1,), (1,)), ((), ())), preferred_element_type=F32)


def _dot_tn(a, b):
    return lax.dot_general(a, b, (((0,), (0,)), ((), ())), preferred_element_type=F32)


def _sigmoid(z):
    return 0.5 + 0.5 * jnp.tanh(0.5 * z)


def _inproj_kernel(x0_ref, xn_ref, pos0_ref, posn_ref, nw_ref, w_ref, lbl_ref, invf_ref,
                   qa_ref, ka_ref, lf_ref, va_ref, ga_ref,
                   q1_ref, k1_ref, v1_ref, q2_ref, k2_ref, v2_ref, q3_ref, k3_ref, v3_ref,
                   ag_ref, gates_ref, perm_scr, h_scr, rope_scr):
    tm = xn_ref.shape[0]
    step = pl.program_id(0)
    slot = step % 2
    lane = lax.broadcasted_iota(jnp.int32, (tm, LANES), 1)
    first = (lane & (ATT_HEAD_DIM - 1)) < (ATT_HEAD_DIM // 2)

    def prepare(x_ref, pos_ref, dst):
        x = x_ref[...]
        ms = jnp.mean(x * x, axis=-1, keepdims=True)
        h_scr[dst] = (x * lax.rsqrt(ms + NORM_EPS) * nw_ref[...]).astype(BF16)

        ang = pos_ref[...].astype(F32) * invf_ref[...]
        lane_c = lax.broadcasted_iota(jnp.int32, ang.shape, 1)
        first_c = (lane_c & (ATT_HEAD_DIM - 1)) < (ATT_HEAD_DIM // 2)
        q_factor = ATT_HEAD_DIM ** -0.5 * LOG2_E
        for table, compact in ((0, jnp.cos(ang)), (1, jnp.sin(ang))):
            for g in range(ROPE_PACK):
                u = compact if g == 0 else pltpu.roll(compact, LANES - g * ROPE_FREQS, 1)
                u = jnp.where(lane_c < ROPE_FREQS, u, pltpu.roll(u, ROPE_FREQS, 1))
                u = jnp.where(lane_c < 2 * ROPE_FREQS, u, pltpu.roll(u, 2 * ROPE_FREQS, 1))
                if table == 1:
                    u = jnp.where(first_c, -u, u)
                rows = pl.ds(g, tm // ROPE_PACK, stride=ROPE_PACK)
                rope_scr[dst, table, rows, :] = u
                rope_scr[dst, table + 2, rows, :] = u * q_factor

    @pl.when(step == 0)
    def _():
        prepare(x0_ref, pos0_ref, 0)

    def proj(c0, n):
        return _dot(h_scr[slot], w_ref[:, c0:c0 + n])

    lbl = lbl_ref[...]
    lmax = jnp.max(lbl, axis=0, keepdims=True)
    lexp = jnp.exp(lbl - lmax)
    lb = lexp[0:1, :] / jnp.sum(lexp, axis=0, keepdims=True)

    def rope(zs, tables):
        rot = jnp.where(first, pltpu.roll(zs, LANES - ATT_HEAD_DIM // 2, 1), pltpu.roll(zs, ATT_HEAD_DIM // 2, 1))
        return zs * rope_scr[slot, tables] + rot * rope_scr[slot, tables + 1]

    def silu_to(ref, sl):
        def epi(z):
            hz = 0.5 * z
            ref[:, sl] = (hz + hz * jnp.tanh(hz)).astype(BF16)
        return epi

    def cast_to(ref, sl):
        def epi(z):
            ref[:, sl] = z.astype(BF16)
        return epi

    def sigmoid_to(ref, sl):
        def epi(z):
            ref[:, sl] = _sigmoid(z).astype(BF16)
        return epi

    def forget_to(sl):
        def epi(z):
            c = 0.5 - 0.5 * lb[:, sl]
            ct = c * jnp.tanh(0.5 * z)
            lf_ref[:, sl] = jnp.log2((lb[:, sl] + c) + ct)
            ka_ref[:, sl] = (c - ct).astype(BF16)
        return epi

    def attn_to(ref, d, slab, tables):
        def epi(z):
            for s in range(ATT_DIM // LANES):
                ls = slice(s * LANES, (s + 1) * LANES)
                val = z[:, ls]
                if tables is not None:
                    val = rope(val, tables)
                if d == 1:
                    ref[:, ls] = val.astype(BF16)
                    continue
                perm_scr[0, slab] = val
                if d > PERM_STRIDE:
                    for r1 in range(PERM_STRIDE):
                        perm_scr[1, slab, r1 * (tm // PERM_STRIDE):(r1 + 1) * (tm // PERM_STRIDE), :] = (
                            perm_scr[0, slab, pl.ds(r1, tm // PERM_STRIDE, stride=PERM_STRIDE), :])
                for r in range(d):
                    cols = slice(r * ATT_DIM + s * LANES, r * ATT_DIM + (s + 1) * LANES)
                    if d > PERM_STRIDE:
                        r1, r2 = r % PERM_STRIDE, r // PERM_STRIDE
                        rows = pl.ds(r1 * (tm // PERM_STRIDE) + r2, tm // d, stride=PERM_STRIDE)
                        ref[:, cols] = perm_scr[1, slab, rows, :].astype(BF16)
                    else:
                        ref[:, cols] = perm_scr[0, slab, pl.ds(r, tm // d, stride=d), :].astype(BF16)
        return epi

    halves = [slice(j * IN_SEG, (j + 1) * IN_SEG) for j in range(HG_DIM // IN_SEG)]
    segments = []
    for j, sl in enumerate(halves):
        segments += [(C_HQ + j * IN_SEG, IN_SEG, silu_to(qa_ref, sl)), (C_HF + j * IN_SEG, IN_SEG, forget_to(sl)),
                     (C_HG + j * IN_SEG, IN_SEG, silu_to(ga_ref, sl))]
    for j in range(2 * D_MODEL // IN_SEG):
        segments.append((C_GATES + j * IN_SEG, IN_SEG,
                         sigmoid_to(gates_ref, slice(j * IN_SEG, (j + 1) * IN_SEG))))
    segments.append((C_AG, ATT_DIM, silu_to(ag_ref, slice(0, ATT_DIM))))
    att_refs = ((q1_ref, k1_ref, v1_ref), (q2_ref, k2_ref, v2_ref), (q3_ref, k3_ref, v3_ref))
    for g in reversed(range(N_GROUPS)):
        d = ATT_DILATIONS[g]
        segments += [(C_AQ + g * ATT_DIM, ATT_DIM, attn_to(att_refs[g][0], d, 0, 2)),
                     (C_AK + g * ATT_DIM, ATT_DIM, attn_to(att_refs[g][1], d, 1, 0)),
                     (C_AV + g * ATT_DIM, ATT_DIM, attn_to(att_refs[g][2], d, 2, None))]
    for j, sl in enumerate(halves):
        segments.append((C_HI + j * IN_SEG, IN_SEG, cast_to(va_ref, sl)))

    z = proj(*segments[0][:2])
    prepare(xn_ref, posn_ref, 1 - slot)
    for i, (_, _, epilogue) in enumerate(segments):
        z_next = proj(*segments[i + 1][:2]) if i + 1 < len(segments) else None
        epilogue(z)
        z = z_next


def _inproj(x2, pos_b, norm_w, w_in_bf, lb_logits, invf):
    T = x2.shape[0]
    tm = IN_TM
    steps = T // tm
    row = lambda n: pl.BlockSpec((tm, n), lambda i: (i, 0))
    const = lambda shape: pl.BlockSpec(shape, lambda i: (0, 0))
    nxt = lambda n: pl.BlockSpec((tm, n), lambda i: (jnp.minimum(i + 1, steps - 1), 0))
    out_shapes = ([jax.ShapeDtypeStruct((T, HG_DIM), BF16),
                   jax.ShapeDtypeStruct((T, HG_DIM), BF16),
                   jax.ShapeDtypeStruct((T, HG_DIM), F32),
                   jax.ShapeDtypeStruct((T, HG_DIM), BF16),
                   jax.ShapeDtypeStruct((T, HG_DIM), BF16)]
                  + [jax.ShapeDtypeStruct((T // d, d * ATT_DIM), BF16) for d in ATT_DILATIONS for _ in range(3)]
                  + [jax.ShapeDtypeStruct((T, ATT_DIM), BF16),
                     jax.ShapeDtypeStruct((T, 2 * D_MODEL), BF16)])
    out_specs = ([row(HG_DIM)] * 5
                 + [pl.BlockSpec((tm // d, d * ATT_DIM), lambda i: (i, 0)) for d in ATT_DILATIONS for _ in range(3)]
                 + [row(ATT_DIM), row(2 * D_MODEL)])
    return pl.pallas_call(
        _inproj_kernel,
        grid=(steps,),
        in_specs=[const((tm, D_MODEL)), nxt(D_MODEL), const((tm // ROPE_PACK, LANES)),
                  pl.BlockSpec((tm // ROPE_PACK, LANES), lambda i: (jnp.minimum(i + 1, steps - 1), 0)),
                  const((1, D_MODEL)),
                  pl.BlockSpec((D_MODEL, IN_COLS), lambda i: (0, 0), pipeline_mode=pl.Buffered(1)),
                  const((2, HG_DIM)), const((1, LANES))],
        out_specs=out_specs,
        out_shape=out_shapes,
        scratch_shapes=[pltpu.VMEM((2, 3, tm, LANES), F32), pltpu.VMEM((2, tm, D_MODEL), BF16),
                        pltpu.VMEM((2, 4, tm, LANES), F32)],
        compiler_params=pltpu.CompilerParams(dimension_semantics=("arbitrary",),
                                             vmem_limit_bytes=VMEM_LIMIT_BYTES),
        name="inproj",
    )(x2, x2, pos_b, pos_b, norm_w, w_in_bf, lb_logits, invf)


def _hgrn_sum_matrix():
    C = HG_CHUNK
    fine = [m for m in HG_LEVEL_SIZES if m < HG_ROW_TILE]
    A = np.zeros(((1 + len(fine)) * C, C), np.float32)
    for t in range(C):
        A[t, :t + 1] = 1.0
    for i, m in enumerate(fine):
        for t in range(C):
            start = (t // m) * m
            if t & m:
                A[(1 + i) * C + t, start:t + 1] = 1.0
            else:
                A[(1 + i) * C + t, t + 1:start + m] = 1.0
    return np.concatenate([A, A], axis=1)


def _hgrn_kernel(q_ref, k_ref, lf_ref, v_ref, g_ref, a_ref, nw_ref, o_ref, state_scr, e_scr):
    C = HG_CHUNK
    n_lev = len(HG_LEVEL_SIZES)
    n_chunks = HG_TS // C

    @pl.when(pl.program_id(1) == 0)
    def _():
        state_scr[...] = jnp.zeros_like(state_scr)

    t_idx = lax.broadcasted_iota(jnp.int32, (C, HG_GROUP * C), 0)
    s_idx = lax.broadcasted_iota(jnp.int32, (C, HG_GROUP * C), 1) & (C - 1)
    diag_mask = t_idx == s_idx
    level_masks = []
    for m in HG_LEVEL_SIZES:
        sh = (2 * m).bit_length() - 1
        level_masks.append(((t_idx & m) != 0) & ((s_idx & m) == 0) & ((t_idx >> sh) == (s_idx >> sh)))
    row_idx = lax.broadcasted_iota(jnp.int32, (C, HG_KEY), 0)
    right_rows = [(row_idx & m) != 0 for m in HG_LEVEL_SIZES]

    def exponents(ci):
        slot = ci % 2
        lf = lf_ref[ci * C:(ci + 1) * C, :]
        hi = lf.astype(BF16)
        lo = (lf - hi.astype(F32)).astype(BF16)
        sums = _dot(a_ref[...], jnp.concatenate([hi, lo], axis=0))
        b = sums[0:C, :]

        def put(i, arg):
            e_scr[slot, i * C:(i + 1) * C, :] = jnp.exp2(arg)

        i_fine = 1
        for i, m in enumerate(HG_LEVEL_SIZES):
            if m >= HG_ROW_TILE:
                parts = []
                for j in range(C // (2 * m)):
                    ref = jnp.broadcast_to(b[j * 2 * m + m - 1:j * 2 * m + m, :], (m, HG_DIM))
                    parts += [ref - b[j * 2 * m:j * 2 * m + m, :], b[j * 2 * m + m:(j + 1) * 2 * m, :] - ref]
                put(i, jnp.concatenate(parts, axis=0))
            else:
                put(i, sums[i_fine * C:(i_fine + 1) * C, :])
                i_fine += 1
        put(n_lev, b)
        put(n_lev + 1, b[C - 1:C, :] - b)

    def block_diag(xs):
        zero = jnp.zeros_like(xs[0])
        return jnp.concatenate([jnp.concatenate([x if b == a else zero for b in range(len(xs))], axis=1)
                                for a, x in enumerate(xs)], axis=0)

    def pair_scores(ci, hp):
        slot = ci % 2
        rows = slice(ci * C, (ci + 1) * C)
        lanes = [slice(h * HG_KEY, (h + 1) * HG_KEY) for h in range(HG_GROUP * hp, HG_GROUP * (hp + 1))]
        q_bf =[q_ref[rows, hl] for hl in lanes]
        k_bf = [k_ref[rows, hl] for hl in lanes]
        q = [x.astype(F32) for x in q_bf]
        k = [x.astype(F32) for x in k_bf]
        scores = jnp.where(diag_mask, _dot_nt(jnp.concatenate(q_bf, axis=1), block_diag(k_bf)), 0.0)
        for i, m in enumerate(HG_LEVEL_SIZES):
            zs = []
            for a, hl in enumerate(lanes):
                if m >= HG_ROW_TILE:
                    w = jnp.concatenate([(q[a] if j % 2 else k[a])[j * m:(j + 1) * m, :] for j in range(C // m)],
                                        axis=0)
                else:
                    w = jnp.where(right_rows[i], q[a], k[a])
                zs.append((w * e_scr[slot, i * C:(i + 1) * C, hl]).astype(BF16))
            scores = jnp.where(level_masks[i], _dot_nt(jnp.concatenate(zs, axis=1), block_diag(zs)), scores)
        q_dec = [(q[a] * e_scr[slot, n_lev * C:(n_lev + 1) * C, hl]).astype(BF16) for a, hl in enumerate(lanes)]
        k_dec = [(k[a] * e_scr[slot, (n_lev + 1) * C:(n_lev + 2) * C, hl]).astype(BF16) for a, hl in enumerate(lanes)]
        return scores, q_dec, k_dec

    def finish(ci, hp, stage1):
        scores, q_dec, k_dec = stage1
        slot = ci % 2
        rows = slice(ci * C, (ci + 1) * C)
        heads = tuple(range(HG_GROUP * hp, HG_GROUP * (hp + 1)))
        lanes = [slice(h * HG_KEY, (h + 1) * HG_KEY) for h in heads]
        v_bf = [v_ref[rows, hl] for hl in lanes]
        o_pair = _dot(scores.astype(BF16), block_diag(v_bf))
        for a, (h, hl) in enumerate(zip(heads, lanes)):
            e_last = e_scr[slot, (n_lev + 1) * C - 1:(n_lev + 1) * C, hl]
            st = state_scr[h]
            o = o_pair[:, a * HG_VAL:(a + 1) * HG_VAL] + _dot_nt(q_dec[a], st.astype(BF16))
            state_scr[h] = st * e_last + _dot_tn(v_bf[a], k_dec[a])
            ms = jnp.mean(o * o, axis=-1, keepdims=True)
            y = o * lax.rsqrt(ms + NORM_EPS) * nw_ref[0:1, 0:HG_VAL] * g_ref[rows, hl].astype(F32)
            o_ref[rows, hl] = y.astype(BF16)

    n_pairs = HG_HEADS // HG_GROUP
    exponents(0)
    pending = []
    for ci in range(n_chunks):
        for hp in range(n_pairs):
            if hp == n_pairs // 2 and ci + 1 < n_chunks:
                exponents(ci + 1)
            pending.append((ci, hp, pair_scores(ci, hp)))
            if len(pending) > HG_SKEW:
                finish(*pending.pop(0))
    while pending:
        finish(*pending.pop(0))


def _hgrn(qa, ka, lf, va, ga, a_mat, hgrn_w, B, S):
    ts = HG_TS
    steps = S // ts
    blk = pl.BlockSpec((ts, HG_DIM), lambda b, i: (b * steps + i, 0))
    rows_e = (len(HG_LEVEL_SIZES) + 2) * HG_CHUNK
    return pl.pallas_call(
        _hgrn_kernel,
        grid=(B, steps),
        in_specs=[blk, blk, blk, blk, blk,
                  pl.BlockSpec(a_mat.shape, lambda b, i: (0, 0)),
                  pl.BlockSpec((PAR_ROWS, HG_DIM), lambda b, i: (0, 0))],
        out_specs=blk,
        out_shape=jax.ShapeDtypeStruct((B * S, HG_DIM), BF16),
        scratch_shapes=[pltpu.VMEM((HG_HEADS, HG_VAL, HG_KEY), F32),
                        pltpu.VMEM((2, rows_e, HG_DIM), F32)],
        compiler_params=pltpu.CompilerParams(dimension_semantics=("parallel", "arbitrary"),
                                             vmem_limit_bytes=VMEM_LIMIT_BYTES),
        name="hgrn2",
    )(qa, ka, lf, va, ga, a_mat, hgrn_w)


def _attn_kernel(*refs, seq):
    n_in = 3 * sum(ATT_DILATIONS) + 1
    in_refs = refs[:n_in]
    ag_ref = in_refs[-1]
    out_ref = refs[n_in]
    o_scr = refs[n_in + 1:n_in + 1 + N_GROUPS]
    lse_scr = refs[n_in + 1 + N_GROUPS:n_in + 1 + 2 * N_GROUPS]
    n = ATT_STEPS

    lane = lax.broadcasted_iota(jnp.int32, (n, LANES), 1)
    head_a = lane < ATT_HEAD_DIM
    qi = lax.broadcasted_iota(jnp.int32, (n, 2 * n), 0)
    kj = lax.broadcasted_iota(jnp.int32, (n, 2 * n), 1)
    band_ok = (kj >= qi) & (kj <= qi + n)
    first_ok = (lax.broadcasted_iota(jnp.int32, (n, n), 1)
                <= lax.broadcasted_iota(jnp.int32, (n, n), 0))

    def scores(q_ref, k_ref, c, pl_):
        q = q_ref[c * n:(c + 1) * n, pl_]
        kk = k_ref[max(c - 1, 0) * n:(c + 1) * n, pl_]
        zero_q = jnp.zeros_like(q)
        return [_dot_nt(jnp.where(head_a, q, zero_q), kk), _dot_nt(jnp.where(head_a, zero_q, q), kk)]

    def finish(g, d, r, c, p, v_ref, s_pair):
        ok = first_ok if c == 0 else band_ok
        vv = v_ref[max(c - 1, 0) * n:(c + 1) * n, p * LANES:(p + 1) * LANES]
        ps, ms = [], []
        for s in s_pair:
            s = jnp.where(ok, s, NEG_BIG)
            sm = s if c == 0 else jnp.maximum(s[:, :n], s[:, n:])
            m = jnp.max(sm, axis=-1, keepdims=True)
            ps.append(jnp.exp2(s - m).astype(BF16))
            ms.append(m)
        ol = _dot(jnp.concatenate(ps, axis=0), jnp.concatenate([vv, jnp.ones_like(vv)], axis=1))
        l = jnp.where(head_a, ol[:n, LANES:], ol[n:, LANES:])
        o = jnp.where(head_a, ol[:n, :LANES], ol[n:, :LANES]) * (1.0 / l)
        lse = jnp.where(head_a, ms[0], ms[1]) + jnp.log2(l)
        dst = pl.ds(c * n, n) if d == 1 else pl.ds(c * (n * d) + r, n, stride=d)
        o_scr[g][p, dst, :] = o
        lse_scr[g][p, dst, :] = lse

    pending = []
    for p in range(ATT_PAIRS):
        pos = 0
        for g, d in enumerate(ATT_DILATIONS):
            q_refs = in_refs[pos:pos + d]
            k_refs = in_refs[pos + d:pos + 2 * d]
            v_refs = in_refs[pos + 2 * d:pos + 3 * d]
            pos += 3 * d
            for r in range(d):
                for c in range(seq // d // n):
                    s_pair = scores(q_refs[r], k_refs[r], c, slice(p * LANES, (p + 1) * LANES))
                    pending.append((g, d, r, c, p, v_refs[r], s_pair))
                    if len(pending) > ATT_SKEW:
                        finish(*pending.pop(0))
    while pending:
        finish(*pending.pop(0))

    rows_per = 256

    def merge(i, carry):
        rows = pl.ds(pl.multiple_of(i * rows_per, rows_per), rows_per)
        for p in range(ATT_PAIRS):
            lanes = slice(p * LANES, (p + 1) * LANES)
            l1, l2, l3 = lse_scr[0][p, rows, :], lse_scr[1][p, rows, :], lse_scr[2][p, rows, :]
            mx = jnp.maximum(jnp.maximum(l1, l2), l3)
            w1, w2, w3 = jnp.exp2(l1 - mx), jnp.exp2(l2 - mx), jnp.exp2(l3 - mx)
            num = w1 * o_scr[0][p, rows, :] + w2 * o_scr[1][p, rows, :] + w3 * o_scr[2][p, rows, :]
            ob = num / (w1 + w2 + w3) * ag_ref[rows, lanes].astype(F32)
            out_ref[rows, lanes] = ob.astype(BF16)
        return carry

    lax.fori_loop(0, seq // rows_per, merge, 0)


def _attention(qkv, ag, B, S):
    width = ATT_PAIRS * LANES
    steps = ATT_DIM // width
    operands, in_specs = [], []
    for g, d in enumerate(ATT_DILATIONS):
        m_len = S // d
        for arr in qkv[g]:
            view = arr.reshape(B, m_len, d * ATT_DIM)
            for r in range(d):
                operands.append(view)
                in_specs.append(pl.BlockSpec((None, m_len, width),
                                             functools.partial(lambda b, hp, r: (b, 0, r * steps + hp), r=r)))
    operands.append(ag.reshape(B, S, ATT_DIM))
    nat = pl.BlockSpec((None, S, width), lambda b, hp: (b, 0, hp))
    in_specs.append(nat)
    out = pl.pallas_call(
        functools.partial(_attn_kernel, seq=S),
        grid=(B, steps),
        in_specs=in_specs,
        out_specs=nat,
        out_shape=jax.ShapeDtypeStruct((B, S, ATT_DIM), BF16),
        scratch_shapes=[pltpu.VMEM((ATT_PAIRS, S, LANES), F32)] * (2 * N_GROUPS),
        compiler_params=pltpu.CompilerParams(dimension_semantics=("parallel", "parallel"),
                                             vmem_limit_bytes=VMEM_LIMIT_BYTES),
        name="dilated_attn",
    )(*operands)
    return out.reshape(B * S, ATT_DIM)


def _out_kernel(x_ref, oa_ref, ob_ref, gates_ref, wa_ref, wb_ref, wo_ref, fnw_ref, out_ref):
    def branches(rows):
        return _dot(oa_ref[rows, :], wa_ref[...]), _dot(ob_ref[rows, :], wb_ref[...])

    def finish(rows, ya, yb):
        g_a = gates_ref[rows, :D_MODEL].astype(F32)
        g_b = gates_ref[rows, D_MODEL:].astype(F32)
        merged = (g_a * ya + g_b * yb).astype(BF16)
        y = x_ref[rows, :] + _dot(merged, wo_ref[...])
        ms = jnp.mean(y * y, axis=-1, keepdims=True)
        out_ref[rows, :] = y * lax.rsqrt(ms + NORM_EPS) * fnw_ref[0:1, :]

    pending = None
    for r in range(x_ref.shape[0] // OUT_SUB):
        rows = slice(r * OUT_SUB, (r + 1) * OUT_SUB)
        nxt = (rows, *branches(rows))
        if pending is not None:
            finish(*pending)
        pending = nxt
    finish(*pending)


def _output(x2, oa, ob, gates, wa, wb, wo, fnw):
    T = x2.shape[0]
    tm = OUT_TM
    row = lambda n: pl.BlockSpec((tm, n), lambda i: (i, 0))
    const = lambda shape: pl.BlockSpec(shape, lambda i: (0, 0))
    return pl.pallas_call(
        _out_kernel,
        grid=(T // tm,),
        in_specs=[row(D_MODEL), row(HG_DIM), row(ATT_DIM), row(2 * D_MODEL),
                  const((HG_DIM, D_MODEL)), const((ATT_DIM, D_MODEL)), const((D_MODEL, D_MODEL)),
                  const((PAR_ROWS, D_MODEL))],
        out_specs=row(D_MODEL),
        out_shape=jax.ShapeDtypeStruct((T, D_MODEL), F32),
        compiler_params=pltpu.CompilerParams(dimension_semantics=("parallel",),
                                             vmem_limit_bytes=VMEM_LIMIT_BYTES),
        name="merge_out",
    )(x2, oa, ob, gates, wa, wb, wo, fnw)


def kernel(x, positions, norm_w, w_in, lb_logits, hgrn_norm_w, w_branch_a, w_branch_b, w_out, final_norm_w):
    B, S, D = x.shape
    assert D == D_MODEL and norm_w.shape[0] == 1 and lb_logits.shape == (2, HG_DIM)
    assert S % (ATT_STEPS * max(ATT_DILATIONS)) == 0 and S % HG_TS == 0 and (B * S) % OUT_TM == 0
    T = B * S
    x2 = x.reshape(T, D)
    pos_b = jnp.repeat(positions.reshape(T // ROPE_PACK, ROPE_PACK), ROPE_FREQS, axis=1)
    inv_freq = ROPE_THETA ** (-jnp.arange(0, ATT_HEAD_DIM, 2, dtype=F32) / ATT_HEAD_DIM)
    invf = jnp.tile(inv_freq, LANES // ROPE_FREQS).reshape(1, LANES)

    outs = _inproj(x2, pos_b, norm_w.reshape(1, D), w_in[0].astype(BF16), lb_logits.astype(F32), invf)
    qa, ka, lf, va, ga = outs[:5]
    qkv = [tuple(outs[5 + 3 * g:8 + 3 * g]) for g in range(N_GROUPS)]
    ag, gates = outs[14], outs[15]

    a_mat = jnp.asarray(_hgrn_sum_matrix(), BF16)
    hgrn_w = jnp.broadcast_to(jnp.tile(hgrn_norm_w.reshape(1, HG_VAL).astype(F32), (1, HG_HEADS)), (PAR_ROWS, HG_DIM))
    oa = _hgrn(qa, ka, lf, va, ga, a_mat, hgrn_w, B, S)
    ob = _attention(qkv, ag, B, S)
    out = _output(x2, oa, ob, gates, w_branch_a[0].astype(BF16), w_branch_b[0].astype(BF16),
                  w_out[0].astype(BF16), jnp.broadcast_to(final_norm_w.reshape(1, D).astype(F32), (PAR_ROWS, D)))
    return out.reshape(B, S, D)
```

```python
import functools

import numpy as np
import jax
import jax.numpy as jnp
from jax import lax
from jax.experimental import pallas as pl
from jax.experimental.pallas import tpu as pltpu

D_MODEL = 1024
HG_HEADS = 8
HG_KEY = 128
HG_VAL = 128
HG_DIM = HG_HEADS * HG_KEY
ATT_DILATIONS = (1, 4, 16)
ATT_STEPS = 128
N_GROUPS = 3
ATT_HEADS = 8
ATT_HEAD_DIM = 64
ATT_DIM = ATT_HEADS * ATT_HEAD_DIM
ROPE_THETA = 10000.0
ROPE_FREQS = ATT_HEAD_DIM // 2
ROPE_PACK = 4
NORM_EPS = 1e-6
LOG2_E = 1.4426950408889634
C_HQ, C_HF, C_HI, C_HG = 0, 1024, 2048, 3072
C_AQ, C_AK, C_AV = 4096, 4096 + 1536, 4096 + 3072
C_AG = 8704
C_GATES = 9216
IN_COLS = 11264

LANES = 128
VMEM_LIMIT_BYTES = 56 * 1024 * 1024

PAR_ROWS = 8

IN_TM = 256
IN_SEG = 512
PERM_STRIDE = 4
HG_CHUNK = 64
HG_TS = 1024
HG_LEVEL_SIZES = tuple(HG_CHUNK >> (l + 1) for l in range(HG_CHUNK.bit_length() - 1))
HG_ROW_TILE = 8
HG_VPU_LEVEL = 1
HG_GROUP = 2
HG_SKEW = 1
OUT_TM = 1024
OUT_SUB = 256
ATT_PAIRS = 2
ATT_SKEW = 3
NEG_BIG = -1e30

F32 = jnp.float32
BF16 = jnp.bfloat16


def _dot(a, b):
    return jnp.dot(a, b, preferred_element_type=F32)


def _dot_nt(a, b):
    return lax.dot_general(a, b, (((1,), (1,)), ((), ())), preferred_element_type=F32)


def _dot_tn(a, b):
    return lax.dot_general(a, b, (((0,), (0,)), ((), ())), preferred_element_type=F32)


def _sigmoid(z):
    return 0.5 + 0.5 * jnp.tanh(0.5 * z)


def _inproj_kernel(x0_ref, xn_ref, pos0_ref, posn_ref, nw_ref, w_ref, lbl_ref, invf_ref,
                   qa_ref, ka_ref, lf_ref, va_ref, ga_ref,
                   q1_ref, k1_ref, v1_ref, q2_ref, k2_ref, v2_ref, q3_ref, k3_ref, v3_ref,
                   ag_ref, gates_ref, perm_scr, h_scr, rope_scr):
    tm = xn_ref.shape[0]
    step = pl.program_id(0)
    slot = step % 2
    lane = lax.broadcasted_iota(jnp.int32, (tm, LANES), 1)
    first = (lane & (ATT_HEAD_DIM - 1)) < (ATT_HEAD_DIM // 2)

    def prepare(x_ref, pos_ref, dst):
        x = x_ref[...]
        ms = jnp.mean(x * x, axis=-1, keepdims=True)
        h_scr[dst] = (x * lax.rsqrt(ms + NORM_EPS) * nw_ref[...]).astype(BF16)

        ang = pos_ref[...].astype(F32) * invf_ref[...]
        lane_c = lax.broadcasted_iota(jnp.int32, ang.shape, 1)
        first_c = (lane_c & (ATT_HEAD_DIM - 1)) < (ATT_HEAD_DIM // 2)
        q_factor = ATT_HEAD_DIM ** -0.5 * LOG2_E
        for table, compact in ((0, jnp.cos(ang)), (1, jnp.sin(ang))):
            for g in range(ROPE_PACK):
                u = compact if g == 0 else pltpu.roll(compact, LANES - g * ROPE_FREQS, 1)
                u = jnp.where(lane_c < ROPE_FREQS, u, pltpu.roll(u, ROPE_FREQS, 1))
                u = jnp.where(lane_c < 2 * ROPE_FREQS, u, pltpu.roll(u, 2 * ROPE_FREQS, 1))
                if table == 1:
                    u = jnp.where(first_c, -u, u)
                rows = pl.ds(g, tm // ROPE_PACK, stride=ROPE_PACK)
                rope_scr[dst, table, rows, :] = u
                rope_scr[dst, table + 2, rows, :] = u * q_factor

    @pl.when(step == 0)
    def _():
        prepare(x0_ref, pos0_ref, 0)

    def proj(c0, n):
        return _dot(h_scr[slot], w_ref[:, c0:c0 + n])

    lbl = lbl_ref[...]
    lmax = jnp.max(lbl, axis=0, keepdims=True)
    lexp = jnp.exp(lbl - lmax)
    lb = lexp[0:1, :] / jnp.sum(lexp, axis=0, keepdims=True)

    def rope(zs, tables):
        rot = jnp.where(first, pltpu.roll(zs, LANES - ATT_HEAD_DIM // 2, 1), pltpu.roll(zs, ATT_HEAD_DIM // 2, 1))
        return zs * rope_scr[slot, tables] + rot * rope_scr[slot, tables + 1]

    def silu_to(ref, sl):
        def epi(z):
            hz = 0.5 * z
            ref[:, sl] = (hz + hz * jnp.tanh(hz)).astype(BF16)
        return epi

    def cast_to(ref, sl):
        def epi(z):
            ref[:, sl] = z.astype(BF16)
        return epi

    def sigmoid_to(ref, sl):
        def epi(z):
            ref[:, sl] = _sigmoid(z).astype(BF16)
        return epi

    def forget_to(sl):
        def epi(z):
            c = 0.5 - 0.5 * lb[:, sl]
            ct = c * jnp.tanh(0.5 * z)
            lf_ref[:, sl] = jnp.log2((lb[:, sl] + c) + ct)
            ka_ref[:, sl] = (c - ct).astype(BF16)
        return epi

    def attn_to(ref, d, slab, tables):
        def epi(z):
            for s in range(ATT_DIM // LANES):
                ls = slice(s * LANES, (s + 1) * LANES)
                val = z[:, ls]
                if tables is not None:
                    val = rope(val, tables)
                if d == 1:
                    ref[:, ls] = val.astype(BF16)
                    continue
                perm_scr[0, slab] = val
                if d > PERM_STRIDE:
                    for r1 in range(PERM_STRIDE):
                        perm_scr[1, slab, r1 * (tm // PERM_STRIDE):(r1 + 1) * (tm // PERM_STRIDE), :] = (
                            perm_scr[0, slab, pl.ds(r1, tm // PERM_STRIDE, stride=PERM_STRIDE), :])
                for r in range(d):
                    cols = slice(r * ATT_DIM + s * LANES, r * ATT_DIM + (s + 1) * LANES)
                    if d > PERM_STRIDE:
                        r1, r2 = r % PERM_STRIDE, r // PERM_STRIDE
                        rows = pl.ds(r1 * (tm // PERM_STRIDE) + r2, tm // d, stride=PERM_STRIDE)
                        ref[:, cols] = perm_scr[1, slab, rows, :].astype(BF16)
                    else:
                        ref[:, cols] = perm_scr[0, slab, pl.ds(r, tm // d, stride=d), :].astype(BF16)
        return epi

    halves = [slice(j * IN_SEG, (j + 1) * IN_SEG) for j in range(HG_DIM // IN_SEG)]
    segments = []
    for j, sl in enumerate(halves):
        segments += [(C_HQ + j * IN_SEG, IN_SEG, silu_to(qa_ref, sl)), (C_HF + j * IN_SEG, IN_SEG, forget_to(sl)),
                     (C_HG + j * IN_SEG, IN_SEG, silu_to(ga_ref, sl))]
    for j in range(2 * D_MODEL // IN_SEG):
        segments.append((C_GATES + j * IN_SEG, IN_SEG,
                         sigmoid_to(gates_ref, slice(j * IN_SEG, (j + 1) * IN_SEG))))
    segments.append((C_AG, ATT_DIM, silu_to(ag_ref, slice(0, ATT_DIM))))
    att_refs = ((q1_ref, k1_ref, v1_ref), (q2_ref, k2_ref, v2_ref), (q3_ref, k3_ref, v3_ref))
    for g in reversed(range(N_GROUPS)):
        d = ATT_DILATIONS[g]
        segments += [(C_AQ + g * ATT_DIM, ATT_DIM, attn_to(att_refs[g][0], d, 0, 2)),
                     (C_AK + g * ATT_DIM, ATT_DIM, attn_to(att_refs[g][1], d, 1, 0)),
                     (C_AV + g * ATT_DIM, ATT_DIM, attn_to(att_refs[g][2], d, 2, None))]
    for j, sl in enumerate(halves):
        segments.append((C_HI + j * IN_SEG, IN_SEG, cast_to(va_ref, sl)))

    z = proj(*segments[0][:2])
    prepare(xn_ref, posn_ref, 1 - slot)
    for i, (_, _, epilogue) in enumerate(segments):
        z_next = proj(*segments[i + 1][:2]) if i + 1 < len(segments) else None
        epilogue(z)
        z = z_next


def _inproj(x2, pos_b, norm_w, w_in_bf, lb_logits, invf):
    T = x2.shape[0]
    tm = IN_TM
    steps = T // tm
    row = lambda n: pl.BlockSpec((tm, n), lambda i: (i, 0))
    const = lambda shape: pl.BlockSpec(shape, lambda i: (0, 0))
    nxt = lambda n: pl.BlockSpec((tm, n), lambda i: (jnp.minimum(i + 1, steps - 1), 0))
    out_shapes = ([jax.ShapeDtypeStruct((T, HG_DIM), BF16),
                   jax.ShapeDtypeStruct((T, HG_DIM), BF16),
                   jax.ShapeDtypeStruct((T, HG_DIM), F32),
                   jax.ShapeDtypeStruct((T, HG_DIM), BF16),
                   jax.ShapeDtypeStruct((T, HG_DIM), BF16)]
                  + [jax.ShapeDtypeStruct((T // d, d * ATT_DIM), BF16) for d in ATT_DILATIONS for _ in range(3)]
                  + [jax.ShapeDtypeStruct((T, ATT_DIM), BF16),
                     jax.ShapeDtypeStruct((T, 2 * D_MODEL), BF16)])
    out_specs = ([row(HG_DIM)] * 5
                 + [pl.BlockSpec((tm // d, d * ATT_DIM), lambda i: (i, 0)) for d in ATT_DILATIONS for _ in range(3)]
                 + [row(ATT_DIM), row(2 * D_MODEL)])
    return pl.pallas_call(
        _inproj_kernel,
        grid=(steps,),
        in_specs=[const((tm, D_MODEL)), nxt(D_MODEL), const((tm // ROPE_PACK, LANES)),
                  pl.BlockSpec((tm // ROPE_PACK, LANES), lambda i: (jnp.minimum(i + 1, steps - 1), 0)),
                  const((1, D_MODEL)),
                  pl.BlockSpec((D_MODEL, IN_COLS), lambda i: (0, 0), pipeline_mode=pl.Buffered(1)),
                  const((2, HG_DIM)), const((1, LANES))],
        out_specs=out_specs,
        out_shape=out_shapes,
        scratch_shapes=[pltpu.VMEM((2, 3, tm, LANES), F32), pltpu.VMEM((2, tm, D_MODEL), BF16),
                        pltpu.VMEM((2, 4, tm, LANES), F32)],
        compiler_params=pltpu.CompilerParams(dimension_semantics=("arbitrary",),
                                             vmem_limit_bytes=VMEM_LIMIT_BYTES),
        name="inproj",
    )(x2, x2, pos_b, pos_b, norm_w, w_in_bf, lb_logits, invf)


def _hgrn_sum_matrix():
    C = HG_CHUNK
    fine = [m for m in HG_LEVEL_SIZES if HG_VPU_LEVEL < m < HG_ROW_TILE]
    A = np.zeros(((1 + len(fine)) * C, C), np.float32)
    for t in range(C):
        A[t, :t + 1] = 1.0
    for i, m in enumerate(fine):
        for t in range(C):
            start = (t // m) * m
            if t & m:
                A[(1 + i) * C + t, start:t + 1] = 1.0
            else:
                A[(1 + i) * C + t, t + 1:start + m] = 1.0
    return np.concatenate([A, A], axis=1)


def _hgrn_kernel(q_ref, k_ref, lf_ref, v_ref, g_ref, a_ref, nw_ref, o_ref, state_scr, e_scr):
    C = HG_CHUNK
    n_lev = len(HG_LEVEL_SIZES)
    n_chunks = HG_TS // C

    @pl.when(pl.program_id(1) == 0)
    def _():
        state_scr[...] = jnp.zeros_like(state_scr)

    t_idx = lax.broadcasted_iota(jnp.int32, (C, HG_GROUP * C), 0)
    s_idx = lax.broadcasted_iota(jnp.int32, (C, HG_GROUP * C), 1) & (C - 1)
    diag_mask = t_idx == s_idx
    level_masks = []
    for m in HG_LEVEL_SIZES:
        sh = (2 * m).bit_length() - 1
        level_masks.append(((t_idx & m) != 0) & ((s_idx & m) == 0) & ((t_idx >> sh) == (s_idx >> sh)))
    row_full = lax.broadcasted_iota(jnp.int32, (C, HG_DIM), 0)
    row_idx = lax.broadcasted_iota(jnp.int32, (C, HG_KEY), 0)
    right_rows = [(row_idx & m) != 0 for m in HG_LEVEL_SIZES]

    def exponents(ci):
        slot = ci % 2
        lf = lf_ref[ci * C:(ci + 1) * C, :]
        hi = lf.astype(BF16)
        lo = (lf - hi.astype(F32)).astype(BF16)
        sums = _dot(a_ref[...], jnp.concatenate([hi, lo], axis=0))
        b = sums[0:C, :]

        def put(i, arg):
            e_scr[slot, i * C:(i + 1) * C, :] = jnp.exp2(arg)

        i_fine = 1
        for i, m in enumerate(HG_LEVEL_SIZES):
            if m >= HG_ROW_TILE:
                parts = []
                for j in range(C // (2 * m)):
                    ref = jnp.broadcast_to(b[j * 2 * m + m - 1:j * 2 * m + m, :], (m, HG_DIM))
                    parts += [ref - b[j * 2 * m:j * 2 * m + m, :], b[j * 2 * m + m:(j + 1) * 2 * m, :] - ref]
                put(i, jnp.concatenate(parts, axis=0))
            elif m > HG_VPU_LEVEL:
                put(i, sums[i_fine * C:(i_fine + 1) * C, :])
                i_fine += 1
            elif m == 2:
                nxt = pltpu.roll(lf, C - 1, 0)
                prv = pltpu.roll(lf, 1, 0)
                pos4 = row_full & 3
                put(i, jnp.where(pos4 == 0, nxt, jnp.where(pos4 == 1, 0.0, jnp.where(pos4 == 2, lf, lf + prv))))
            else:
                put(i, jnp.where((row_full & 1) == 1, lf, 0.0))
        put(n_lev, b)
        put(n_lev + 1, b[C - 1:C, :] - b)

    def block_diag(xs):
        zero = jnp.zeros_like(xs[0])
        return jnp.concatenate([jnp.concatenate([x if b == a else zero for b in range(len(xs))], axis=1)
                                for a, x in enumerate(xs)], axis=0)

    def pair_scores(ci, hp):
        slot = ci % 2
        rows = slice(ci * C, (ci + 1) * C)
        lanes = [slice(h * HG_KEY, (h + 1) * HG_KEY) for h in range(HG_GROUP * hp, HG_GROUP * (hp + 1))]
        q_bf =[q_ref[rows, hl] for hl in lanes]
        k_bf = [k_ref[rows, hl] for hl in lanes]
        q = [x.astype(F32) for x in q_bf]
        k = [x.astype(F32) for x in k_bf]
        scores = jnp.where(diag_mask, _dot_nt(jnp.concatenate(q_bf, axis=1), block_diag(k_bf)), 0.0)
        for i, m in enumerate(HG_LEVEL_SIZES):
            zs = []
            for a, hl in enumerate(lanes):
                if m >= HG_ROW_TILE:
                    w = jnp.concatenate([(q[a] if j % 2 else k[a])[j * m:(j + 1) * m, :] for j in range(C // m)],
                                        axis=0)
                else:
                    w = jnp.where(right_rows[i], q[a], k[a])
                zs.append((w * e_scr[slot, i * C:(i + 1) * C, hl]).astype(BF16))
            scores = jnp.where(level_masks[i], _dot_nt(jnp.concatenate(zs, axis=1), block_diag(zs)), scores)
        q_dec = [(q[a] * e_scr[slot, n_lev * C:(n_lev + 1) * C, hl]).astype(BF16) for a, hl in enumerate(lanes)]
        k_dec = [(k[a] * e_scr[slot, (n_lev + 1) * C:(n_lev + 2) * C, hl]).astype(BF16) for a, hl in enumerate(lanes)]
        return scores, q_dec, k_dec

    def finish(ci, hp, stage1):
        scores, q_dec, k_dec = stage1
        slot = ci % 2
        rows = slice(ci * C, (ci + 1) * C)
        heads = tuple(range(HG_GROUP * hp, HG_GROUP * (hp + 1)))
        lanes = [slice(h * HG_KEY, (h + 1) * HG_KEY) for h in heads]
        v_bf = [v_ref[rows, hl] for hl in lanes]
        o_pair = _dot(scores.astype(BF16), block_diag(v_bf))
        for a, (h, hl) in enumerate(zip(heads, lanes)):
            e_last = e_scr[slot, (n_lev + 1) * C - 1:(n_lev + 1) * C, hl]
            st = state_scr[h]
            o = o_pair[:, a * HG_VAL:(a + 1) * HG_VAL] + _dot_nt(q_dec[a], st.astype(BF16))
            state_scr[h] = st * e_last + _dot_tn(v_bf[a], k_dec[a])
            ms = jnp.mean(o * o, axis=-1, keepdims=True)
            y = o * lax.rsqrt(ms + NORM_EPS) * nw_ref[0:1, 0:HG_VAL] * g_ref[rows, hl].astype(F32)
            o_ref[rows, hl] = y.astype(BF16)

    n_pairs = HG_HEADS // HG_GROUP
    exponents(0)
    pending = []
    for ci in range(n_chunks):
        for hp in range(n_pairs):
            if hp == n_pairs // 2 and ci + 1 < n_chunks:
                exponents(ci + 1)
            pending.append((ci, hp, pair_scores(ci, hp)))
            if len(pending) > HG_SKEW:
                finish(*pending.pop(0))
    while pending:
        finish(*pending.pop(0))


def _hgrn(qa, ka, lf, va, ga, a_mat, hgrn_w, B, S):
    ts = HG_TS
    steps = S // ts
    blk = pl.BlockSpec((ts, HG_DIM), lambda b, i: (b * steps + i, 0))
    rows_e = (len(HG_LEVEL_SIZES) + 2) * HG_CHUNK
    return pl.pallas_call(
        _hgrn_kernel,
        grid=(B, steps),
        in_specs=[blk, blk, blk, blk, blk,
                  pl.BlockSpec(a_mat.shape, lambda b, i: (0, 0)),
                  pl.BlockSpec((PAR_ROWS, HG_DIM), lambda b, i: (0, 0))],
        out_specs=blk,
        out_shape=jax.ShapeDtypeStruct((B * S, HG_DIM), BF16),
        scratch_shapes=[pltpu.VMEM((HG_HEADS, HG_VAL, HG_KEY), F32),
                        pltpu.VMEM((2, rows_e, HG_DIM), F32)],
        compiler_params=pltpu.CompilerParams(dimension_semantics=("parallel", "arbitrary"),
                                             vmem_limit_bytes=VMEM_LIMIT_BYTES),
        name="hgrn2",
    )(qa, ka, lf, va, ga, a_mat, hgrn_w)


def _attn_kernel(*refs, seq):
    n_in = 3 * sum(ATT_DILATIONS) + 1
    in_refs = refs[:n_in]
    ag_ref = in_refs[-1]
    out_ref = refs[n_in]
    o_scr = refs[n_in + 1:n_in + 1 + N_GROUPS]
    lse_scr = refs[n_in + 1 + N_GROUPS:n_in + 1 + 2 * N_GROUPS]
    n = ATT_STEPS

    lane = lax.broadcasted_iota(jnp.int32, (n, LANES), 1)
    head_a = lane < ATT_HEAD_DIM
    qi = lax.broadcasted_iota(jnp.int32, (n, 2 * n), 0)
    kj = lax.broadcasted_iota(jnp.int32, (n, 2 * n), 1)
    band_ok = (kj >= qi) & (kj <= qi + n)
    first_ok = (lax.broadcasted_iota(jnp.int32, (n, n), 1)
                <= lax.broadcasted_iota(jnp.int32, (n, n), 0))

    def scores(q_ref, k_ref, c, pl_):
        q = q_ref[c * n:(c + 1) * n, pl_]
        kk = k_ref[max(c - 1, 0) * n:(c + 1) * n, pl_]
        zero_q = jnp.zeros_like(q)
        return [_dot_nt(jnp.where(head_a, q, zero_q), kk), _dot_nt(jnp.where(head_a, zero_q, q), kk)]

    def finish(g, d, r, c, p, v_ref, s_pair):
        ok = first_ok if c == 0 else band_ok
        vv = v_ref[max(c - 1, 0) * n:(c + 1) * n, p * LANES:(p + 1) * LANES]
        ps, ms = [], []
        for s in s_pair:
            s = jnp.where(ok, s, NEG_BIG)
            sm = s if c == 0 else jnp.maximum(s[:, :n], s[:, n:])
            m = jnp.max(sm, axis=-1, keepdims=True)
            ps.append(jnp.exp2(s - m).astype(BF16))
            ms.append(m)
        ol = _dot(jnp.concatenate(ps, axis=0), jnp.concatenate([vv, jnp.ones_like(vv)], axis=1))
        l = jnp.where(head_a, ol[:n, LANES:], ol[n:, LANES:])
        o = jnp.where(head_a, ol[:n, :LANES], ol[n:, :LANES]) * (1.0 / l)
        lse = jnp.where(head_a, ms[0], ms[1]) + jnp.log2(l)
        dst = pl.ds(c * n, n) if d == 1 else pl.ds(c * (n * d) + r, n, stride=d)
        o_scr[g][p, dst, :] = o
        lse_scr[g][p, dst, :] = lse

    pending = []
    for p in range(ATT_PAIRS):
        pos = 0
        for g, d in enumerate(ATT_DILATIONS):
            q_refs = in_refs[pos:pos + d]
            k_refs = in_refs[pos + d:pos + 2 * d]
            v_refs = in_refs[pos + 2 * d:pos + 3 * d]
            pos += 3 * d
            for r in range(d):
                for c in range(seq // d // n):
                    s_pair = scores(q_refs[r], k_refs[r], c, slice(p * LANES, (p + 1) * LANES))
                    pending.append((g, d, r, c, p, v_refs[r], s_pair))
                    if len(pending) > ATT_SKEW:
                        finish(*pending.pop(0))
    while pending:
        finish(*pending.pop(0))

    rows_per = 256

    def merge(i, carry):
        rows = pl.ds(pl.multiple_of(i * rows_per, rows_per), rows_per)
        for p in range(ATT_PAIRS):
            lanes = slice(p * LANES, (p + 1) * LANES)
            l1, l2, l3 = lse_scr[0][p, rows, :], lse_scr[1][p, rows, :], lse_scr[2][p, rows, :]
            mx = jnp.maximum(jnp.maximum(l1, l2), l3)
            w1, w2, w3 = jnp.exp2(l1 - mx), jnp.exp2(l2 - mx), jnp.exp2(l3 - mx)
            num = w1 * o_scr[0][p, rows, :] + w2 * o_scr[1][p, rows, :] + w3 * o_scr[2][p, rows, :]
            ob = num / (w1 + w2 + w3) * ag_ref[rows, lanes].astype(F32)
            out_ref[rows, lanes] = ob.astype(BF16)
        return carry

    lax.fori_loop(0, seq // rows_per, merge, 0)


def _attention(qkv, ag, B, S):
    width = ATT_PAIRS * LANES
    steps = ATT_DIM // width
    operands, in_specs = [], []
    for g, d in enumerate(ATT_DILATIONS):
        m_len = S // d
        for arr in qkv[g]:
            view = arr.reshape(B, m_len, d * ATT_DIM)
            for r in range(d):
                operands.append(view)
                in_specs.append(pl.BlockSpec((None, m_len, width),
                                             functools.partial(lambda b, hp, r: (b, 0, r * steps + hp), r=r)))
    operands.append(ag.reshape(B, S, ATT_DIM))
    nat = pl.BlockSpec((None, S, width), lambda b, hp: (b, 0, hp))
    in_specs.append(nat)
    out = pl.pallas_call(
        functools.partial(_attn_kernel, seq=S),
        grid=(B, steps),
        in_specs=in_specs,
        out_specs=nat,
        out_shape=jax.ShapeDtypeStruct((B, S, ATT_DIM), BF16),
        scratch_shapes=[pltpu.VMEM((ATT_PAIRS, S, LANES), F32)] * (2 * N_GROUPS),
        compiler_params=pltpu.CompilerParams(dimension_semantics=("parallel", "parallel"),
                                             vmem_limit_bytes=VMEM_LIMIT_BYTES),
        name="dilated_attn",
    )(*operands)
    return out.reshape(B * S, ATT_DIM)


def _out_kernel(x_ref, oa_ref, ob_ref, gates_ref, wa32_ref, wb32_ref, wo32_ref, fnw_ref, out_ref,
                wa_ref, wb_ref, wo_ref):
    @pl.when(pl.program_id(0) == 0)
    def _():
        wa_ref[...] = wa32_ref[...].astype(BF16)
        wb_ref[...] = wb32_ref[...].astype(BF16)
        wo_ref[...] = wo32_ref[...].astype(BF16)

    def branches(rows):
        return _dot(oa_ref[rows, :], wa_ref[...]), _dot(ob_ref[rows, :], wb_ref[...])

    def finish(rows, ya, yb):
        g_a = gates_ref[rows, :D_MODEL].astype(F32)
        g_b = gates_ref[rows, D_MODEL:].astype(F32)
        merged = (g_a * ya + g_b * yb).astype(BF16)
        y = x_ref[rows, :] + _dot(merged, wo_ref[...])
        ms = jnp.mean(y * y, axis=-1, keepdims=True)
        out_ref[rows, :] = y * lax.rsqrt(ms + NORM_EPS) * fnw_ref[0:1, :]

    pending = None
    for r in range(x_ref.shape[0] // OUT_SUB):
        rows = slice(r * OUT_SUB, (r + 1) * OUT_SUB)
        nxt = (rows, *branches(rows))
        if pending is not None:
            finish(*pending)
        pending = nxt
    finish(*pending)


def _output(x2, oa, ob, gates, wa, wb, wo, fnw):
    T = x2.shape[0]
    tm = OUT_TM
    row = lambda n: pl.BlockSpec((tm, n), lambda i: (i, 0))
    const = lambda shape: pl.BlockSpec(shape, lambda i: (0, 0))
    return pl.pallas_call(
        _out_kernel,
        grid=(T // tm,),
        in_specs=[row(D_MODEL), row(HG_DIM), row(ATT_DIM), row(2 * D_MODEL),
                  const((HG_DIM, D_MODEL)), const((ATT_DIM, D_MODEL)), const((D_MODEL, D_MODEL)),
                  const((PAR_ROWS, D_MODEL))],
        out_specs=row(D_MODEL),
        out_shape=jax.ShapeDtypeStruct((T, D_MODEL), F32),
        scratch_shapes=[pltpu.VMEM((HG_DIM, D_MODEL), BF16), pltpu.VMEM((ATT_DIM, D_MODEL), BF16),
                        pltpu.VMEM((D_MODEL, D_MODEL), BF16)],
        compiler_params=pltpu.CompilerParams(dimension_semantics=("arbitrary",),
                                             vmem_limit_bytes=VMEM_LIMIT_BYTES),
        name="merge_out",
    )(x2, oa, ob, gates, wa, wb, wo, fnw)


def kernel(x, positions, norm_w, w_in, lb_logits, hgrn_norm_w, w_branch_a, w_branch_b, w_out, final_norm_w):
    B, S, D = x.shape
    assert D == D_MODEL and norm_w.shape[0] == 1 and lb_logits.shape == (2, HG_DIM)
    assert S % (ATT_STEPS * max(ATT_DILATIONS)) == 0 and S % HG_TS == 0 and (B * S) % OUT_TM == 0
    T = B * S
    x2 = x.reshape(T, D)
    pos_b = jnp.repeat(positions.reshape(T // ROPE_PACK, ROPE_PACK), ROPE_FREQS, axis=1)
    inv_freq = ROPE_THETA ** (-jnp.arange(0, ATT_HEAD_DIM, 2, dtype=F32) / ATT_HEAD_DIM)
    invf = jnp.tile(inv_freq, LANES // ROPE_FREQS).reshape(1, LANES)

    outs = _inproj(x2, pos_b, norm_w.reshape(1, D), w_in[0].astype(BF16), lb_logits.astype(F32), invf)
    qa, ka, lf, va, ga = outs[:5]
    qkv = [tuple(outs[5 + 3 * g:8 + 3 * g]) for g in range(N_GROUPS)]
    ag, gates = outs[14], outs[15]

    a_mat = jnp.asarray(_hgrn_sum_matrix(), BF16)
    hgrn_w = jnp.broadcast_to(jnp.tile(hgrn_norm_w.reshape(1, HG_VAL).astype(F32), (1, HG_HEADS)), (PAR_ROWS, HG_DIM))
    oa = _hgrn(qa, ka, lf, va, ga, a_mat, hgrn_w, B, S)
    ob = _attention(qkv, ag, B, S)
    out = _output(x2, oa, ob, gates, w_branch_a[0].astype(F32), w_branch_b[0].astype(F32),
                  w_out[0].astype(F32), jnp.broadcast_to(final_norm_w.reshape(1, D).astype(F32), (PAR_ROWS, D)))
    return out.reshape(B, S, D)
```

```python
import functools

import numpy as np
import jax
import jax.numpy as jnp
from jax import lax
from jax.experimental import pallas as pl
from jax.experimental.pallas import tpu as pltpu

D_MODEL = 1024
HG_HEADS = 8
HG_KEY = 128
HG_VAL = 128
HG_DIM = HG_HEADS * HG_KEY
ATT_DILATIONS = (1, 4, 16)
ATT_STEPS = 128
N_GROUPS = 3
ATT_HEADS = 8
ATT_HEAD_DIM = 64
ATT_DIM = ATT_HEADS * ATT_HEAD_DIM
ROPE_THETA = 10000.0
ROPE_FREQS = ATT_HEAD_DIM // 2
ROPE_PACK = 4
NORM_EPS = 1e-6
LOG2_E = 1.4426950408889634
C_HQ, C_HF, C_HI, C_HG = 0, 1024, 2048, 3072
C_AQ, C_AK, C_AV = 4096, 4096 + 1536, 4096 + 3072
C_AG = 8704
C_GATES = 9216
IN_COLS = 11264

LANES = 128
VMEM_LIMIT_BYTES = 56 * 1024 * 1024

PAR_ROWS = 8

IN_TM = 256
IN_SEG = 512
IN_W_CHUNKS = IN_COLS // IN_SEG
PERM_STRIDE = 4
HG_CHUNK = 64
HG_TS = 1024
HG_LEVEL_SIZES = tuple(HG_CHUNK >> (l + 1) for l in range(HG_CHUNK.bit_length() - 1))
HG_ROW_TILE = 8
HG_VPU_LEVEL = 1
HG_GROUP = 2
HG_SKEW = 1
OUT_TM = 1024
OUT_SUB = 256
ATT_PAIRS = 2
ATT_SKEW = 3
NEG_BIG = -1e30

F32 = jnp.float32
BF16 = jnp.bfloat16


def _dot(a, b):
    return jnp.dot(a, b, preferred_element_type=F32)


def _dot_nt(a, b):
    return lax.dot_general(a, b, (((1,), (1,)), ((), ())), preferred_element_type=F32)


def _dot_tn(a, b):
    return lax.dot_general(a, b, (((0,), (0,)), ((), ())), preferred_element_type=F32)


def _sigmoid(z):
    return 0.5 + 0.5 * jnp.tanh(0.5 * z)


def _inproj_kernel(xn_ref, posn_ref, nw_ref, wchunk_ref, lbl_ref, invf_ref,
                   qa_ref, ka_ref, lf_ref, va_ref, ga_ref,
                   q1_ref, k1_ref, v1_ref, q2_ref, k2_ref, v2_ref, q3_ref, k3_ref, v3_ref,
                   ag_ref, gates_ref, perm_scr, h_scr, rope_scr, w_scr):
    tm = xn_ref.shape[0]
    i = pl.program_id(0)
    step = i - IN_W_CHUNKS
    slot = step & 1

    @pl.when(step < 0)
    def _():
        w_scr[i] = wchunk_ref[...].astype(BF16)
    lane = lax.broadcasted_iota(jnp.int32, (tm, LANES), 1)
    first = (lane & (ATT_HEAD_DIM - 1)) < (ATT_HEAD_DIM // 2)

    def prepare(x_ref, pos_ref, dst):
        x = x_ref[...]
        ms = jnp.mean(x * x, axis=-1, keepdims=True)
        h_scr[dst] = (x * lax.rsqrt(ms + NORM_EPS) * nw_ref[...]).astype(BF16)

        ang = pos_ref[...].astype(F32) * invf_ref[...]
        lane_c = lax.broadcasted_iota(jnp.int32, ang.shape, 1)
        first_c = (lane_c & (ATT_HEAD_DIM - 1)) < (ATT_HEAD_DIM // 2)
        q_factor = ATT_HEAD_DIM ** -0.5 * LOG2_E
        for table, compact in ((0, jnp.cos(ang)), (1, jnp.sin(ang))):
            for g in range(ROPE_PACK):
                u = compact if g == 0 else pltpu.roll(compact, LANES - g * ROPE_FREQS, 1)
                u = jnp.where(lane_c < ROPE_FREQS, u, pltpu.roll(u, ROPE_FREQS, 1))
                u = jnp.where(lane_c < 2 * ROPE_FREQS, u, pltpu.roll(u, 2 * ROPE_FREQS, 1))
                if table == 1:
                    u = jnp.where(first_c, -u, u)
                rows = pl.ds(g, tm // ROPE_PACK, stride=ROPE_PACK)
                rope_scr[dst, table, rows, :] = u
                rope_scr[dst, table + 2, rows, :] = u * q_factor

    @pl.when(step == -1)
    def _():
        prepare(xn_ref, posn_ref, 0)

    def proj(c0, n):
        assert n == IN_SEG and c0 % IN_SEG == 0
        return _dot(h_scr[slot], w_scr[c0 // IN_SEG])

    lbl = lbl_ref[...]
    lmax = jnp.max(lbl, axis=0, keepdims=True)
    lexp = jnp.exp(lbl - lmax)
    lb = lexp[0:1, :] / jnp.sum(lexp, axis=0, keepdims=True)

    def rope(zs, tables):
        rot = jnp.where(first, pltpu.roll(zs, LANES - ATT_HEAD_DIM // 2, 1), pltpu.roll(zs, ATT_HEAD_DIM // 2, 1))
        return zs * rope_scr[slot, tables] + rot * rope_scr[slot, tables + 1]

    def silu_to(ref, sl):
        def epi(z):
            hz = 0.5 * z
            ref[:, sl] = (hz + hz * jnp.tanh(hz)).astype(BF16)
        return epi

    def cast_to(ref, sl):
        def epi(z):
            ref[:, sl] = z.astype(BF16)
        return epi

    def sigmoid_to(ref, sl):
        def epi(z):
            ref[:, sl] = _sigmoid(z).astype(BF16)
        return epi

    def forget_to(sl):
        def epi(z):
            c = 0.5 - 0.5 * lb[:, sl]
            ct = c * jnp.tanh(0.5 * z)
            lf_ref[:, sl] = jnp.log2((lb[:, sl] + c) + ct)
            ka_ref[:, sl] = (c - ct).astype(BF16)
        return epi

    def attn_to(ref, d, slab, tables):
        def epi(z):
            for s in range(ATT_DIM // LANES):
                ls = slice(s * LANES, (s + 1) * LANES)
                val = z[:, ls]
                if tables is not None:
                    val = rope(val, tables)
                if d == 1:
                    ref[:, ls] = val.astype(BF16)
                    continue
                perm_scr[0, slab] = val
                if d > PERM_STRIDE:
                    for r1 in range(PERM_STRIDE):
                        perm_scr[1, slab, r1 * (tm // PERM_STRIDE):(r1 + 1) * (tm // PERM_STRIDE), :] = (
                            perm_scr[0, slab, pl.ds(r1, tm // PERM_STRIDE, stride=PERM_STRIDE), :])
                for r in range(d):
                    cols = slice(r * ATT_DIM + s * LANES, r * ATT_DIM + (s + 1) * LANES)
                    if d > PERM_STRIDE:
                        r1, r2 = r % PERM_STRIDE, r // PERM_STRIDE
                        rows = pl.ds(r1 * (tm // PERM_STRIDE) + r2, tm // d, stride=PERM_STRIDE)
                        ref[:, cols] = perm_scr[1, slab, rows, :].astype(BF16)
                    else:
                        ref[:, cols] = perm_scr[0, slab, pl.ds(r, tm // d, stride=d), :].astype(BF16)
        return epi

    halves = [slice(j * IN_SEG, (j + 1) * IN_SEG) for j in range(HG_DIM // IN_SEG)]
    segments = []
    for j, sl in enumerate(halves):
        segments += [(C_HQ + j * IN_SEG, IN_SEG, silu_to(qa_ref, sl)), (C_HF + j * IN_SEG, IN_SEG, forget_to(sl)),
                     (C_HG + j * IN_SEG, IN_SEG, silu_to(ga_ref, sl))]
    for j in range(2 * D_MODEL // IN_SEG):
        segments.append((C_GATES + j * IN_SEG, IN_SEG,
                         sigmoid_to(gates_ref, slice(j * IN_SEG, (j + 1) * IN_SEG))))
    segments.append((C_AG, ATT_DIM, silu_to(ag_ref, slice(0, ATT_DIM))))
    att_refs = ((q1_ref, k1_ref, v1_ref), (q2_ref, k2_ref, v2_ref), (q3_ref, k3_ref, v3_ref))
    for g in reversed(range(N_GROUPS)):
        d = ATT_DILATIONS[g]
        segments += [(C_AQ + g * ATT_DIM, ATT_DIM, attn_to(att_refs[g][0], d, 0, 2)),
                     (C_AK + g * ATT_DIM, ATT_DIM, attn_to(att_refs[g][1], d, 1, 0)),
                     (C_AV + g * ATT_DIM, ATT_DIM, attn_to(att_refs[g][2], d, 2, None))]
    for j, sl in enumerate(halves):
        segments.append((C_HI + j * IN_SEG, IN_SEG, cast_to(va_ref, sl)))

    @pl.when(step >= 0)
    def _():
        z = proj(*segments[0][:2])
        prepare(xn_ref, posn_ref, 1 - slot)
        for j, (_, _, epilogue) in enumerate(segments):
            z_next = proj(*segments[j + 1][:2]) if j + 1 < len(segments) else None
            epilogue(z)
            z = z_next


def _inproj(x2, pos_b, norm_w, w_in, lb_logits, invf):
    T = x2.shape[0]
    tm = IN_TM
    steps = T // tm
    tile = lambda i: jnp.clip(i - IN_W_CHUNKS, 0, steps - 1)
    row = lambda n: pl.BlockSpec((tm, n), lambda i: (tile(i), 0))
    const = lambda shape: pl.BlockSpec(shape, lambda i: (0, 0))
    nxt = lambda r, n: pl.BlockSpec((r, n), lambda i: (tile(i + 1), 0))
    out_shapes = ([jax.ShapeDtypeStruct((T, HG_DIM), BF16),
                   jax.ShapeDtypeStruct((T, HG_DIM), BF16),
                   jax.ShapeDtypeStruct((T, HG_DIM), F32),
                   jax.ShapeDtypeStruct((T, HG_DIM), BF16),
                   jax.ShapeDtypeStruct((T, HG_DIM), BF16)]
                  + [jax.ShapeDtypeStruct((T // d, d * ATT_DIM), BF16) for d in ATT_DILATIONS for _ in range(3)]
                  + [jax.ShapeDtypeStruct((T, ATT_DIM), BF16),
                     jax.ShapeDtypeStruct((T, 2 * D_MODEL), BF16)])
    out_specs = ([row(HG_DIM)] * 5
                 + [pl.BlockSpec((tm // d, d * ATT_DIM), lambda i: (tile(i), 0))
                    for d in ATT_DILATIONS for _ in range(3)]
                 + [row(ATT_DIM), row(2 * D_MODEL)])
    return pl.pallas_call(
        _inproj_kernel,
        grid=(IN_W_CHUNKS + steps,),
        in_specs=[nxt(tm, D_MODEL), nxt(tm // ROPE_PACK, LANES), const((1, D_MODEL)),
                  pl.BlockSpec((D_MODEL, IN_SEG), lambda i: (0, jnp.minimum(i, IN_W_CHUNKS - 1))),
                  const((2, HG_DIM)), const((1, LANES))],
        out_specs=out_specs,
        out_shape=out_shapes,
        scratch_shapes=[pltpu.VMEM((2, 3, tm, LANES), F32), pltpu.VMEM((2, tm, D_MODEL), BF16),
                        pltpu.VMEM((2, 4, tm, LANES), F32), pltpu.VMEM((IN_W_CHUNKS, D_MODEL, IN_SEG), BF16)],
        compiler_params=pltpu.CompilerParams(dimension_semantics=("arbitrary",),
                                             vmem_limit_bytes=VMEM_LIMIT_BYTES),
        name="inproj",
    )(x2, pos_b, norm_w, w_in, lb_logits, invf)


def _hgrn_sum_matrix():
    C = HG_CHUNK
    fine = [m for m in HG_LEVEL_SIZES if HG_VPU_LEVEL < m < HG_ROW_TILE]
    A = np.zeros(((1 + len(fine)) * C, C), np.float32)
    for t in range(C):
        A[t, :t + 1] = 1.0
    for i, m in enumerate(fine):
        for t in range(C):
            start = (t // m) * m
            if t & m:
                A[(1 + i) * C + t, start:t + 1] = 1.0
            else:
                A[(1 + i) * C + t, t + 1:start + m] = 1.0
    return np.concatenate([A, A], axis=1)


def _hgrn_kernel(q_ref, k_ref, lf_ref, v_ref, g_ref, a_ref, nw_ref, o_ref, state_scr, e_scr):
    C = HG_CHUNK
    n_lev = len(HG_LEVEL_SIZES)
    n_chunks = HG_TS // C

    @pl.when(pl.program_id(1) == 0)
    def _():
        state_scr[...] = jnp.zeros_like(state_scr)

    t_idx = lax.broadcasted_iota(jnp.int32, (C, HG_GROUP * C), 0)
    s_idx = lax.broadcasted_iota(jnp.int32, (C, HG_GROUP * C), 1) & (C - 1)
    diag_mask = t_idx == s_idx
    level_masks = []
    for m in HG_LEVEL_SIZES:
        sh = (2 * m).bit_length() - 1
        level_masks.append(((t_idx & m) != 0) & ((s_idx & m) == 0) & ((t_idx >> sh) == (s_idx >> sh)))
    row_full = lax.broadcasted_iota(jnp.int32, (C, HG_DIM), 0)
    row_idx = lax.broadcasted_iota(jnp.int32, (C, HG_KEY), 0)
    right_rows = [(row_idx & m) != 0 for m in HG_LEVEL_SIZES]

    def exponents(ci):
        slot = ci % 2
        lf = lf_ref[ci * C:(ci + 1) * C, :]
        hi = lf.astype(BF16)
        lo = (lf - hi.astype(F32)).astype(BF16)
        sums = _dot(a_ref[...], jnp.concatenate([hi, lo], axis=0))
        b = sums[0:C, :]

        def put(i, arg):
            e_scr[slot, i * C:(i + 1) * C, :] = jnp.exp2(arg)

        i_fine = 1
        for i, m in enumerate(HG_LEVEL_SIZES):
            if m >= HG_ROW_TILE:
                parts = []
                for j in range(C // (2 * m)):
                    ref = jnp.broadcast_to(b[j * 2 * m + m - 1:j * 2 * m + m, :], (m, HG_DIM))
                    parts += [ref - b[j * 2 * m:j * 2 * m + m, :], b[j * 2 * m + m:(j + 1) * 2 * m, :] - ref]
                put(i, jnp.concatenate(parts, axis=0))
            elif m > HG_VPU_LEVEL:
                put(i, sums[i_fine * C:(i_fine + 1) * C, :])
                i_fine += 1
            elif m == 2:
                nxt = pltpu.roll(lf, C - 1, 0)
                prv = pltpu.roll(lf, 1, 0)
                pos4 = row_full & 3
                put(i, jnp.where(pos4 == 0, nxt, jnp.where(pos4 == 1, 0.0, jnp.where(pos4 == 2, lf, lf + prv))))
            else:
                put(i, jnp.where((row_full & 1) == 1, lf, 0.0))
        put(n_lev, b)
        put(n_lev + 1, b[C - 1:C, :] - b)

    def block_diag(xs):
        zero = jnp.zeros_like(xs[0])
        return jnp.concatenate([jnp.concatenate([x if b == a else zero for b in range(len(xs))], axis=1)
                                for a, x in enumerate(xs)], axis=0)

    def pair_scores(ci, hp):
        slot = ci % 2
        rows = slice(ci * C, (ci + 1) * C)
        lanes = [slice(h * HG_KEY, (h + 1) * HG_KEY) for h in range(HG_GROUP * hp, HG_GROUP * (hp + 1))]
        q_bf =[q_ref[rows, hl] for hl in lanes]
        k_bf = [k_ref[rows, hl] for hl in lanes]
        q = [x.astype(F32) for x in q_bf]
        k = [x.astype(F32) for x in k_bf]
        scores = jnp.where(diag_mask, _dot_nt(jnp.concatenate(q_bf, axis=1), block_diag(k_bf)), 0.0)
        for i, m in enumerate(HG_LEVEL_SIZES):
            zs = []
            for a, hl in enumerate(lanes):
                if m >= HG_ROW_TILE:
                    w = jnp.concatenate([(q[a] if j % 2 else k[a])[j * m:(j + 1) * m, :] for j in range(C // m)],
                                        axis=0)
                else:
                    w = jnp.where(right_rows[i], q[a], k[a])
                zs.append((w * e_scr[slot, i * C:(i + 1) * C, hl]).astype(BF16))
            scores = jnp.where(level_masks[i], _dot_nt(jnp.concatenate(zs, axis=1), block_diag(zs)), scores)
        q_dec = [(q[a] * e_scr[slot, n_lev * C:(n_lev + 1) * C, hl]).astype(BF16) for a, hl in enumerate(lanes)]
        k_dec = [(k[a] * e_scr[slot, (n_lev + 1) * C:(n_lev + 2) * C, hl]).astype(BF16) for a, hl in enumerate(lanes)]
        return scores, q_dec, k_dec

    def finish(ci, hp, stage1):
        scores, q_dec, k_dec = stage1
        slot = ci % 2
        rows = slice(ci * C, (ci + 1) * C)
        heads = tuple(range(HG_GROUP * hp, HG_GROUP * (hp + 1)))
        lanes = [slice(h * HG_KEY, (h + 1) * HG_KEY) for h in heads]
        v_bf = [v_ref[rows, hl] for hl in lanes]
        o_pair = _dot(scores.astype(BF16), block_diag(v_bf))
        for a, (h, hl) in enumerate(zip(heads, lanes)):
            e_last = e_scr[slot, (n_lev + 1) * C - 1:(n_lev + 1) * C, hl]
            st = state_scr[h]
            o = o_pair[:, a * HG_VAL:(a + 1) * HG_VAL] + _dot_nt(q_dec[a], st.astype(BF16))
            state_scr[h] = st * e_last + _dot_tn(v_bf[a], k_dec[a])
            ms = jnp.mean(o * o, axis=-1, keepdims=True)
            y = o * lax.rsqrt(ms + NORM_EPS) * nw_ref[0:1, 0:HG_VAL] * g_ref[rows, hl].astype(F32)
            o_ref[rows, hl] = y.astype(BF16)

    n_pairs = HG_HEADS // HG_GROUP
    exponents(0)
    pending = []
    for ci in range(n_chunks):
        for hp in range(n_pairs):
            if hp == n_pairs // 2 and ci + 1 < n_chunks:
                exponents(ci + 1)
            pending.append((ci, hp, pair_scores(ci, hp)))
            if len(pending) > HG_SKEW:
                finish(*pending.pop(0))
    while pending:
        finish(*pending.pop(0))


def _hgrn(qa, ka, lf, va, ga, a_mat, hgrn_w, B, S):
    ts = HG_TS
    steps = S // ts
    blk = pl.BlockSpec((ts, HG_DIM), lambda b, i: (b * steps + i, 0))
    rows_e = (len(HG_LEVEL_SIZES) + 2) * HG_CHUNK
    return pl.pallas_call(
        _hgrn_kernel,
        grid=(B, steps),
        in_specs=[blk, blk, blk, blk, blk,
                  pl.BlockSpec(a_mat.shape, lambda b, i: (0, 0)),
                  pl.BlockSpec((PAR_ROWS, HG_DIM), lambda b, i: (0, 0))],
        out_specs=blk,
        out_shape=jax.ShapeDtypeStruct((B * S, HG_DIM), BF16),
        scratch_shapes=[pltpu.VMEM((HG_HEADS, HG_VAL, HG_KEY), F32),
                        pltpu.VMEM((2, rows_e, HG_DIM), F32)],
        compiler_params=pltpu.CompilerParams(dimension_semantics=("parallel", "arbitrary"),
                                             vmem_limit_bytes=VMEM_LIMIT_BYTES),
        name="hgrn2",
    )(qa, ka, lf, va, ga, a_mat, hgrn_w)


def _attn_kernel(*refs, seq):
    n_in = 3 * sum(ATT_DILATIONS) + 1
    in_refs = refs[:n_in]
    ag_ref = in_refs[-1]
    out_ref = refs[n_in]
    o_scr = refs[n_in + 1:n_in + 1 + N_GROUPS]
    lse_scr = refs[n_in + 1 + N_GROUPS:n_in + 1 + 2 * N_GROUPS]
    n = ATT_STEPS

    lane = lax.broadcasted_iota(jnp.int32, (n, LANES), 1)
    head_a = lane < ATT_HEAD_DIM
    qi = lax.broadcasted_iota(jnp.int32, (n, 2 * n), 0)
    kj = lax.broadcasted_iota(jnp.int32, (n, 2 * n), 1)
    band_ok = (kj >= qi) & (kj <= qi + n)
    first_ok = (lax.broadcasted_iota(jnp.int32, (n, n), 1)
                <= lax.broadcasted_iota(jnp.int32, (n, n), 0))

    def scores(q_ref, k_ref, c, pl_):
        q = q_ref[c * n:(c + 1) * n, pl_]
        kk = k_ref[max(c - 1, 0) * n:(c + 1) * n, pl_]
        zero_q = jnp.zeros_like(q)
        return [_dot_nt(jnp.where(head_a, q, zero_q), kk), _dot_nt(jnp.where(head_a, zero_q, q), kk)]

    def finish(g, d, r, c, p, v_ref, s_pair):
        ok = first_ok if c == 0 else band_ok
        vv = v_ref[max(c - 1, 0) * n:(c + 1) * n, p * LANES:(p + 1) * LANES]
        ps, ms = [], []
        for s in s_pair:
            s = jnp.where(ok, s, NEG_BIG)
            sm = s if c == 0 else jnp.maximum(s[:, :n], s[:, n:])
            m = jnp.max(sm, axis=-1, keepdims=True)
            ps.append(jnp.exp2(s - m).astype(BF16))
            ms.append(m)
        ol = _dot(jnp.concatenate(ps, axis=0), jnp.concatenate([vv, jnp.ones_like(vv)], axis=1))
        l = jnp.where(head_a, ol[:n, LANES:], ol[n:, LANES:])
        o = jnp.where(head_a, ol[:n, :LANES], ol[n:, :LANES]) * (1.0 / l)
        lse = jnp.where(head_a, ms[0], ms[1]) + jnp.log2(l)
        dst = pl.ds(c * n, n) if d == 1 else pl.ds(c * (n * d) + r, n, stride=d)
        o_scr[g][p, dst, :] = o
        lse_scr[g][p, dst, :] = lse

    pending = []
    for p in range(ATT_PAIRS):
        pos = 0
        for g, d in enumerate(ATT_DILATIONS):
            q_refs = in_refs[pos:pos + d]
            k_refs = in_refs[pos + d:pos + 2 * d]
            v_refs = in_refs[pos + 2 * d:pos + 3 * d]
            pos += 3 * d
            for r in range(d):
                for c in range(seq // d // n):
                    s_pair = scores(q_refs[r], k_refs[r], c, slice(p * LANES, (p + 1) * LANES))
                    pending.append((g, d, r, c, p, v_refs[r], s_pair))
                    if len(pending) > ATT_SKEW:
                        finish(*pending.pop(0))
    while pending:
        finish(*pending.pop(0))

    rows_per = 256

    def merge(i, carry):
        rows = pl.ds(pl.multiple_of(i * rows_per, rows_per), rows_per)
        for p in range(ATT_PAIRS):
            lanes = slice(p * LANES, (p + 1) * LANES)
            l1, l2, l3 = lse_scr[0][p, rows, :], lse_scr[1][p, rows, :], lse_scr[2][p, rows, :]
            mx = jnp.maximum(jnp.maximum(l1, l2), l3)
            w1, w2, w3 = jnp.exp2(l1 - mx), jnp.exp2(l2 - mx), jnp.exp2(l3 - mx)
            num = w1 * o_scr[0][p, rows, :] + w2 * o_scr[1][p, rows, :] + w3 * o_scr[2][p, rows, :]
            ob = num / (w1 + w2 + w3) * ag_ref[rows, lanes].astype(F32)
            out_ref[rows, lanes] = ob.astype(BF16)
        return carry

    lax.fori_loop(0, seq // rows_per, merge, 0)


def _attention(qkv, ag, B, S):
    width = ATT_PAIRS * LANES
    steps = ATT_DIM // width
    operands, in_specs = [], []
    for g, d in enumerate(ATT_DILATIONS):
        m_len = S // d
        for arr in qkv[g]:
            view = arr.reshape(B, m_len, d * ATT_DIM)
            for r in range(d):
                operands.append(view)
                in_specs.append(pl.BlockSpec((None, m_len, width),
                                             functools.partial(lambda b, hp, r: (b, 0, r * steps + hp), r=r)))
    operands.append(ag.reshape(B, S, ATT_DIM))
    nat = pl.BlockSpec((None, S, width), lambda b, hp: (b, 0, hp))
    in_specs.append(nat)
    out = pl.pallas_call(
        functools.partial(_attn_kernel, seq=S),
        grid=(B, steps),
        in_specs=in_specs,
        out_specs=nat,
        out_shape=jax.ShapeDtypeStruct((B, S, ATT_DIM), BF16),
        scratch_shapes=[pltpu.VMEM((ATT_PAIRS, S, LANES), F32)] * (2 * N_GROUPS),
        compiler_params=pltpu.CompilerParams(dimension_semantics=("parallel", "parallel"),
                                             vmem_limit_bytes=VMEM_LIMIT_BYTES),
        name="dilated_attn",
    )(*operands)
    return out.reshape(B * S, ATT_DIM)


def _out_kernel(x_ref, oa_ref, ob_ref, gates_ref, wa32_ref, wb32_ref, wo32_ref, fnw_ref, out_ref,
                wa_ref, wb_ref, wo_ref):
    @pl.when(pl.program_id(0) == 0)
    def _():
        wa_ref[...] = wa32_ref[...].astype(BF16)
        wb_ref[...] = wb32_ref[...].astype(BF16)
        wo_ref[...] = wo32_ref[...].astype(BF16)

    def branches(rows):
        return _dot(oa_ref[rows, :], wa_ref[...]), _dot(ob_ref[rows, :], wb_ref[...])

    def finish(rows, ya, yb):
        g_a = gates_ref[rows, :D_MODEL].astype(F32)
        g_b = gates_ref[rows, D_MODEL:].astype(F32)
        merged = (g_a * ya + g_b * yb).astype(BF16)
        y = x_ref[rows, :] + _dot(merged, wo_ref[...])
        ms = jnp.mean(y * y, axis=-1, keepdims=True)
        out_ref[rows, :] = y * lax.rsqrt(ms + NORM_EPS) * fnw_ref[0:1, :]

    pending = None
    for r in range(x_ref.shape[0] // OUT_SUB):
        rows = slice(r * OUT_SUB, (r + 1) * OUT_SUB)
        nxt = (rows, *branches(rows))
        if pending is not None:
            finish(*pending)
        pending = nxt
    finish(*pending)


def _output(x2, oa, ob, gates, wa, wb, wo, fnw):
    T = x2.shape[0]
    tm = OUT_TM
    row = lambda n: pl.BlockSpec((tm, n), lambda i: (i, 0))
    const = lambda shape: pl.BlockSpec(shape, lambda i: (0, 0))
    return pl.pallas_call(
        _out_kernel,
        grid=(T // tm,),
        in_specs=[row(D_MODEL), row(HG_DIM), row(ATT_DIM), row(2 * D_MODEL),
                  const((HG_DIM, D_MODEL)), const((ATT_DIM, D_MODEL)), const((D_MODEL, D_MODEL)),
                  const((PAR_ROWS, D_MODEL))],
        out_specs=row(D_MODEL),
        out_shape=jax.ShapeDtypeStruct((T, D_MODEL), F32),
        scratch_shapes=[pltpu.VMEM((HG_DIM, D_MODEL), BF16), pltpu.VMEM((ATT_DIM, D_MODEL), BF16),
                        pltpu.VMEM((D_MODEL, D_MODEL), BF16)],
        compiler_params=pltpu.CompilerParams(dimension_semantics=("arbitrary",),
                                             vmem_limit_bytes=VMEM_LIMIT_BYTES),
        name="merge_out",
    )(x2, oa, ob, gates, wa, wb, wo, fnw)


def kernel(x, positions, norm_w, w_in, lb_logits, hgrn_norm_w, w_branch_a, w_branch_b, w_out, final_norm_w):
    B, S, D = x.shape
    assert D == D_MODEL and norm_w.shape[0] == 1 and lb_logits.shape == (2, HG_DIM)
    assert S % (ATT_STEPS * max(ATT_DILATIONS)) == 0 and S % HG_TS == 0 and (B * S) % OUT_TM == 0
    T = B * S
    x2 = x.reshape(T, D)
    pos_b = jnp.repeat(positions.reshape(T // ROPE_PACK, ROPE_PACK), ROPE_FREQS, axis=1)
    inv_freq = ROPE_THETA ** (-jnp.arange(0, ATT_HEAD_DIM, 2, dtype=F32) / ATT_HEAD_DIM)
    invf = jnp.tile(inv_freq, LANES // ROPE_FREQS).reshape(1, LANES)

    outs = _inproj(x2, pos_b, norm_w.reshape(1, D), w_in[0].astype(F32), lb_logits.astype(F32), invf)
    qa, ka, lf, va, ga = outs[:5]
    qkv = [tuple(outs[5 + 3 * g:8 + 3 * g]) for g in range(N_GROUPS)]
    ag, gates = outs[14], outs[15]

    a_mat = jnp.asarray(_hgrn_sum_matrix(), BF16)
    hgrn_w = jnp.broadcast_to(jnp.tile(hgrn_norm_w.reshape(1, HG_VAL).astype(F32), (1, HG_HEADS)), (PAR_ROWS, HG_DIM))
    oa = _hgrn(qa, ka, lf, va, ga, a_mat, hgrn_w, B, S)
    ob = _attention(qkv, ag, B, S)
    out = _output(x2, oa, ob, gates, w_branch_a[0].astype(F32), w_branch_b[0].astype(F32),
                  w_out[0].astype(F32), jnp.broadcast_to(final_norm_w.reshape(1, D).astype(F32), (PAR_ROWS, D)))
    return out.reshape(B, S, D)
```

```python
import functools

import numpy as np
import jax
import jax.numpy as jnp
from jax import lax
from jax.experimental import pallas as pl
from jax.experimental.pallas import tpu as pltpu

D_MODEL = 1024
HG_HEADS = 8
HG_KEY = 128
HG_VAL = 128
HG_DIM = HG_HEADS * HG_KEY
ATT_DILATIONS = (1, 4, 16)
ATT_STEPS = 128
N_GROUPS = 3
ATT_HEADS = 8
ATT_HEAD_DIM = 64
ATT_DIM = ATT_HEADS * ATT_HEAD_DIM
ROPE_THETA = 10000.0
ROPE_FREQS = ATT_HEAD_DIM // 2
ROPE_PACK = 4
NORM_EPS = 1e-6
LOG2_E = 1.4426950408889634
C_HQ, C_HF, C_HI, C_HG = 0, 1024, 2048, 3072
C_AQ, C_AK, C_AV = 4096, 4096 + 1536, 4096 + 3072
C_AG = 8704
C_GATES = 9216
IN_COLS = 11264

LANES = 128
VMEM_LIMIT_BYTES = 56 * 1024 * 1024

PAR_ROWS = 8

IN_TM = 256
IN_SEG = 512
IN_W_CHUNKS = IN_COLS // IN_SEG
IN_W_PER_STEP = 2
IN_W_STEPS = IN_W_CHUNKS // IN_W_PER_STEP
PERM_STRIDE = 4
HG_CHUNK = 64
HG_TS = 1024
HG_LEVEL_SIZES = tuple(HG_CHUNK >> (l + 1) for l in range(HG_CHUNK.bit_length() - 1))
HG_ROW_TILE = 8
HG_VPU_LEVEL = 1
HG_GROUP = 2
HG_SKEW = 1
OUT_TM = 1024
OUT_SUB = 256
ATT_PAIRS = 2
ATT_SKEW = 3
NEG_BIG = -1e30

F32 = jnp.float32
BF16 = jnp.bfloat16


def _dot(a, b):
    return jnp.dot(a, b, preferred_element_type=F32)


def _dot_nt(a, b):
    return lax.dot_general(a, b, (((1,), (1,)), ((), ())), preferred_element_type=F32)


def _dot_tn(a, b):
    return lax.dot_general(a, b, (((0,), (0,)), ((), ())), preferred_element_type=F32)


def _sigmoid(z):
    return 0.5 + 0.5 * jnp.tanh(0.5 * z)


def _inproj_kernel(xn_ref, posn_ref, nw_ref, wchunk_ref, lbl_ref, invf_ref,
                   qa_ref, ka_ref, lf_ref, va_ref, ga_ref,
                   q1_ref, k1_ref, v1_ref, q2_ref, k2_ref, v2_ref, q3_ref, k3_ref, v3_ref,
                   ag_ref, gates_ref, perm_scr, h_scr, rope_scr, w_scr):
    tm = xn_ref.shape[0]
    i = pl.program_id(0)
    step = i - IN_W_STEPS
    slot = step & 1

    @pl.when(step < 0)
    def _():
        for k in range(IN_W_PER_STEP):
            w_scr[i * IN_W_PER_STEP + k] = wchunk_ref[:, k * IN_SEG:(k + 1) * IN_SEG].astype(BF16)
    lane = lax.broadcasted_iota(jnp.int32, (tm, LANES), 1)
    first = (lane & (ATT_HEAD_DIM - 1)) < (ATT_HEAD_DIM // 2)

    def prepare(x_ref, pos_ref, dst):
        x = x_ref[...]
        ms = jnp.mean(x * x, axis=-1, keepdims=True)
        h_scr[dst] = (x * lax.rsqrt(ms + NORM_EPS) * nw_ref[...]).astype(BF16)

        ang = pos_ref[...].astype(F32) * invf_ref[...]
        lane_c = lax.broadcasted_iota(jnp.int32, ang.shape, 1)
        first_c = (lane_c & (ATT_HEAD_DIM - 1)) < (ATT_HEAD_DIM // 2)
        q_factor = ATT_HEAD_DIM ** -0.5 * LOG2_E
        for table, compact in ((0, jnp.cos(ang)), (1, jnp.sin(ang))):
            for g in range(ROPE_PACK):
                u = compact if g == 0 else pltpu.roll(compact, LANES - g * ROPE_FREQS, 1)
                u = jnp.where(lane_c < ROPE_FREQS, u, pltpu.roll(u, ROPE_FREQS, 1))
                u = jnp.where(lane_c < 2 * ROPE_FREQS, u, pltpu.roll(u, 2 * ROPE_FREQS, 1))
                if table == 1:
                    u = jnp.where(first_c, -u, u)
                rows = pl.ds(g, tm // ROPE_PACK, stride=ROPE_PACK)
                rope_scr[dst, table, rows, :] = u
                rope_scr[dst, table + 2, rows, :] = u * q_factor

    @pl.when(step == -1)
    def _():
        prepare(xn_ref, posn_ref, 0)

    def proj(c0, n):
        assert n == IN_SEG and c0 % IN_SEG == 0
        return _dot(h_scr[slot], w_scr[c0 // IN_SEG])

    lbl = lbl_ref[...]
    lmax = jnp.max(lbl, axis=0, keepdims=True)
    lexp = jnp.exp(lbl - lmax)
    lb = lexp[0:1, :] / jnp.sum(lexp, axis=0, keepdims=True)

    def rope(zs, tables):
        rot = jnp.where(first, pltpu.roll(zs, LANES - ATT_HEAD_DIM // 2, 1), pltpu.roll(zs, ATT_HEAD_DIM // 2, 1))
        return zs * rope_scr[slot, tables] + rot * rope_scr[slot, tables + 1]

    def silu_to(ref, sl):
        def epi(z):
            hz = 0.5 * z
            ref[:, sl] = (hz + hz * jnp.tanh(hz)).astype(BF16)
        return epi

    def cast_to(ref, sl):
        def epi(z):
            ref[:, sl] = z.astype(BF16)
        return epi

    def sigmoid_to(ref, sl):
        def epi(z):
            ref[:, sl] = _sigmoid(z).astype(BF16)
        return epi

    def forget_to(sl):
        def epi(z):
            c = 0.5 - 0.5 * lb[:, sl]
            ct = c * jnp.tanh(0.5 * z)
            lf_ref[:, sl] = jnp.log2((lb[:, sl] + c) + ct)
            ka_ref[:, sl] = (c - ct).astype(BF16)
        return epi

    def attn_to(ref, d, slab, tables):
        def epi(z):
            for s in range(ATT_DIM // LANES):
                ls = slice(s * LANES, (s + 1) * LANES)
                val = z[:, ls]
                if tables is not None:
                    val = rope(val, tables)
                if d == 1:
                    ref[:, ls] = val.astype(BF16)
                    continue
                perm_scr[0, slab] = val
                if d > PERM_STRIDE:
                    for r1 in range(PERM_STRIDE):
                        perm_scr[1, slab, r1 * (tm // PERM_STRIDE):(r1 + 1) * (tm // PERM_STRIDE), :] = (
                            perm_scr[0, slab, pl.ds(r1, tm // PERM_STRIDE, stride=PERM_STRIDE), :])
                for r in range(d):
                    cols = slice(r * ATT_DIM + s * LANES, r * ATT_DIM + (s + 1) * LANES)
                    if d > PERM_STRIDE:
                        r1, r2 = r % PERM_STRIDE, r // PERM_STRIDE
                        rows = pl.ds(r1 * (tm // PERM_STRIDE) + r2, tm // d, stride=PERM_STRIDE)
                        ref[:, cols] = perm_scr[1, slab, rows, :].astype(BF16)
                    else:
                        ref[:, cols] = perm_scr[0, slab, pl.ds(r, tm // d, stride=d), :].astype(BF16)
        return epi

    halves = [slice(j * IN_SEG, (j + 1) * IN_SEG) for j in range(HG_DIM // IN_SEG)]
    segments = []
    for j, sl in enumerate(halves):
        segments += [(C_HQ + j * IN_SEG, IN_SEG, silu_to(qa_ref, sl)), (C_HF + j * IN_SEG, IN_SEG, forget_to(sl)),
                     (C_HG + j * IN_SEG, IN_SEG, silu_to(ga_ref, sl))]
    for j in range(2 * D_MODEL // IN_SEG):
        segments.append((C_GATES + j * IN_SEG, IN_SEG,
                         sigmoid_to(gates_ref, slice(j * IN_SEG, (j + 1) * IN_SEG))))
    segments.append((C_AG, ATT_DIM, silu_to(ag_ref, slice(0, ATT_DIM))))
    att_refs = ((q1_ref, k1_ref, v1_ref), (q2_ref, k2_ref, v2_ref), (q3_ref, k3_ref, v3_ref))
    for g in reversed(range(N_GROUPS)):
        d = ATT_DILATIONS[g]
        segments += [(C_AQ + g * ATT_DIM, ATT_DIM, attn_to(att_refs[g][0], d, 0, 2)),
                     (C_AK + g * ATT_DIM, ATT_DIM, attn_to(att_refs[g][1], d, 1, 0)),
                     (C_AV + g * ATT_DIM, ATT_DIM, attn_to(att_refs[g][2], d, 2, None))]
    for j, sl in enumerate(halves):
        segments.append((C_HI + j * IN_SEG, IN_SEG, cast_to(va_ref, sl)))

    @pl.when(step >= 0)
    def _():
        z = proj(*segments[0][:2])
        prepare(xn_ref, posn_ref, 1 - slot)
        for j, (_, _, epilogue) in enumerate(segments):
            z_next = proj(*segments[j + 1][:2]) if j + 1 < len(segments) else None
            epilogue(z)
            z = z_next


def _inproj(x2, pos_b, norm_w, w_in, lb_logits, invf):
    T = x2.shape[0]
    tm = IN_TM
    steps = T // tm
    tile = lambda i: jnp.clip(i - IN_W_STEPS, 0, steps - 1)
    row = lambda n: pl.BlockSpec((tm, n), lambda i: (tile(i), 0))
    const = lambda shape: pl.BlockSpec(shape, lambda i: (0, 0))
    nxt = lambda r, n: pl.BlockSpec((r, n), lambda i: (tile(i + 1), 0))
    out_shapes = ([jax.ShapeDtypeStruct((T, HG_DIM), BF16),
                   jax.ShapeDtypeStruct((T, HG_DIM), BF16),
                   jax.ShapeDtypeStruct((T, HG_DIM), F32),
                   jax.ShapeDtypeStruct((T, HG_DIM), BF16),
                   jax.ShapeDtypeStruct((T, HG_DIM), BF16)]
                  + [jax.ShapeDtypeStruct((T // d, d * ATT_DIM), BF16) for d in ATT_DILATIONS for _ in range(3)]
                  + [jax.ShapeDtypeStruct((T, ATT_DIM), BF16),
                     jax.ShapeDtypeStruct((T, 2 * D_MODEL), BF16)])
    out_specs = ([row(HG_DIM)] * 5
                 + [pl.BlockSpec((tm // d, d * ATT_DIM), lambda i: (tile(i), 0))
                    for d in ATT_DILATIONS for _ in range(3)]
                 + [row(ATT_DIM), row(2 * D_MODEL)])
    return pl.pallas_call(
        _inproj_kernel,
        grid=(IN_W_STEPS + steps,),
        in_specs=[nxt(tm, D_MODEL), nxt(tm // ROPE_PACK, LANES), const((1, D_MODEL)),
                  pl.BlockSpec((D_MODEL, IN_W_PER_STEP * IN_SEG), lambda i: (0, jnp.minimum(i, IN_W_STEPS - 1))),
                  const((2, HG_DIM)), const((1, LANES))],
        out_specs=out_specs,
        out_shape=out_shapes,
        scratch_shapes=[pltpu.VMEM((2, 3, tm, LANES), F32), pltpu.VMEM((2, tm, D_MODEL), BF16),
                        pltpu.VMEM((2, 4, tm, LANES), F32), pltpu.VMEM((IN_W_CHUNKS, D_MODEL, IN_SEG), BF16)],
        compiler_params=pltpu.CompilerParams(dimension_semantics=("arbitrary",),
                                             vmem_limit_bytes=VMEM_LIMIT_BYTES),
        name="inproj",
    )(x2, pos_b, norm_w, w_in, lb_logits, invf)


def _hgrn_sum_matrix():
    C = HG_CHUNK
    fine = [m for m in HG_LEVEL_SIZES if HG_VPU_LEVEL < m < HG_ROW_TILE]
    A = np.zeros(((1 + len(fine)) * C, C), np.float32)
    for t in range(C):
        A[t, :t + 1] = 1.0
    for i, m in enumerate(fine):
        for t in range(C):
            start = (t // m) * m
            if t & m:
                A[(1 + i) * C + t, start:t + 1] = 1.0
            else:
                A[(1 + i) * C + t, t + 1:start + m] = 1.0
    return np.concatenate([A, A], axis=1)


def _hgrn_kernel(q_ref, k_ref, lf_ref, v_ref, g_ref, a_ref, nw_ref, o_ref, state_scr, e_scr):
    C = HG_CHUNK
    n_lev = len(HG_LEVEL_SIZES)
    n_chunks = HG_TS // C

    @pl.when(pl.program_id(1) == 0)
    def _():
        state_scr[...] = jnp.zeros_like(state_scr)

    t_idx = lax.broadcasted_iota(jnp.int32, (C, HG_GROUP * C), 0)
    s_idx = lax.broadcasted_iota(jnp.int32, (C, HG_GROUP * C), 1) & (C - 1)
    diag_mask = t_idx == s_idx
    level_masks = []
    for m in HG_LEVEL_SIZES:
        sh = (2 * m).bit_length() - 1
        level_masks.append(((t_idx & m) != 0) & ((s_idx & m) == 0) & ((t_idx >> sh) == (s_idx >> sh)))
    row_full = lax.broadcasted_iota(jnp.int32, (C, HG_DIM), 0)
    row_idx = lax.broadcasted_iota(jnp.int32, (C, HG_KEY), 0)
    right_rows = [(row_idx & m) != 0 for m in HG_LEVEL_SIZES]

    def exponents(ci):
        slot = ci % 2
        lf = lf_ref[ci * C:(ci + 1) * C, :]
        hi = lf.astype(BF16)
        lo = (lf - hi.astype(F32)).astype(BF16)
        sums = _dot(a_ref[...], jnp.concatenate([hi, lo], axis=0))
        b = sums[0:C, :]

        def put(i, arg):
            e_scr[slot, i * C:(i + 1) * C, :] = jnp.exp2(arg)

        i_fine = 1
        for i, m in enumerate(HG_LEVEL_SIZES):
            if m >= HG_ROW_TILE:
                parts = []
                for j in range(C // (2 * m)):
                    ref = jnp.broadcast_to(b[j * 2 * m + m - 1:j * 2 * m + m, :], (m, HG_DIM))
                    parts += [ref - b[j * 2 * m:j * 2 * m + m, :], b[j * 2 * m + m:(j + 1) * 2 * m, :] - ref]
                put(i, jnp.concatenate(parts, axis=0))
            elif m > HG_VPU_LEVEL:
                put(i, sums[i_fine * C:(i_fine + 1) * C, :])
                i_fine += 1
            elif m == 2:
                nxt = pltpu.roll(lf, C - 1, 0)
                prv = pltpu.roll(lf, 1, 0)
                pos4 = row_full & 3
                put(i, jnp.where(pos4 == 0, nxt, jnp.where(pos4 == 1, 0.0, jnp.where(pos4 == 2, lf, lf + prv))))
            else:
                put(i, jnp.where((row_full & 1) == 1, lf, 0.0))
        put(n_lev, b)
        put(n_lev + 1, b[C - 1:C, :] - b)

    def block_diag(xs):
        zero = jnp.zeros_like(xs[0])
        return jnp.concatenate([jnp.concatenate([x if b == a else zero for b in range(len(xs))], axis=1)
                                for a, x in enumerate(xs)], axis=0)

    def pair_scores(ci, hp):
        slot = ci % 2
        rows = slice(ci * C, (ci + 1) * C)
        lanes = [slice(h * HG_KEY, (h + 1) * HG_KEY) for h in range(HG_GROUP * hp, HG_GROUP * (hp + 1))]
        q_bf =[q_ref[rows, hl] for hl in lanes]
        k_bf = [k_ref[rows, hl] for hl in lanes]
        q = [x.astype(F32) for x in q_bf]
        k = [x.astype(F32) for x in k_bf]
        scores = jnp.where(diag_mask, _dot_nt(jnp.concatenate(q_bf, axis=1), block_diag(k_bf)), 0.0)
        for i, m in enumerate(HG_LEVEL_SIZES):
            zs = []
            for a, hl in enumerate(lanes):
                if m >= HG_ROW_TILE:
                    w = jnp.concatenate([(q[a] if j % 2 else k[a])[j * m:(j + 1) * m, :] for j in range(C // m)],
                                        axis=0)
                else:
                    w = jnp.where(right_rows[i], q[a], k[a])
                zs.append((w * e_scr[slot, i * C:(i + 1) * C, hl]).astype(BF16))
            scores = jnp.where(level_masks[i], _dot_nt(jnp.concatenate(zs, axis=1), block_diag(zs)), scores)
        q_dec = [(q[a] * e_scr[slot, n_lev * C:(n_lev + 1) * C, hl]).astype(BF16) for a, hl in enumerate(lanes)]
        k_dec = [(k[a] * e_scr[slot, (n_lev + 1) * C:(n_lev + 2) * C, hl]).astype(BF16) for a, hl in enumerate(lanes)]
        return scores, q_dec, k_dec

    def finish(ci, hp, stage1):
        scores, q_dec, k_dec = stage1
        slot = ci % 2
        rows = slice(ci * C, (ci + 1) * C)
        heads = tuple(range(HG_GROUP * hp, HG_GROUP * (hp + 1)))
        lanes = [slice(h * HG_KEY, (h + 1) * HG_KEY) for h in heads]
        v_bf = [v_ref[rows, hl] for hl in lanes]
        o_pair = _dot(scores.astype(BF16), block_diag(v_bf))
        for a, (h, hl) in enumerate(zip(heads, lanes)):
            e_last = e_scr[slot, (n_lev + 1) * C - 1:(n_lev + 1) * C, hl]
            st = state_scr[h]
            o = o_pair[:, a * HG_VAL:(a + 1) * HG_VAL] + _dot_nt(q_dec[a], st.astype(BF16))
            state_scr[h] = st * e_last + _dot_tn(v_bf[a], k_dec[a])
            ms = jnp.mean(o * o, axis=-1, keepdims=True)
            y = o * lax.rsqrt(ms + NORM_EPS) * nw_ref[0:1, 0:HG_VAL] * g_ref[rows, hl].astype(F32)
            o_ref[rows, hl] = y.astype(BF16)

    n_pairs = HG_HEADS // HG_GROUP
    exponents(0)
    pending = []
    for ci in range(n_chunks):
        for hp in range(n_pairs):
            if hp == n_pairs // 2 and ci + 1 < n_chunks:
                exponents(ci + 1)
            pending.append((ci, hp, pair_scores(ci, hp)))
            if len(pending) > HG_SKEW:
                finish(*pending.pop(0))
    while pending:
        finish(*pending.pop(0))


def _hgrn(qa, ka, lf, va, ga, a_mat, hgrn_w, B, S):
    ts = HG_TS
    steps = S // ts
    blk = pl.BlockSpec((ts, HG_DIM), lambda b, i: (b * steps + i, 0))
    rows_e = (len(HG_LEVEL_SIZES) + 2) * HG_CHUNK
    return pl.pallas_call(
        _hgrn_kernel,
        grid=(B, steps),
        in_specs=[blk, blk, blk, blk, blk,
                  pl.BlockSpec(a_mat.shape, lambda b, i: (0, 0)),
                  pl.BlockSpec((PAR_ROWS, HG_DIM), lambda b, i: (0, 0))],
        out_specs=blk,
        out_shape=jax.ShapeDtypeStruct((B * S, HG_DIM), BF16),
        scratch_shapes=[pltpu.VMEM((HG_HEADS, HG_VAL, HG_KEY), F32),
                        pltpu.VMEM((2, rows_e, HG_DIM), F32)],
        compiler_params=pltpu.CompilerParams(dimension_semantics=("parallel", "arbitrary"),
                                             vmem_limit_bytes=VMEM_LIMIT_BYTES),
        name="hgrn2",
    )(qa, ka, lf, va, ga, a_mat, hgrn_w)


def _attn_kernel(*refs, seq):
    n_in = 3 * sum(ATT_DILATIONS) + 1
    in_refs = refs[:n_in]
    ag_ref = in_refs[-1]
    out_ref = refs[n_in]
    o_scr = refs[n_in + 1:n_in + 1 + N_GROUPS]
    lse_scr = refs[n_in + 1 + N_GROUPS:n_in + 1 + 2 * N_GROUPS]
    n = ATT_STEPS

    lane = lax.broadcasted_iota(jnp.int32, (n, LANES), 1)
    head_a = lane < ATT_HEAD_DIM
    qi = lax.broadcasted_iota(jnp.int32, (n, 2 * n), 0)
    kj = lax.broadcasted_iota(jnp.int32, (n, 2 * n), 1)
    band_ok = (kj >= qi) & (kj <= qi + n)
    first_ok = (lax.broadcasted_iota(jnp.int32, (n, n), 1)
                <= lax.broadcasted_iota(jnp.int32, (n, n), 0))

    def scores(q_ref, k_ref, c, pl_):
        q = q_ref[c * n:(c + 1) * n, pl_]
        kk = k_ref[max(c - 1, 0) * n:(c + 1) * n, pl_]
        zero_q = jnp.zeros_like(q)
        return [_dot_nt(jnp.where(head_a, q, zero_q), kk), _dot_nt(jnp.where(head_a, zero_q, q), kk)]

    def finish(g, d, r, c, p, v_ref, s_pair):
        ok = first_ok if c == 0 else band_ok
        vv = v_ref[max(c - 1, 0) * n:(c + 1) * n, p * LANES:(p + 1) * LANES]
        ps, ms = [], []
        for s in s_pair:
            s = jnp.where(ok, s, NEG_BIG)
            sm = s if c == 0 else jnp.maximum(s[:, :n], s[:, n:])
            m = jnp.max(sm, axis=-1, keepdims=True)
            ps.append(jnp.exp2(s - m).astype(BF16))
            ms.append(m)
        ol = _dot(jnp.concatenate(ps, axis=0), jnp.concatenate([vv, jnp.ones_like(vv)], axis=1))
        l = jnp.where(head_a, ol[:n, LANES:], ol[n:, LANES:])
        o = jnp.where(head_a, ol[:n, :LANES], ol[n:, :LANES]) * (1.0 / l)
        lse = jnp.where(head_a, ms[0], ms[1]) + jnp.log2(l)
        dst = pl.ds(c * n, n) if d == 1 else pl.ds(c * (n * d) + r, n, stride=d)
        o_scr[g][p, dst, :] = o
        lse_scr[g][p, dst, :] = lse

    pending = []
    for p in range(ATT_PAIRS):
        pos = 0
        for g, d in enumerate(ATT_DILATIONS):
            q_refs = in_refs[pos:pos + d]
            k_refs = in_refs[pos + d:pos + 2 * d]
            v_refs = in_refs[pos + 2 * d:pos + 3 * d]
            pos += 3 * d
            for r in range(d):
                for c in range(seq // d // n):
                    s_pair = scores(q_refs[r], k_refs[r], c, slice(p * LANES, (p + 1) * LANES))
                    pending.append((g, d, r, c, p, v_refs[r], s_pair))
                    if len(pending) > ATT_SKEW:
                        finish(*pending.pop(0))
    while pending:
        finish(*pending.pop(0))

    rows_per = 256

    def merge(i, carry):
        rows = pl.ds(pl.multiple_of(i * rows_per, rows_per), rows_per)
        for p in range(ATT_PAIRS):
            lanes = slice(p * LANES, (p + 1) * LANES)
            l1, l2, l3 = lse_scr[0][p, rows, :], lse_scr[1][p, rows, :], lse_scr[2][p, rows, :]
            mx = jnp.maximum(jnp.maximum(l1, l2), l3)
            w1, w2, w3 = jnp.exp2(l1 - mx), jnp.exp2(l2 - mx), jnp.exp2(l3 - mx)
            num = w1 * o_scr[0][p, rows, :] + w2 * o_scr[1][p, rows, :] + w3 * o_scr[2][p, rows, :]
            ob = num / (w1 + w2 + w3) * ag_ref[rows, lanes].astype(F32)
            out_ref[rows, lanes] = ob.astype(BF16)
        return carry

    lax.fori_loop(0, seq // rows_per, merge, 0)


def _attention(qkv, ag, B, S):
    width = ATT_PAIRS * LANES
    steps = ATT_DIM // width
    operands, in_specs = [], []
    for g, d in enumerate(ATT_DILATIONS):
        m_len = S // d
        for arr in qkv[g]:
            view = arr.reshape(B, m_len, d * ATT_DIM)
            for r in range(d):
                operands.append(view)
                in_specs.append(pl.BlockSpec((None, m_len, width),
                                             functools.partial(lambda b, hp, r: (b, 0, r * steps + hp), r=r)))
    operands.append(ag.reshape(B, S, ATT_DIM))
    nat = pl.BlockSpec((None, S, width), lambda b, hp: (b, 0, hp))
    in_specs.append(nat)
    out = pl.pallas_call(
        functools.partial(_attn_kernel, seq=S),
        grid=(B, steps),
        in_specs=in_specs,
        out_specs=nat,
        out_shape=jax.ShapeDtypeStruct((B, S, ATT_DIM), BF16),
        scratch_shapes=[pltpu.VMEM((ATT_PAIRS, S, LANES), F32)] * (2 * N_GROUPS),
        compiler_params=pltpu.CompilerParams(dimension_semantics=("parallel", "parallel"),
                                             vmem_limit_bytes=VMEM_LIMIT_BYTES),
        name="dilated_attn",
    )(*operands)
    return out.reshape(B * S, ATT_DIM)


def _out_kernel(x_ref, oa_ref, ob_ref, gates_ref, wa32_ref, wb32_ref, wo32_ref, fnw_ref, out_ref,
                wa_ref, wb_ref, wo_ref):
    @pl.when(pl.program_id(0) == 0)
    def _():
        wa_ref[...] = wa32_ref[...].astype(BF16)
        wb_ref[...] = wb32_ref[...].astype(BF16)
        wo_ref[...] = wo32_ref[...].astype(BF16)

    def branches(rows):
        return _dot(oa_ref[rows, :], wa_ref[...]), _dot(ob_ref[rows, :], wb_ref[...])

    def finish(rows, ya, yb):
        g_a = gates_ref[rows, :D_MODEL].astype(F32)
        g_b = gates_ref[rows, D_MODEL:].astype(F32)
        merged = (g_a * ya + g_b * yb).astype(BF16)
        y = x_ref[rows, :] + _dot(merged, wo_ref[...])
        ms = jnp.mean(y * y, axis=-1, keepdims=True)
        out_ref[rows, :] = y * lax.rsqrt(ms + NORM_EPS) * fnw_ref[0:1, :]

    pending = None
    for r in range(x_ref.shape[0] // OUT_SUB):
        rows = slice(r * OUT_SUB, (r + 1) * OUT_SUB)
        nxt = (rows, *branches(rows))
        if pending is not None:
            finish(*pending)
        pending = nxt
    finish(*pending)


def _output(x2, oa, ob, gates, wa, wb, wo, fnw):
    T = x2.shape[0]
    tm = OUT_TM
    row = lambda n: pl.BlockSpec((tm, n), lambda i: (i, 0))
    const = lambda shape: pl.BlockSpec(shape, lambda i: (0, 0))
    return pl.pallas_call(
        _out_kernel,
        grid=(T // tm,),
        in_specs=[row(D_MODEL), row(HG_DIM), row(ATT_DIM), row(2 * D_MODEL),
                  const((HG_DIM, D_MODEL)), const((ATT_DIM, D_MODEL)), const((D_MODEL, D_MODEL)),
                  const((PAR_ROWS, D_MODEL))],
        out_specs=row(D_MODEL),
        out_shape=jax.ShapeDtypeStruct((T, D_MODEL), F32),
        scratch_shapes=[pltpu.VMEM((HG_DIM, D_MODEL), BF16), pltpu.VMEM((ATT_DIM, D_MODEL), BF16),
                        pltpu.VMEM((D_MODEL, D_MODEL), BF16)],
        compiler_params=pltpu.CompilerParams(dimension_semantics=("arbitrary",),
                                             vmem_limit_bytes=VMEM_LIMIT_BYTES),
        name="merge_out",
    )(x2, oa, ob, gates, wa, wb, wo, fnw)


def kernel(x, positions, norm_w, w_in, lb_logits, hgrn_norm_w, w_branch_a, w_branch_b, w_out, final_norm_w):
    B, S, D = x.shape
    assert D == D_MODEL and norm_w.shape[0] == 1 and lb_logits.shape == (2, HG_DIM)
    assert S % (ATT_STEPS * max(ATT_DILATIONS)) == 0 and S % HG_TS == 0 and (B * S) % OUT_TM == 0
    T = B * S
    x2 = x.reshape(T, D)
    pos_b = jnp.repeat(positions.reshape(T // ROPE_PACK, ROPE_PACK), ROPE_FREQS, axis=1)
    inv_freq = ROPE_THETA ** (-jnp.arange(0, ATT_HEAD_DIM, 2, dtype=F32) / ATT_HEAD_DIM)
    invf = jnp.tile(inv_freq, LANES // ROPE_FREQS).reshape(1, LANES)

    outs = _inproj(x2, pos_b, norm_w.reshape(1, D), w_in[0].astype(F32), lb_logits.astype(F32), invf)
    qa, ka, lf, va, ga = outs[:5]
    qkv = [tuple(outs[5 + 3 * g:8 + 3 * g]) for g in range(N_GROUPS)]
    ag, gates = outs[14], outs[15]

    a_mat = jnp.asarray(_hgrn_sum_matrix(), BF16)
    hgrn_w = jnp.broadcast_to(jnp.tile(hgrn_norm_w.reshape(1, HG_VAL).astype(F32), (1, HG_HEADS)), (PAR_ROWS, HG_DIM))
    oa = _hgrn(qa, ka, lf, va, ga, a_mat, hgrn_w, B, S)
    ob = _attention(qkv, ag, B, S)
    out = _output(x2, oa, ob, gates, w_branch_a[0].astype(F32), w_branch_b[0].astype(F32),
                  w_out[0].astype(F32), jnp.broadcast_to(final_norm_w.reshape(1, D).astype(F32), (PAR_ROWS, D)))
    return out.reshape(B, S, D)
```

```python
import functools

import numpy as np
import jax
import jax.numpy as jnp
from jax import lax
from jax.experimental import pallas as pl
from jax.experimental.pallas import tpu as pltpu

D_MODEL = 1024
HG_HEADS = 8
HG_KEY = 128
HG_VAL = 128
HG_DIM = HG_HEADS * HG_KEY
ATT_DILATIONS = (1, 4, 16)
ATT_STEPS = 128
N_GROUPS = 3
ATT_HEADS = 8
ATT_HEAD_DIM = 64
ATT_DIM = ATT_HEADS * ATT_HEAD_DIM
ROPE_THETA = 10000.0
ROPE_FREQS = ATT_HEAD_DIM // 2
ROPE_PACK = 4
NORM_EPS = 1e-6
LOG2_E = 1.4426950408889634
C_HQ, C_HF, C_HI, C_HG = 0, 1024, 2048, 3072
C_AQ, C_AK, C_AV = 4096, 4096 + 1536, 4096 + 3072
C_AG = 8704
C_GATES = 9216
IN_COLS = 11264

LANES = 128
VMEM_LIMIT_BYTES = 56 * 1024 * 1024

PAR_ROWS = 8

IN_TM = 256
IN_SEG = 512
IN_W_CHUNKS = IN_COLS // IN_SEG
IN_W_PER_STEP = 2
IN_W_STEPS = IN_W_CHUNKS // IN_W_PER_STEP
PERM_STRIDE = 4
HG_CHUNK = 64
HG_LEVEL_SIZES = tuple(HG_CHUNK >> (l + 1) for l in range(HG_CHUNK.bit_length() - 1))
HG_ROW_TILE = 8
HG_VPU_LEVEL = 1
HG_GROUP = 2
HG_SKEW = 1
OUT_TM = 1024
OUT_SUB = 256
ATT_PAIRS = 1
ATT_SKEW = 3
NEG_BIG = -1e30

F32 = jnp.float32
BF16 = jnp.bfloat16


def _dot(a, b):
    return jnp.dot(a, b, preferred_element_type=F32)


def _dot_nt(a, b):
    return lax.dot_general(a, b, (((1,), (1,)), ((), ())), preferred_element_type=F32)


def _dot_tn(a, b):
    return lax.dot_general(a, b, (((0,), (0,)), ((), ())), preferred_element_type=F32)


def _sigmoid(z):
    return 0.5 + 0.5 * jnp.tanh(0.5 * z)


def _inproj_kernel(xn_ref, posn_ref, nw_ref, wchunk_ref, lbl_ref, invf_ref,
                   qa_ref, ka_ref, lf_ref, va_ref, ga_ref,
                   q1_ref, k1_ref, v1_ref, q2_ref, k2_ref, v2_ref, q3_ref, k3_ref, v3_ref,
                   ag_ref, gates_ref, perm_scr, h_scr, rope_scr, w_scr):
    tm = xn_ref.shape[0]
    i = pl.program_id(0)
    step = i - IN_W_STEPS
    slot = step & 1

    @pl.when(step < 0)
    def _():
        for k in range(IN_W_PER_STEP):
            w_scr[i * IN_W_PER_STEP + k] = wchunk_ref[:, k * IN_SEG:(k + 1) * IN_SEG].astype(BF16)
    lane = lax.broadcasted_iota(jnp.int32, (tm, LANES), 1)
    first = (lane & (ATT_HEAD_DIM - 1)) < (ATT_HEAD_DIM // 2)

    def prepare(x_ref, pos_ref, dst):
        x = x_ref[...]
        ms = jnp.mean(x * x, axis=-1, keepdims=True)
        h_scr[dst] = (x * lax.rsqrt(ms + NORM_EPS) * nw_ref[...]).astype(BF16)

        ang = pos_ref[...].astype(F32) * invf_ref[...]
        lane_c = lax.broadcasted_iota(jnp.int32, ang.shape, 1)
        first_c = (lane_c & (ATT_HEAD_DIM - 1)) < (ATT_HEAD_DIM // 2)
        q_factor = ATT_HEAD_DIM ** -0.5 * LOG2_E
        for table, compact in ((0, jnp.cos(ang)), (1, jnp.sin(ang))):
            for g in range(ROPE_PACK):
                u = compact if g == 0 else pltpu.roll(compact, LANES - g * ROPE_FREQS, 1)
                u = jnp.where(lane_c < ROPE_FREQS, u, pltpu.roll(u, ROPE_FREQS, 1))
                u = jnp.where(lane_c < 2 * ROPE_FREQS, u, pltpu.roll(u, 2 * ROPE_FREQS, 1))
                if table == 1:
                    u = jnp.where(first_c, -u, u)
                rows = pl.ds(g, tm // ROPE_PACK, stride=ROPE_PACK)
                rope_scr[dst, table, rows, :] = u
                rope_scr[dst, table + 2, rows, :] = u * q_factor

    @pl.when(step == -1)
    def _():
        prepare(xn_ref, posn_ref, 0)

    def proj(c0, n):
        assert n == IN_SEG and c0 % IN_SEG == 0
        return _dot(h_scr[slot], w_scr[c0 // IN_SEG])

    lbl = lbl_ref[...]
    lmax = jnp.max(lbl, axis=0, keepdims=True)
    lexp = jnp.exp(lbl - lmax)
    lb = lexp[0:1, :] / jnp.sum(lexp, axis=0, keepdims=True)

    def rope(zs, tables):
        rot = jnp.where(first, pltpu.roll(zs, LANES - ATT_HEAD_DIM // 2, 1), pltpu.roll(zs, ATT_HEAD_DIM // 2, 1))
        return zs * rope_scr[slot, tables] + rot * rope_scr[slot, tables + 1]

    def silu_to(ref, sl):
        def epi(z):
            hz = 0.5 * z
            ref[:, sl] = (hz + hz * jnp.tanh(hz)).astype(BF16)
        return epi

    def cast_to(ref, sl):
        def epi(z):
            ref[:, sl] = z.astype(BF16)
        return epi

    def sigmoid_to(ref, sl):
        def epi(z):
            ref[:, sl] = _sigmoid(z).astype(BF16)
        return epi

    def forget_to(sl):
        def epi(z):
            c = 0.5 - 0.5 * lb[:, sl]
            ct = c * jnp.tanh(0.5 * z)
            lf_ref[:, sl] = jnp.log2((lb[:, sl] + c) + ct)
            ka_ref[:, sl] = (c - ct).astype(BF16)
        return epi

    def attn_to(ref, d, slab, tables):
        def epi(z):
            for s in range(ATT_DIM // LANES):
                ls = slice(s * LANES, (s + 1) * LANES)
                val = z[:, ls]
                if tables is not None:
                    val = rope(val, tables)
                if d == 1:
                    ref[:, ls] = val.astype(BF16)
                    continue
                perm_scr[0, slab] = val
                if d > PERM_STRIDE:
                    for r1 in range(PERM_STRIDE):
                        perm_scr[1, slab, r1 * (tm // PERM_STRIDE):(r1 + 1) * (tm // PERM_STRIDE), :] = (
                            perm_scr[0, slab, pl.ds(r1, tm // PERM_STRIDE, stride=PERM_STRIDE), :])
                for r in range(d):
                    cols = slice(r * ATT_DIM + s * LANES, r * ATT_DIM + (s + 1) * LANES)
                    if d > PERM_STRIDE:
                        r1, r2 = r % PERM_STRIDE, r // PERM_STRIDE
                        rows = pl.ds(r1 * (tm // PERM_STRIDE) + r2, tm // d, stride=PERM_STRIDE)
                        ref[:, cols] = perm_scr[1, slab, rows, :].astype(BF16)
                    else:
                        ref[:, cols] = perm_scr[0, slab, pl.ds(r, tm // d, stride=d), :].astype(BF16)
        return epi

    halves = [slice(j * IN_SEG, (j + 1) * IN_SEG) for j in range(HG_DIM // IN_SEG)]
    segments = []
    for j, sl in enumerate(halves):
        segments += [(C_HQ + j * IN_SEG, IN_SEG, silu_to(qa_ref, sl)), (C_HF + j * IN_SEG, IN_SEG, forget_to(sl)),
                     (C_HG + j * IN_SEG, IN_SEG, silu_to(ga_ref, sl))]
    for j in range(2 * D_MODEL // IN_SEG):
        segments.append((C_GATES + j * IN_SEG, IN_SEG,
                         sigmoid_to(gates_ref, slice(j * IN_SEG, (j + 1) * IN_SEG))))
    segments.append((C_AG, ATT_DIM, silu_to(ag_ref, slice(0, ATT_DIM))))
    att_refs = ((q1_ref, k1_ref, v1_ref), (q2_ref, k2_ref, v2_ref), (q3_ref, k3_ref, v3_ref))
    for g in reversed(range(N_GROUPS)):
        d = ATT_DILATIONS[g]
        segments += [(C_AQ + g * ATT_DIM, ATT_DIM, attn_to(att_refs[g][0], d, 0, 2)),
                     (C_AK + g * ATT_DIM, ATT_DIM, attn_to(att_refs[g][1], d, 1, 0)),
                     (C_AV + g * ATT_DIM, ATT_DIM, attn_to(att_refs[g][2], d, 2, None))]
    for j, sl in enumerate(halves):
        segments.append((C_HI + j * IN_SEG, IN_SEG, cast_to(va_ref, sl)))

    @pl.when(step >= 0)
    def _():
        z = proj(*segments[0][:2])
        prepare(xn_ref, posn_ref, 1 - slot)
        for j, (_, _, epilogue) in enumerate(segments):
            z_next = proj(*segments[j + 1][:2]) if j + 1 < len(segments) else None
            epilogue(z)
            z = z_next


def _inproj(x2, pos_b, norm_w, w_in, lb_logits, invf):
    T = x2.shape[0]
    tm = IN_TM
    steps = T // tm
    tile = lambda i: jnp.clip(i - IN_W_STEPS, 0, steps - 1)
    row = lambda n: pl.BlockSpec((tm, n), lambda i: (tile(i), 0))
    const = lambda shape: pl.BlockSpec(shape, lambda i: (0, 0))
    nxt = lambda r, n: pl.BlockSpec((r, n), lambda i: (tile(i + 1), 0))
    out_shapes = ([jax.ShapeDtypeStruct((T, HG_DIM), BF16),
                   jax.ShapeDtypeStruct((T, HG_DIM), BF16),
                   jax.ShapeDtypeStruct((T, HG_DIM), F32),
                   jax.ShapeDtypeStruct((T, HG_DIM), BF16),
                   jax.ShapeDtypeStruct((T, HG_DIM), BF16)]
                  + [jax.ShapeDtypeStruct((T // d, d * ATT_DIM), BF16) for d in ATT_DILATIONS for _ in range(3)]
                  + [jax.ShapeDtypeStruct((T, ATT_DIM), BF16),
                     jax.ShapeDtypeStruct((T, 2 * D_MODEL), BF16)])
    out_specs = ([row(HG_DIM)] * 5
                 + [pl.BlockSpec((tm // d, d * ATT_DIM), lambda i: (tile(i), 0))
                    for d in ATT_DILATIONS for _ in range(3)]
                 + [row(ATT_DIM), row(2 * D_MODEL)])
    return pl.pallas_call(
        _inproj_kernel,
        grid=(IN_W_STEPS + steps,),
        in_specs=[nxt(tm, D_MODEL), nxt(tm // ROPE_PACK, LANES), const((1, D_MODEL)),
                  pl.BlockSpec((D_MODEL, IN_W_PER_STEP * IN_SEG), lambda i: (0, jnp.minimum(i, IN_W_STEPS - 1))),
                  const((2, HG_DIM)), const((1, LANES))],
        out_specs=out_specs,
        out_shape=out_shapes,
        scratch_shapes=[pltpu.VMEM((2, 3, tm, LANES), F32), pltpu.VMEM((2, tm, D_MODEL), BF16),
                        pltpu.VMEM((2, 4, tm, LANES), F32), pltpu.VMEM((IN_W_CHUNKS, D_MODEL, IN_SEG), BF16)],
        compiler_params=pltpu.CompilerParams(dimension_semantics=("arbitrary",),
                                             vmem_limit_bytes=VMEM_LIMIT_BYTES),
        name="inproj",
    )(x2, pos_b, norm_w, w_in, lb_logits, invf)


def _hgrn_sum_matrix():
    C = HG_CHUNK
    fine = [m for m in HG_LEVEL_SIZES if HG_VPU_LEVEL < m < HG_ROW_TILE]
    A = np.zeros(((1 + len(fine)) * C, C), np.float32)
    for t in range(C):
        A[t, :t + 1] = 1.0
    for i, m in enumerate(fine):
        for t in range(C):
            start = (t // m) * m
            if t & m:
                A[(1 + i) * C + t, start:t + 1] = 1.0
            else:
                A[(1 + i) * C + t, t + 1:start + m] = 1.0
    return np.concatenate([A, A], axis=1)


def _hgrn_units(q_ref, k_ref, lf_ref, v_ref, g_ref, a_ref, nw_ref, o_ref, state_scr, e_scr):
    C = HG_CHUNK
    n_lev = len(HG_LEVEL_SIZES)
    n_chunks = q_ref.shape[0] // C

    @pl.when(pl.program_id(1) == 0)
    def _():
        state_scr[...] = jnp.zeros_like(state_scr)

    t_idx = lax.broadcasted_iota(jnp.int32, (C, HG_GROUP * C), 0)
    s_idx = lax.broadcasted_iota(jnp.int32, (C, HG_GROUP * C), 1) & (C - 1)
    diag_mask = t_idx == s_idx
    level_masks = []
    for m in HG_LEVEL_SIZES:
        sh = (2 * m).bit_length() - 1
        level_masks.append(((t_idx & m) != 0) & ((s_idx & m) == 0) & ((t_idx >> sh) == (s_idx >> sh)))
    row_full = lax.broadcasted_iota(jnp.int32, (C, HG_DIM), 0)
    row_idx = lax.broadcasted_iota(jnp.int32, (C, HG_KEY), 0)
    right_rows = [(row_idx & m) != 0 for m in HG_LEVEL_SIZES]

    def exponents(ci):
        slot = ci % 2
        lf = lf_ref[ci * C:(ci + 1) * C, :]
        hi = lf.astype(BF16)
        lo = (lf - hi.astype(F32)).astype(BF16)
        sums = _dot(a_ref[...], jnp.concatenate([hi, lo], axis=0))
        b = sums[0:C, :]

        def put(i, arg):
            e_scr[slot, i * C:(i + 1) * C, :] = jnp.exp2(arg)

        i_fine = 1
        for i, m in enumerate(HG_LEVEL_SIZES):
            if m >= HG_ROW_TILE:
                parts = []
                for j in range(C // (2 * m)):
                    ref = jnp.broadcast_to(b[j * 2 * m + m - 1:j * 2 * m + m, :], (m, HG_DIM))
                    parts += [ref - b[j * 2 * m:j * 2 * m + m, :], b[j * 2 * m + m:(j + 1) * 2 * m, :] - ref]
                put(i, jnp.concatenate(parts, axis=0))
            elif m > HG_VPU_LEVEL:
                put(i, sums[i_fine * C:(i_fine + 1) * C, :])
                i_fine += 1
            elif m == 2:
                nxt = pltpu.roll(lf, C - 1, 0)
                prv = pltpu.roll(lf, 1, 0)
                pos4 = row_full & 3
                put(i, jnp.where(pos4 == 0, nxt, jnp.where(pos4 == 1, 0.0, jnp.where(pos4 == 2, lf, lf + prv))))
            else:
                put(i, jnp.where((row_full & 1) == 1, lf, 0.0))
        put(n_lev, b)
        put(n_lev + 1, b[C - 1:C, :] - b)

    def block_diag(xs):
        zero = jnp.zeros_like(xs[0])
        return jnp.concatenate([jnp.concatenate([x if b == a else zero for b in range(len(xs))], axis=1)
                                for a, x in enumerate(xs)], axis=0)

    def pair_scores(ci, hp):
        slot = ci % 2
        rows = slice(ci * C, (ci + 1) * C)
        lanes = [slice(h * HG_KEY, (h + 1) * HG_KEY) for h in range(HG_GROUP * hp, HG_GROUP * (hp + 1))]
        q_bf =[q_ref[rows, hl] for hl in lanes]
        k_bf = [k_ref[rows, hl] for hl in lanes]
        q = [x.astype(F32) for x in q_bf]
        k = [x.astype(F32) for x in k_bf]
        scores = jnp.where(diag_mask, _dot_nt(jnp.concatenate(q_bf, axis=1), block_diag(k_bf)), 0.0)
        for i, m in enumerate(HG_LEVEL_SIZES):
            zs = []
            for a, hl in enumerate(lanes):
                if m >= HG_ROW_TILE:
                    w = jnp.concatenate([(q[a] if j % 2 else k[a])[j * m:(j + 1) * m, :] for j in range(C // m)],
                                        axis=0)
                else:
                    w = jnp.where(right_rows[i], q[a], k[a])
                zs.append((w * e_scr[slot, i * C:(i + 1) * C, hl]).astype(BF16))
            scores = jnp.where(level_masks[i], _dot_nt(jnp.concatenate(zs, axis=1), block_diag(zs)), scores)
        q_dec = [(q[a] * e_scr[slot, n_lev * C:(n_lev + 1) * C, hl]).astype(BF16) for a, hl in enumerate(lanes)]
        k_dec = [(k[a] * e_scr[slot, (n_lev + 1) * C:(n_lev + 2) * C, hl]).astype(BF16) for a, hl in enumerate(lanes)]
        return scores, q_dec, k_dec

    def finish(ci, hp, stage1):
        scores, q_dec, k_dec = stage1
        slot = ci % 2
        rows = slice(ci * C, (ci + 1) * C)
        heads = tuple(range(HG_GROUP * hp, HG_GROUP * (hp + 1)))
        lanes = [slice(h * HG_KEY, (h + 1) * HG_KEY) for h in heads]
        v_bf = [v_ref[rows, hl] for hl in lanes]
        o_pair = _dot(scores.astype(BF16), block_diag(v_bf))
        for a, (h, hl) in enumerate(zip(heads, lanes)):
            e_last = e_scr[slot, (n_lev + 1) * C - 1:(n_lev + 1) * C, hl]
            st = state_scr[h]
            o = o_pair[:, a * HG_VAL:(a + 1) * HG_VAL] + _dot_nt(q_dec[a], st.astype(BF16))
            state_scr[h] = st * e_last + _dot_tn(v_bf[a], k_dec[a])
            ms = jnp.mean(o * o, axis=-1, keepdims=True)
            y = o * lax.rsqrt(ms + NORM_EPS) * nw_ref[0:1, 0:HG_VAL] * g_ref[rows, hl].astype(F32)
            o_ref[rows, hl] = y.astype(BF16)

    n_pairs = HG_HEADS // HG_GROUP
    exponents(0)
    pending = []
    for ci in range(n_chunks):
        for hp in range(n_pairs):
            if hp == n_pairs // 2 and ci + 1 < n_chunks:
                exponents(ci + 1)
            pending.append((ci, hp, pair_scores(ci, hp)))
            if len(pending) > HG_SKEW:
                finish(*pending.pop(0))
            yield
    while pending:
        finish(*pending.pop(0))
        yield


N_HGRN_IN = 7


def _attn_units(refs, seq):
    n_in = 3 * sum(ATT_DILATIONS) + 1
    in_refs = refs[:n_in]
    ag_ref = in_refs[-1]
    out_ref = refs[n_in]
    pairs = out_ref.shape[1] // LANES
    o_scr = refs[n_in + 1:n_in + 1 + N_GROUPS]
    lse_scr = refs[n_in + 1 + N_GROUPS:n_in + 1 + 2 * N_GROUPS]
    n = ATT_STEPS

    lane = lax.broadcasted_iota(jnp.int32, (n, LANES), 1)
    head_a = lane < ATT_HEAD_DIM
    qi = lax.broadcasted_iota(jnp.int32, (n, 2 * n), 0)
    kj = lax.broadcasted_iota(jnp.int32, (n, 2 * n), 1)
    band_ok = (kj >= qi) & (kj <= qi + n)
    first_ok = (lax.broadcasted_iota(jnp.int32, (n, n), 1)
                <= lax.broadcasted_iota(jnp.int32, (n, n), 0))

    def scores(q_ref, k_ref, c, pl_):
        q = q_ref[c * n:(c + 1) * n, pl_]
        kk = k_ref[max(c - 1, 0) * n:(c + 1) * n, pl_]
        zero_q = jnp.zeros_like(q)
        return [_dot_nt(jnp.where(head_a, q, zero_q), kk), _dot_nt(jnp.where(head_a, zero_q, q), kk)]

    def finish(g, d, r, c, p, v_ref, s_pair):
        ok = first_ok if c == 0 else band_ok
        vv = v_ref[max(c - 1, 0) * n:(c + 1) * n, p * LANES:(p + 1) * LANES]
        ps, ms = [], []
        for s in s_pair:
            s = jnp.where(ok, s, NEG_BIG)
            sm = s if c == 0 else jnp.maximum(s[:, :n], s[:, n:])
            m = jnp.max(sm, axis=-1, keepdims=True)
            ps.append(jnp.exp2(s - m).astype(BF16))
            ms.append(m)
        ol = _dot(jnp.concatenate(ps, axis=0), jnp.concatenate([vv, jnp.ones_like(vv)], axis=1))
        l = jnp.where(head_a, ol[:n, LANES:], ol[n:, LANES:])
        o = jnp.where(head_a, ol[:n, :LANES], ol[n:, :LANES]) * (1.0 / l)
        lse = jnp.where(head_a, ms[0], ms[1]) + jnp.log2(l)
        dst = pl.ds(c * n, n) if d == 1 else pl.ds(c * (n * d) + r, n, stride=d)
        o_scr[g][p, dst, :] = o
        lse_scr[g][p, dst, :] = lse

    pending = []
    for p in range(pairs):
        pos = 0
        for g, d in enumerate(ATT_DILATIONS):
            q_refs = in_refs[pos:pos + d]
            k_refs = in_refs[pos + d:pos + 2 * d]
            v_refs = in_refs[pos + 2 * d:pos + 3 * d]
            pos += 3 * d
            for r in range(d):
                for c in range(seq // d // n):
                    s_pair = scores(q_refs[r], k_refs[r], c, slice(p * LANES, (p + 1) * LANES))
                    pending.append((g, d, r, c, p, v_refs[r], s_pair))
                    if len(pending) > ATT_SKEW:
                        finish(*pending.pop(0))
                    yield
    while pending:
        finish(*pending.pop(0))
        yield

    rows_per = 256

    def merge(i, carry):
        rows = pl.ds(pl.multiple_of(i * rows_per, rows_per), rows_per)
        for p in range(pairs):
            lanes = slice(p * LANES, (p + 1) * LANES)
            l1, l2, l3 = lse_scr[0][p, rows, :], lse_scr[1][p, rows, :], lse_scr[2][p, rows, :]
            mx = jnp.maximum(jnp.maximum(l1, l2), l3)
            w1, w2, w3 = jnp.exp2(l1 - mx), jnp.exp2(l2 - mx), jnp.exp2(l3 - mx)
            num = w1 * o_scr[0][p, rows, :] + w2 * o_scr[1][p, rows, :] + w3 * o_scr[2][p, rows, :]
            ob = num / (w1 + w2 + w3) * ag_ref[rows, lanes].astype(F32)
            out_ref[rows, lanes] = ob.astype(BF16)
        return carry

    lax.fori_loop(0, seq // rows_per, merge, 0)


def _branches_kernel(*refs, seq):
    n_att_in = 3 * sum(ATT_DILATIONS) + 1
    h_in = refs[:N_HGRN_IN]
    a_in = refs[N_HGRN_IN:N_HGRN_IN + n_att_in]
    oa_ref, ob_ref = refs[N_HGRN_IN + n_att_in:N_HGRN_IN + n_att_in + 2]
    scratch = refs[N_HGRN_IN + n_att_in + 2:]
    streams = [_hgrn_units(*h_in, oa_ref, *scratch[:2]), _attn_units(a_in + (ob_ref,) + scratch[2:], seq)]
    counts = [(oa_ref.shape[0] // HG_CHUNK) * (HG_HEADS // HG_GROUP) + HG_SKEW,
              (ob_ref.shape[1] // LANES) * sum(seq // ATT_STEPS // d * d for d in ATT_DILATIONS) + ATT_SKEW]
    done = [0, 0]
    live = [True, True]
    while any(live):
        k = min((i for i in range(2) if live[i]), key=lambda i: done[i] / counts[i])
        try:
            next(streams[k])
            done[k] += 1
        except StopIteration:
            live[k] = False


def _branches(qa, ka, lf, va, ga, a_mat, hgrn_w, qkv, ag, B, S):
    width = ATT_PAIRS * LANES
    steps = ATT_DIM // width
    ts = S // steps
    assert ts % HG_CHUNK == 0
    blk = pl.BlockSpec((ts, HG_DIM), lambda b, i: (b * steps + i, 0))
    rows_e = (len(HG_LEVEL_SIZES) + 2) * HG_CHUNK
    operands = [qa, ka, lf, va, ga, a_mat, hgrn_w]
    in_specs = [blk, blk, blk, blk, blk,
                pl.BlockSpec(a_mat.shape, lambda b, i: (0, 0)),
                pl.BlockSpec((PAR_ROWS, HG_DIM), lambda b, i: (0, 0))]
    for g, d in enumerate(ATT_DILATIONS):
        m_len = S // d
        for arr in qkv[g]:
            view = arr.reshape(B, m_len, d * ATT_DIM)
            for r in range(d):
                operands.append(view)
                in_specs.append(pl.BlockSpec((None, m_len, width),
                                             functools.partial(lambda b, hp, r: (b, 0, r * steps + hp), r=r)))
    operands.append(ag.reshape(B, S, ATT_DIM))
    nat = pl.BlockSpec((None, S, width), lambda b, hp: (b, 0, hp))
    in_specs.append(nat)
    oa, ob = pl.pallas_call(
        functools.partial(_branches_kernel, seq=S),
        grid=(B, steps),
        in_specs=in_specs,
        out_specs=[blk, nat],
        out_shape=[jax.ShapeDtypeStruct((B * S, HG_DIM), BF16), jax.ShapeDtypeStruct((B, S, ATT_DIM), BF16)],
        scratch_shapes=([pltpu.VMEM((HG_HEADS, HG_VAL, HG_KEY), F32), pltpu.VMEM((2, rows_e, HG_DIM), F32)]
                        + [pltpu.VMEM((ATT_PAIRS, S, LANES), F32)] * (2 * N_GROUPS)),
        compiler_params=pltpu.CompilerParams(dimension_semantics=("parallel", "arbitrary"),
                                             vmem_limit_bytes=VMEM_LIMIT_BYTES),
        name="branches",
    )(*operands)
    return oa, ob.reshape(B * S, ATT_DIM)


def _out_kernel(x_ref, oa_ref, ob_ref, gates_ref, wa32_ref, wb32_ref, wo32_ref, fnw_ref, out_ref,
                wa_ref, wb_ref, wo_ref):
    @pl.when(pl.program_id(0) == 0)
    def _():
        wa_ref[...] = wa32_ref[...].astype(BF16)
        wb_ref[...] = wb32_ref[...].astype(BF16)
        wo_ref[...] = wo32_ref[...].astype(BF16)

    def branches(rows):
        return _dot(oa_ref[rows, :], wa_ref[...]), _dot(ob_ref[rows, :], wb_ref[...])

    def finish(rows, ya, yb):
        g_a = gates_ref[rows, :D_MODEL].astype(F32)
        g_b = gates_ref[rows, D_MODEL:].astype(F32)
        merged = (g_a * ya + g_b * yb).astype(BF16)
        y = x_ref[rows, :] + _dot(merged, wo_ref[...])
        ms = jnp.mean(y * y, axis=-1, keepdims=True)
        out_ref[rows, :] = y * lax.rsqrt(ms + NORM_EPS) * fnw_ref[0:1, :]

    pending = None
    for r in range(x_ref.shape[0] // OUT_SUB):
        rows = slice(r * OUT_SUB, (r + 1) * OUT_SUB)
        nxt = (rows, *branches(rows))
        if pending is not None:
            finish(*pending)
        pending = nxt
    finish(*pending)


def _output(x2, oa, ob, gates, wa, wb, wo, fnw):
    T = x2.shape[0]
    tm = OUT_TM
    row = lambda n: pl.BlockSpec((tm, n), lambda i: (i, 0))
    const = lambda shape: pl.BlockSpec(shape, lambda i: (0, 0))
    return pl.pallas_call(
        _out_kernel,
        grid=(T // tm,),
        in_specs=[row(D_MODEL), row(HG_DIM), row(ATT_DIM), row(2 * D_MODEL),
                  const((HG_DIM, D_MODEL)), const((ATT_DIM, D_MODEL)), const((D_MODEL, D_MODEL)),
                  const((PAR_ROWS, D_MODEL))],
        out_specs=row(D_MODEL),
        out_shape=jax.ShapeDtypeStruct((T, D_MODEL), F32),
        scratch_shapes=[pltpu.VMEM((HG_DIM, D_MODEL), BF16), pltpu.VMEM((ATT_DIM, D_MODEL), BF16),
                        pltpu.VMEM((D_MODEL, D_MODEL), BF16)],
        compiler_params=pltpu.CompilerParams(dimension_semantics=("arbitrary",),
                                             vmem_limit_bytes=VMEM_LIMIT_BYTES),
        name="merge_out",
    )(x2, oa, ob, gates, wa, wb, wo, fnw)


def kernel(x, positions, norm_w, w_in, lb_logits, hgrn_norm_w, w_branch_a, w_branch_b, w_out, final_norm_w):
    B, S, D = x.shape
    assert D == D_MODEL and norm_w.shape[0] == 1 and lb_logits.shape == (2, HG_DIM)
    assert S % (ATT_STEPS * max(ATT_DILATIONS)) == 0 and (B * S) % OUT_TM == 0
    T = B * S
    x2 = x.reshape(T, D)
    pos_b = jnp.repeat(positions.reshape(T // ROPE_PACK, ROPE_PACK), ROPE_FREQS, axis=1)
    inv_freq = ROPE_THETA ** (-jnp.arange(0, ATT_HEAD_DIM, 2, dtype=F32) / ATT_HEAD_DIM)
    invf = jnp.tile(inv_freq, LANES // ROPE_FREQS).reshape(1, LANES)

    outs = _inproj(x2, pos_b, norm_w.reshape(1, D), w_in[0].astype(F32), lb_logits.astype(F32), invf)
    qa, ka, lf, va, ga = outs[:5]
    qkv = [tuple(outs[5 + 3 * g:8 + 3 * g]) for g in range(N_GROUPS)]
    ag, gates = outs[14], outs[15]

    a_mat = jnp.asarray(_hgrn_sum_matrix(), BF16)
    hgrn_w = jnp.broadcast_to(jnp.tile(hgrn_norm_w.reshape(1, HG_VAL).astype(F32), (1, HG_HEADS)), (PAR_ROWS, HG_DIM))
    oa, ob = _branches(qa, ka, lf, va, ga, a_mat, hgrn_w, qkv, ag, B, S)
    out = _output(x2, oa, ob, gates, w_branch_a[0].astype(F32), w_branch_b[0].astype(F32),
                  w_out[0].astype(F32), jnp.broadcast_to(final_norm_w.reshape(1, D).astype(F32), (PAR_ROWS, D)))
    return out.reshape(B, S, D)
```

```python
import functools

import numpy as np
import jax
import jax.numpy as jnp
from jax import lax
from jax.experimental import pallas as pl
from jax.experimental.pallas import tpu as pltpu

D_MODEL = 1024
HG_HEADS = 8
HG_KEY = 128
HG_VAL = 128
HG_DIM = HG_HEADS * HG_KEY
ATT_DILATIONS = (1, 4, 16)
ATT_STEPS = 128
N_GROUPS = 3
ATT_HEADS = 8
ATT_HEAD_DIM = 64
ATT_DIM = ATT_HEADS * ATT_HEAD_DIM
ROPE_THETA = 10000.0
ROPE_FREQS = ATT_HEAD_DIM // 2
ROPE_PACK = 4
NORM_EPS = 1e-6
LOG2_E = 1.4426950408889634
C_HQ, C_HF, C_HI, C_HG = 0, 1024, 2048, 3072
C_AQ, C_AK, C_AV = 4096, 4096 + 1536, 4096 + 3072
C_AG = 8704
C_GATES = 9216
IN_COLS = 11264

LANES = 128
VMEM_LIMIT_BYTES = 56 * 1024 * 1024

PAR_ROWS = 8

IN_TM = 256
IN_SEG = 512
IN_W_CHUNKS = IN_COLS // IN_SEG
IN_W_PER_STEP = 2
IN_W_STEPS = IN_W_CHUNKS // IN_W_PER_STEP
PERM_STRIDE = 4
HG_CHUNK = 64
HG_TS = 1024
HG_LEVEL_SIZES = tuple(HG_CHUNK >> (l + 1) for l in range(HG_CHUNK.bit_length() - 1))
HG_ROW_TILE = 8
HG_VPU_LEVEL = 1
HG_GROUP = 2
HG_SKEW = 1
OUT_TM = 1024
OUT_SUB = 256
ATT_PAIRS = 2
ATT_SKEW = 3
NEG_BIG = -1e30

F32 = jnp.float32
BF16 = jnp.bfloat16


def _dot(a, b):
    return jnp.dot(a, b, preferred_element_type=F32)


def _dot_nt(a, b):
    return lax.dot_general(a, b, (((1,), (1,)), ((), ())), preferred_element_type=F32)


def _dot_tn(a, b):
    return lax.dot_general(a, b, (((0,), (0,)), ((), ())), preferred_element_type=F32)


def _sigmoid(z):
    return 0.5 + 0.5 * jnp.tanh(0.5 * z)


def _inproj_kernel(xn_ref, posn_ref, nw_ref, wchunk_ref, lbl_ref, invf_ref,
                   qa_ref, ka_ref, lf_ref, va_ref, ga_ref,
                   q1_ref, k1_ref, v1_ref, q2_ref, k2_ref, v2_ref, q3_ref, k3_ref, v3_ref,
                   ag_ref, gates_ref, perm_scr, h_scr, rope_scr, w_scr):
    tm = xn_ref.shape[0]
    i = pl.program_id(0)
    step = i - IN_W_STEPS
    slot = step & 1

    @pl.when(step < 0)
    def _():
        for k in range(IN_W_PER_STEP):
            w_scr[i * IN_W_PER_STEP + k] = wchunk_ref[:, k * IN_SEG:(k + 1) * IN_SEG].astype(BF16)
    lane = lax.broadcasted_iota(jnp.int32, (tm, LANES), 1)
    first = (lane & (ATT_HEAD_DIM - 1)) < (ATT_HEAD_DIM // 2)

    def prepare(x_ref, pos_ref, dst):
        x = x_ref[...]
        ms = jnp.mean(x * x, axis=-1, keepdims=True)
        h_scr[dst] = (x * lax.rsqrt(ms + NORM_EPS) * nw_ref[...]).astype(BF16)

        ang = pos_ref[...].astype(F32) * invf_ref[...]
        lane_c = lax.broadcasted_iota(jnp.int32, ang.shape, 1)
        first_c = (lane_c & (ATT_HEAD_DIM - 1)) < (ATT_HEAD_DIM // 2)
        q_factor = ATT_HEAD_DIM ** -0.5 * LOG2_E
        for table, compact in ((0, jnp.cos(ang)), (1, jnp.sin(ang))):
            for g in range(ROPE_PACK):
                u = compact if g == 0 else pltpu.roll(compact, LANES - g * ROPE_FREQS, 1)
                u = jnp.where(lane_c < ROPE_FREQS, u, pltpu.roll(u, ROPE_FREQS, 1))
                u = jnp.where(lane_c < 2 * ROPE_FREQS, u, pltpu.roll(u, 2 * ROPE_FREQS, 1))
                if table == 1:
                    u = jnp.where(first_c, -u, u)
                rows = pl.ds(g, tm // ROPE_PACK, stride=ROPE_PACK)
                rope_scr[dst, table, rows, :] = u
                rope_scr[dst, table + 2, rows, :] = u * q_factor

    @pl.when(step == -1)
    def _():
        prepare(xn_ref, posn_ref, 0)

    def proj(c0, n):
        assert n == IN_SEG and c0 % IN_SEG == 0
        return _dot(h_scr[slot], w_scr[c0 // IN_SEG])

    lbl = lbl_ref[...]
    lmax = jnp.max(lbl, axis=0, keepdims=True)
    lexp = jnp.exp(lbl - lmax)
    lb = lexp[0:1, :] / jnp.sum(lexp, axis=0, keepdims=True)

    def rope(zs, tables):
        rot = jnp.where(first, pltpu.roll(zs, LANES - ATT_HEAD_DIM // 2, 1), pltpu.roll(zs, ATT_HEAD_DIM // 2, 1))
        return zs * rope_scr[slot, tables] + rot * rope_scr[slot, tables + 1]

    def silu_to(ref, sl):
        def epi(z):
            hz = 0.5 * z
            ref[:, sl] = (hz + hz * jnp.tanh(hz)).astype(BF16)
        return epi

    def cast_to(ref, sl):
        def epi(z):
            ref[:, sl] = z.astype(BF16)
        return epi

    def sigmoid_to(ref, sl):
        def epi(z):
            ref[:, sl] = _sigmoid(z).astype(BF16)
        return epi

    def forget_to(sl):
        def epi(z):
            c = 0.5 - 0.5 * lb[:, sl]
            ct = c * jnp.tanh(0.5 * z)
            lf_ref[:, sl] = jnp.log2((lb[:, sl] + c) + ct)
            ka_ref[:, sl] = (c - ct).astype(BF16)
        return epi

    def attn_to(ref, d, slab, tables):
        def epi(z):
            for s in range(ATT_DIM // LANES):
                ls = slice(s * LANES, (s + 1) * LANES)
                val = z[:, ls]
                if tables is not None:
                    val = rope(val, tables)
                if d == 1:
                    ref[:, ls] = val.astype(BF16)
                    continue
                perm_scr[0, slab] = val
                if d > PERM_STRIDE:
                    for r1 in range(PERM_STRIDE):
                        perm_scr[1, slab, r1 * (tm // PERM_STRIDE):(r1 + 1) * (tm // PERM_STRIDE), :] = (
                            perm_scr[0, slab, pl.ds(r1, tm // PERM_STRIDE, stride=PERM_STRIDE), :])
                for r in range(d):
                    cols = slice(r * ATT_DIM + s * LANES, r * ATT_DIM + (s + 1) * LANES)
                    if d > PERM_STRIDE:
                        r1, r2 = r % PERM_STRIDE, r // PERM_STRIDE
                        rows = pl.ds(r1 * (tm // PERM_STRIDE) + r2, tm // d, stride=PERM_STRIDE)
                        ref[:, cols] = perm_scr[1, slab, rows, :].astype(BF16)
                    else:
                        ref[:, cols] = perm_scr[0, slab, pl.ds(r, tm // d, stride=d), :].astype(BF16)
        return epi

    halves = [slice(j * IN_SEG, (j + 1) * IN_SEG) for j in range(HG_DIM // IN_SEG)]
    segments = []
    for j, sl in enumerate(halves):
        segments += [(C_HQ + j * IN_SEG, IN_SEG, silu_to(qa_ref, sl)), (C_HF + j * IN_SEG, IN_SEG, forget_to(sl)),
                     (C_HG + j * IN_SEG, IN_SEG, silu_to(ga_ref, sl))]
    for j in range(2 * D_MODEL // IN_SEG):
        segments.append((C_GATES + j * IN_SEG, IN_SEG,
                         sigmoid_to(gates_ref, slice(j * IN_SEG, (j + 1) * IN_SEG))))
    segments.append((C_AG, ATT_DIM, silu_to(ag_ref, slice(0, ATT_DIM))))
    att_refs = ((q1_ref, k1_ref, v1_ref), (q2_ref, k2_ref, v2_ref), (q3_ref, k3_ref, v3_ref))
    for g in reversed(range(N_GROUPS)):
        d = ATT_DILATIONS[g]
        segments += [(C_AQ + g * ATT_DIM, ATT_DIM, attn_to(att_refs[g][0], d, 0, 2)),
                     (C_AK + g * ATT_DIM, ATT_DIM, attn_to(att_refs[g][1], d, 1, 0)),
                     (C_AV + g * ATT_DIM, ATT_DIM, attn_to(att_refs[g][2], d, 2, None))]
    for j, sl in enumerate(halves):
        segments.append((C_HI + j * IN_SEG, IN_SEG, cast_to(va_ref, sl)))

    @pl.when(step >= 0)
    def _():
        z = proj(*segments[0][:2])
        prepare(xn_ref, posn_ref, 1 - slot)
        for j, (_, _, epilogue) in enumerate(segments):
            z_next = proj(*segments[j + 1][:2]) if j + 1 < len(segments) else None
            epilogue(z)
            z = z_next


def _inproj(x2, pos_b, norm_w, w_in, lb_logits, invf):
    T = x2.shape[0]
    tm = IN_TM
    steps = T // tm
    tile = lambda i: jnp.clip(i - IN_W_STEPS, 0, steps - 1)
    row = lambda n: pl.BlockSpec((tm, n), lambda i: (tile(i), 0))
    const = lambda shape: pl.BlockSpec(shape, lambda i: (0, 0))
    nxt = lambda r, n: pl.BlockSpec((r, n), lambda i: (tile(i + 1), 0))
    out_shapes = ([jax.ShapeDtypeStruct((T, HG_DIM), BF16),
                   jax.ShapeDtypeStruct((T, HG_DIM), BF16),
                   jax.ShapeDtypeStruct((T, HG_DIM), F32),
                   jax.ShapeDtypeStruct((T, HG_DIM), BF16),
                   jax.ShapeDtypeStruct((T, HG_DIM), BF16)]
                  + [jax.ShapeDtypeStruct((T // d, d * ATT_DIM), BF16) for d in ATT_DILATIONS for _ in range(3)]
                  + [jax.ShapeDtypeStruct((T, ATT_DIM), BF16),
                     jax.ShapeDtypeStruct((T, 2 * D_MODEL), BF16)])
    out_specs = ([row(HG_DIM)] * 5
                 + [pl.BlockSpec((tm // d, d * ATT_DIM), lambda i: (tile(i), 0))
                    for d in ATT_DILATIONS for _ in range(3)]
                 + [row(ATT_DIM), row(2 * D_MODEL)])
    return pl.pallas_call(
        _inproj_kernel,
        grid=(IN_W_STEPS + steps,),
        in_specs=[nxt(tm, D_MODEL), nxt(tm // ROPE_PACK, LANES), const((1, D_MODEL)),
                  pl.BlockSpec((D_MODEL, IN_W_PER_STEP * IN_SEG), lambda i: (0, jnp.minimum(i, IN_W_STEPS - 1))),
                  const((2, HG_DIM)), const((1, LANES))],
        out_specs=out_specs,
        out_shape=out_shapes,
        scratch_shapes=[pltpu.VMEM((2, 3, tm, LANES), F32), pltpu.VMEM((2, tm, D_MODEL), BF16),
                        pltpu.VMEM((2, 4, tm, LANES), F32), pltpu.VMEM((IN_W_CHUNKS, D_MODEL, IN_SEG), BF16)],
        compiler_params=pltpu.CompilerParams(dimension_semantics=("arbitrary",),
                                             vmem_limit_bytes=VMEM_LIMIT_BYTES),
        name="inproj",
    )(x2, pos_b, norm_w, w_in, lb_logits, invf)


def _hgrn_sum_matrix():
    C = HG_CHUNK
    fine = [m for m in HG_LEVEL_SIZES if HG_VPU_LEVEL < m < HG_ROW_TILE]
    A = np.zeros(((1 + len(fine)) * C, C), np.float32)
    for t in range(C):
        A[t, :t + 1] = 1.0
    for i, m in enumerate(fine):
        for t in range(C):
            start = (t // m) * m
            if t & m:
                A[(1 + i) * C + t, start:t + 1] = 1.0
            else:
                A[(1 + i) * C + t, t + 1:start + m] = 1.0
    return np.concatenate([A, A], axis=1)


def _hgrn_kernel(q_ref, k_ref, lf_ref, v_ref, g_ref, a_ref, nw_ref, o_ref, state_scr, e_scr):
    C = HG_CHUNK
    n_lev = len(HG_LEVEL_SIZES)
    n_chunks = HG_TS // C

    @pl.when(pl.program_id(1) == 0)
    def _():
        state_scr[...] = jnp.zeros_like(state_scr)

    t_idx = lax.broadcasted_iota(jnp.int32, (C, HG_GROUP * C), 0)
    s_idx = lax.broadcasted_iota(jnp.int32, (C, HG_GROUP * C), 1) & (C - 1)
    diag_mask = t_idx == s_idx
    level_masks = []
    for m in HG_LEVEL_SIZES:
        sh = (2 * m).bit_length() - 1
        level_masks.append(((t_idx & m) != 0) & ((s_idx & m) == 0) & ((t_idx >> sh) == (s_idx >> sh)))
    row_full = lax.broadcasted_iota(jnp.int32, (C, HG_DIM), 0)
    row_idx = lax.broadcasted_iota(jnp.int32, (C, HG_KEY), 0)
    right_rows = [(row_idx & m) != 0 for m in HG_LEVEL_SIZES]

    def exponents(ci):
        slot = ci % 2
        lf = lf_ref[ci * C:(ci + 1) * C, :]
        hi = lf.astype(BF16)
        lo = (lf - hi.astype(F32)).astype(BF16)
        sums = _dot(a_ref[...], jnp.concatenate([hi, lo], axis=0))
        b = sums[0:C, :]

        def put(i, arg):
            e_scr[slot, i * C:(i + 1) * C, :] = jnp.exp2(arg)

        i_fine = 1
        for i, m in enumerate(HG_LEVEL_SIZES):
            if m >= HG_ROW_TILE:
                parts = []
                for j in range(C // (2 * m)):
                    ref = jnp.broadcast_to(b[j * 2 * m + m - 1:j * 2 * m + m, :], (m, HG_DIM))
                    parts += [ref - b[j * 2 * m:j * 2 * m + m, :], b[j * 2 * m + m:(j + 1) * 2 * m, :] - ref]
                put(i, jnp.concatenate(parts, axis=0))
            elif m > HG_VPU_LEVEL:
                put(i, sums[i_fine * C:(i_fine + 1) * C, :])
                i_fine += 1
            elif m == 2:
                nxt = pltpu.roll(lf, C - 1, 0)
                prv = pltpu.roll(lf, 1, 0)
                pos4 = row_full & 3
                put(i, jnp.where(pos4 == 0, nxt, jnp.where(pos4 == 1, 0.0, jnp.where(pos4 == 2, lf, lf + prv))))
            else:
                put(i, jnp.where((row_full & 1) == 1, lf, 0.0))
        put(n_lev, b)
        put(n_lev + 1, b[C - 1:C, :] - b)

    def block_diag(xs):
        zero = jnp.zeros_like(xs[0])
        return jnp.concatenate([jnp.concatenate([x if b == a else zero for b in range(len(xs))], axis=1)
                                for a, x in enumerate(xs)], axis=0)

    def pair_scores(ci, hp):
        slot = ci % 2
        rows = slice(ci * C, (ci + 1) * C)
        lanes = [slice(h * HG_KEY, (h + 1) * HG_KEY) for h in range(HG_GROUP * hp, HG_GROUP * (hp + 1))]
        q_bf =[q_ref[rows, hl] for hl in lanes]
        k_bf = [k_ref[rows, hl] for hl in lanes]
        q = [x.astype(F32) for x in q_bf]
        k = [x.astype(F32) for x in k_bf]
        scores = jnp.where(diag_mask, _dot_nt(jnp.concatenate(q_bf, axis=1), block_diag(k_bf)), 0.0)
        for i, m in enumerate(HG_LEVEL_SIZES):
            zs = []
            for a, hl in enumerate(lanes):
                if m >= HG_ROW_TILE:
                    w = jnp.concatenate([(q[a] if j % 2 else k[a])[j * m:(j + 1) * m, :] for j in range(C // m)],
                                        axis=0)
                else:
                    w = jnp.where(right_rows[i], q[a], k[a])
                zs.append((w * e_scr[slot, i * C:(i + 1) * C, hl]).astype(BF16))
            scores = jnp.where(level_masks[i], _dot_nt(jnp.concatenate(zs, axis=1), block_diag(zs)), scores)
        q_dec = [(q[a] * e_scr[slot, n_lev * C:(n_lev + 1) * C, hl]).astype(BF16) for a, hl in enumerate(lanes)]
        k_dec = [(k[a] * e_scr[slot, (n_lev + 1) * C:(n_lev + 2) * C, hl]).astype(BF16) for a, hl in enumerate(lanes)]
        return scores, q_dec, k_dec

    def finish(ci, hp, stage1):
        scores, q_dec, k_dec = stage1
        slot = ci % 2
        rows = slice(ci * C, (ci + 1) * C)
        heads = tuple(range(HG_GROUP * hp, HG_GROUP * (hp + 1)))
        lanes = [slice(h * HG_KEY, (h + 1) * HG_KEY) for h in heads]
        v_bf = [v_ref[rows, hl] for hl in lanes]
        o_pair = _dot(scores.astype(BF16), block_diag(v_bf))
        for a, (h, hl) in enumerate(zip(heads, lanes)):
            e_last = e_scr[slot, (n_lev + 1) * C - 1:(n_lev + 1) * C, hl]
            st = state_scr[h]
            o = o_pair[:, a * HG_VAL:(a + 1) * HG_VAL] + _dot_nt(q_dec[a], st.astype(BF16))
            state_scr[h] = st * e_last + _dot_tn(v_bf[a], k_dec[a])
            ms = jnp.mean(o * o, axis=-1, keepdims=True)
            y = o * lax.rsqrt(ms + NORM_EPS) * nw_ref[0:1, 0:HG_VAL] * g_ref[rows, hl].astype(F32)
            o_ref[rows, hl] = y.astype(BF16)

    n_pairs = HG_HEADS // HG_GROUP
    exponents(0)
    pending = []
    for ci in range(n_chunks):
        for hp in range(n_pairs):
            if hp == n_pairs // 2 and ci + 1 < n_chunks:
                exponents(ci + 1)
            pending.append((ci, hp, pair_scores(ci, hp)))
            if len(pending) > HG_SKEW:
                finish(*pending.pop(0))
    while pending:
        finish(*pending.pop(0))


def _hgrn(qa, ka, lf, va, ga, a_mat, hgrn_w, B, S):
    ts = HG_TS
    steps = S // ts
    blk = pl.BlockSpec((ts, HG_DIM), lambda b, i: (b * steps + i, 0))
    rows_e = (len(HG_LEVEL_SIZES) + 2) * HG_CHUNK
    return pl.pallas_call(
        _hgrn_kernel,
        grid=(B, steps),
        in_specs=[blk, blk, blk, blk, blk,
                  pl.BlockSpec(a_mat.shape, lambda b, i: (0, 0)),
                  pl.BlockSpec((PAR_ROWS, HG_DIM), lambda b, i: (0, 0))],
        out_specs=blk,
        out_shape=jax.ShapeDtypeStruct((B * S, HG_DIM), BF16),
        scratch_shapes=[pltpu.VMEM((HG_HEADS, HG_VAL, HG_KEY), F32),
                        pltpu.VMEM((2, rows_e, HG_DIM), F32)],
        compiler_params=pltpu.CompilerParams(dimension_semantics=("parallel", "arbitrary"),
                                             vmem_limit_bytes=VMEM_LIMIT_BYTES),
        name="hgrn2",
    )(qa, ka, lf, va, ga, a_mat, hgrn_w)


def _attn_kernel(*refs, seq):
    n_in = 3 * sum(ATT_DILATIONS) + 1
    in_refs = refs[:n_in]
    ag_ref = in_refs[-1]
    out_ref = refs[n_in]
    o_scr = refs[n_in + 1:n_in + 1 + N_GROUPS]
    lse_scr = refs[n_in + 1 + N_GROUPS:n_in + 1 + 2 * N_GROUPS]
    vm_scr = refs[n_in + 1 + 2 * N_GROUPS]
    n = ATT_STEPS

    lane = lax.broadcasted_iota(jnp.int32, (n, LANES), 1)
    head_a = lane < ATT_HEAD_DIM
    qi = lax.broadcasted_iota(jnp.int32, (n, 2 * n), 0)
    kj = lax.broadcasted_iota(jnp.int32, (n, 2 * n), 1)
    band_ok = (kj >= qi) & (kj <= qi + n)
    first_ok = (lax.broadcasted_iota(jnp.int32, (n, n), 1)
                <= lax.broadcasted_iota(jnp.int32, (n, n), 0))

    def row_sum_matrix(keys):
        rl = lax.broadcasted_iota(jnp.int32, (2 * keys, LANES), 0)
        ll = lax.broadcasted_iota(jnp.int32, (2 * keys, LANES), 1)
        return jnp.where((rl < keys) == (ll < ATT_HEAD_DIM), 1.0, 0.0).astype(BF16)

    ones_two = row_sum_matrix(2 * n)
    ones_one = row_sum_matrix(n)

    def scores(q_ref, k_ref, c, pl_):
        q = q_ref[c * n:(c + 1) * n, pl_]
        kk = k_ref[max(c - 1, 0) * n:(c + 1) * n, pl_]
        zero_q = jnp.zeros_like(q)
        return [_dot_nt(jnp.where(head_a, q, zero_q), kk), _dot_nt(jnp.where(head_a, zero_q, q), kk)]

    def finish(g, d, r, c, p, v_ref, s_pair):
        ok = first_ok if c == 0 else band_ok
        keys = slice(r * (seq // d) + max(c - 1, 0) * n, r * (seq // d) + (c + 1) * n)
        ps, ms = [], []
        for s in s_pair:
            s = jnp.where(ok, s, NEG_BIG)
            sm = s if c == 0 else jnp.maximum(s[:, :n], s[:, n:])
            m = jnp.max(sm, axis=-1, keepdims=True)
            ps.append(jnp.exp2(s - m).astype(BF16))
            ms.append(m)
        v2 = jnp.concatenate([vm_scr[g, 0, keys, :], vm_scr[g, 1, keys, :]], axis=0)
        ol = _dot(jnp.concatenate(ps, axis=1), jnp.concatenate([v2, ones_one if c == 0 else ones_two], axis=1))
        l = ol[:, LANES:]
        o = ol[:, :LANES] * (1.0 / l)
        lse = jnp.where(head_a, ms[0], ms[1]) + jnp.log2(l)
        dst = pl.ds(c * n, n) if d == 1 else pl.ds(c * (n * d) + r, n, stride=d)
        o_scr[g][p, dst, :] = o
        lse_scr[g][p, dst, :] = lse

    pending = []
    for p in range(ATT_PAIRS):
        pos = 0
        for g, d in enumerate(ATT_DILATIONS):
            q_refs = in_refs[pos:pos + d]
            k_refs = in_refs[pos + d:pos + 2 * d]
            v_refs = in_refs[pos + 2 * d:pos + 3 * d]
            pos += 3 * d
            for r in range(d):
                m_len = seq // d
                vcls = v_refs[r][:, p * LANES:(p + 1) * LANES]
                va = lax.broadcasted_iota(jnp.int32, vcls.shape, 1) < ATT_HEAD_DIM
                vm_scr[g, 0, r * m_len:(r + 1) * m_len, :] = jnp.where(va, vcls, jnp.zeros_like(vcls))
                vm_scr[g, 1, r * m_len:(r + 1) * m_len, :] = jnp.where(va, jnp.zeros_like(vcls), vcls)
                for c in range(seq // d // n):
                    s_pair = scores(q_refs[r], k_refs[r], c, slice(p * LANES, (p + 1) * LANES))
                    pending.append((g, d, r, c, p, v_refs[r], s_pair))
                    if len(pending) > ATT_SKEW:
                        finish(*pending.pop(0))
    while pending:
        finish(*pending.pop(0))

    rows_per = 256

    def merge(i, carry):
        rows = pl.ds(pl.multiple_of(i * rows_per, rows_per), rows_per)
        for p in range(ATT_PAIRS):
            lanes = slice(p * LANES, (p + 1) * LANES)
            l1, l2, l3 = lse_scr[0][p, rows, :], lse_scr[1][p, rows, :], lse_scr[2][p, rows, :]
            mx = jnp.maximum(jnp.maximum(l1, l2), l3)
            w1, w2, w3 = jnp.exp2(l1 - mx), jnp.exp2(l2 - mx), jnp.exp2(l3 - mx)
            num = w1 * o_scr[0][p, rows, :] + w2 * o_scr[1][p, rows, :] + w3 * o_scr[2][p, rows, :]
            ob = num / (w1 + w2 + w3) * ag_ref[rows, lanes].astype(F32)
            out_ref[rows, lanes] = ob.astype(BF16)
        return carry

    lax.fori_loop(0, seq // rows_per, merge, 0)


def _attention(qkv, ag, B, S):
    width = ATT_PAIRS * LANES
    steps = ATT_DIM // width
    operands, in_specs = [], []
    for g, d in enumerate(ATT_DILATIONS):
        m_len = S // d
        for arr in qkv[g]:
            view = arr.reshape(B, m_len, d * ATT_DIM)
            for r in range(d):
                operands.append(view)
                in_specs.append(pl.BlockSpec((None, m_len, width),
                                             functools.partial(lambda b, hp, r: (b, 0, r * steps + hp), r=r)))
    operands.append(ag.reshape(B, S, ATT_DIM))
    nat = pl.BlockSpec((None, S, width), lambda b, hp: (b, 0, hp))
    in_specs.append(nat)
    out = pl.pallas_call(
        functools.partial(_attn_kernel, seq=S),
        grid=(B, steps),
        in_specs=in_specs,
        out_specs=nat,
        out_shape=jax.ShapeDtypeStruct((B, S, ATT_DIM), BF16),
        scratch_shapes=([pltpu.VMEM((ATT_PAIRS, S, LANES), F32)] * (2 * N_GROUPS)
                        + [pltpu.VMEM((N_GROUPS, 2, S, LANES), BF16)]),
        compiler_params=pltpu.CompilerParams(dimension_semantics=("parallel", "parallel"),
                                             vmem_limit_bytes=VMEM_LIMIT_BYTES),
        name="dilated_attn",
    )(*operands)
    return out.reshape(B * S, ATT_DIM)


def _out_kernel(x_ref, oa_ref, ob_ref, gates_ref, wa32_ref, wb32_ref, wo32_ref, fnw_ref, out_ref,
                wa_ref, wb_ref, wo_ref):
    @pl.when(pl.program_id(0) == 0)
    def _():
        wa_ref[...] = wa32_ref[...].astype(BF16)
        wb_ref[...] = wb32_ref[...].astype(BF16)
        wo_ref[...] = wo32_ref[...].astype(BF16)

    def branches(rows):
        return _dot(oa_ref[rows, :], wa_ref[...]), _dot(ob_ref[rows, :], wb_ref[...])

    def finish(rows, ya, yb):
        g_a = gates_ref[rows, :D_MODEL].astype(F32)
        g_b = gates_ref[rows, D_MODEL:].astype(F32)
        merged = (g_a * ya + g_b * yb).astype(BF16)
        y = x_ref[rows, :] + _dot(merged, wo_ref[...])
        ms = jnp.mean(y * y, axis=-1, keepdims=True)
        out_ref[rows, :] = y * lax.rsqrt(ms + NORM_EPS) * fnw_ref[0:1, :]

    pending = None
    for r in range(x_ref.shape[0] // OUT_SUB):
        rows = slice(r * OUT_SUB, (r + 1) * OUT_SUB)
        nxt = (rows, *branches(rows))
        if pending is not None:
            finish(*pending)
        pending = nxt
    finish(*pending)


def _output(x2, oa, ob, gates, wa, wb, wo, fnw):
    T = x2.shape[0]
    tm = OUT_TM
    row = lambda n: pl.BlockSpec((tm, n), lambda i: (i, 0))
    const = lambda shape: pl.BlockSpec(shape, lambda i: (0, 0))
    return pl.pallas_call(
        _out_kernel,
        grid=(T // tm,),
        in_specs=[row(D_MODEL), row(HG_DIM), row(ATT_DIM), row(2 * D_MODEL),
                  const((HG_DIM, D_MODEL)), const((ATT_DIM, D_MODEL)), const((D_MODEL, D_MODEL)),
                  const((PAR_ROWS, D_MODEL))],
        out_specs=row(D_MODEL),
        out_shape=jax.ShapeDtypeStruct((T, D_MODEL), F32),
        scratch_shapes=[pltpu.VMEM((HG_DIM, D_MODEL), BF16), pltpu.VMEM((ATT_DIM, D_MODEL), BF16),
                        pltpu.VMEM((D_MODEL, D_MODEL), BF16)],
        compiler_params=pltpu.CompilerParams(dimension_semantics=("arbitrary",),
                                             vmem_limit_bytes=VMEM_LIMIT_BYTES),
        name="merge_out",
    )(x2, oa, ob, gates, wa, wb, wo, fnw)


def kernel(x, positions, norm_w, w_in, lb_logits, hgrn_norm_w, w_branch_a, w_branch_b, w_out, final_norm_w):
    B, S, D = x.shape
    assert D == D_MODEL and norm_w.shape[0] == 1 and lb_logits.shape == (2, HG_DIM)
    assert S % (ATT_STEPS * max(ATT_DILATIONS)) == 0 and S % HG_TS == 0 and (B * S) % OUT_TM == 0
    T = B * S
    x2 = x.reshape(T, D)
    pos_b = jnp.repeat(positions.reshape(T // ROPE_PACK, ROPE_PACK), ROPE_FREQS, axis=1)
    inv_freq = ROPE_THETA ** (-jnp.arange(0, ATT_HEAD_DIM, 2, dtype=F32) / ATT_HEAD_DIM)
    invf = jnp.tile(inv_freq, LANES // ROPE_FREQS).reshape(1, LANES)

    outs = _inproj(x2, pos_b, norm_w.reshape(1, D), w_in[0].astype(F32), lb_logits.astype(F32), invf)
    qa, ka, lf, va, ga = outs[:5]
    qkv = [tuple(outs[5 + 3 * g:8 + 3 * g]) for g in range(N_GROUPS)]
    ag, gates = outs[14], outs[15]

    a_mat = jnp.asarray(_hgrn_sum_matrix(), BF16)
    hgrn_w = jnp.broadcast_to(jnp.tile(hgrn_norm_w.reshape(1, HG_VAL).astype(F32), (1, HG_HEADS)), (PAR_ROWS, HG_DIM))
    oa = _hgrn(qa, ka, lf, va, ga, a_mat, hgrn_w, B, S)
    ob = _attention(qkv, ag, B, S)
    out = _output(x2, oa, ob, gates, w_branch_a[0].astype(F32), w_branch_b[0].astype(F32),
                  w_out[0].astype(F32), jnp.broadcast_to(final_norm_w.reshape(1, D).astype(F32), (PAR_ROWS, D)))
    return out.reshape(B, S, D)
```

```python
import functools

import numpy as np
import jax
import jax.numpy as jnp
from jax import lax
from jax.experimental import pallas as pl
from jax.experimental.pallas import tpu as pltpu

D_MODEL = 1024
HG_HEADS = 8
HG_KEY = 128
HG_VAL = 128
HG_DIM = HG_HEADS * HG_KEY
ATT_DILATIONS = (1, 4, 16)
ATT_STEPS = 128
N_GROUPS = 3
ATT_HEADS = 8
ATT_HEAD_DIM = 64
ATT_DIM = ATT_HEADS * ATT_HEAD_DIM
ROPE_THETA = 10000.0
ROPE_FREQS = ATT_HEAD_DIM // 2
ROPE_PACK = 4
NORM_EPS = 1e-6
LOG2_E = 1.4426950408889634
C_HQ, C_HF, C_HI, C_HG = 0, 1024, 2048, 3072
C_AQ, C_AK, C_AV = 4096, 4096 + 1536, 4096 + 3072
C_AG = 8704
C_GATES = 9216
IN_COLS = 11264

LANES = 128
VMEM_LIMIT_BYTES = 56 * 1024 * 1024

PAR_ROWS = 8

IN_TM = 256
IN_SEG = 512
IN_W_CHUNKS = IN_COLS // IN_SEG
IN_W_PER_STEP = 2
IN_W_STEPS = IN_W_CHUNKS // IN_W_PER_STEP
PERM_STRIDE = 4
HG_CHUNK = 64
HG_TS = 1024
HG_LEVEL_SIZES = tuple(HG_CHUNK >> (l + 1) for l in range(HG_CHUNK.bit_length() - 1))
HG_ROW_TILE = 8
HG_VPU_LEVEL = 1
HG_GROUP = 2
HG_SKEW = 1
OUT_TM = 1024
OUT_SUB = 256
ATT_PAIRS = 2
ATT_SKEW = 3
NEG_BIG = -1e30

F32 = jnp.float32
BF16 = jnp.bfloat16


def _dot(a, b):
    return jnp.dot(a, b, preferred_element_type=F32)


def _dot_nt(a, b):
    return lax.dot_general(a, b, (((1,), (1,)), ((), ())), preferred_element_type=F32)


def _dot_tn(a, b):
    return lax.dot_general(a, b, (((0,), (0,)), ((), ())), preferred_element_type=F32)


def _sigmoid(z):
    return 0.5 + 0.5 * jnp.tanh(0.5 * z)


def _inproj_kernel(xn_ref, posn_ref, nw_ref, wchunk_ref, lbl_ref, invf_ref,
                   qa_ref, ka_ref, lf_ref, va_ref, ga_ref,
                   q1_ref, k1_ref, v1_ref, q2_ref, k2_ref, v2_ref, q3_ref, k3_ref, v3_ref,
                   ag_ref, gates_ref, perm_scr, h_scr, rope_scr, w_scr):
    tm = xn_ref.shape[0]
    i = pl.program_id(0)
    step = i - IN_W_STEPS
    slot = step & 1

    @pl.when(step < 0)
    def _():
        for k in range(IN_W_PER_STEP):
            w_scr[i * IN_W_PER_STEP + k] = wchunk_ref[:, k * IN_SEG:(k + 1) * IN_SEG].astype(BF16)
    lane = lax.broadcasted_iota(jnp.int32, (tm, LANES), 1)
    first = (lane & (ATT_HEAD_DIM - 1)) < (ATT_HEAD_DIM // 2)

    def prepare(x_ref, pos_ref, dst):
        x = x_ref[...]
        ms = jnp.mean(x * x, axis=-1, keepdims=True)
        h_scr[dst] = (x * lax.rsqrt(ms + NORM_EPS) * nw_ref[...]).astype(BF16)

        ang = pos_ref[...].astype(F32) * invf_ref[...]
        lane_c = lax.broadcasted_iota(jnp.int32, ang.shape, 1)
        first_c = (lane_c & (ATT_HEAD_DIM - 1)) < (ATT_HEAD_DIM // 2)
        q_factor = ATT_HEAD_DIM ** -0.5 * LOG2_E
        for table, compact in ((0, jnp.cos(ang)), (1, jnp.sin(ang))):
            for g in range(ROPE_PACK):
                u = compact if g == 0 else pltpu.roll(compact, LANES - g * ROPE_FREQS, 1)
                u = jnp.where(lane_c < ROPE_FREQS, u, pltpu.roll(u, ROPE_FREQS, 1))
                u = jnp.where(lane_c < 2 * ROPE_FREQS, u, pltpu.roll(u, 2 * ROPE_FREQS, 1))
                if table == 1:
                    u = jnp.where(first_c, -u, u)
                rows = pl.ds(g, tm // ROPE_PACK, stride=ROPE_PACK)
                rope_scr[dst, table, rows, :] = u
                rope_scr[dst, table + 2, rows, :] = u * q_factor

    @pl.when(step == -1)
    def _():
        prepare(xn_ref, posn_ref, 0)

    def proj(c0, n):
        assert n == IN_SEG and c0 % IN_SEG == 0
        return _dot(h_scr[slot], w_scr[c0 // IN_SEG])

    lbl = lbl_ref[...]
    lmax = jnp.max(lbl, axis=0, keepdims=True)
    lexp = jnp.exp(lbl - lmax)
    lb = lexp[0:1, :] / jnp.sum(lexp, axis=0, keepdims=True)

    def rope(zs, tables):
        rot = jnp.where(first, pltpu.roll(zs, LANES - ATT_HEAD_DIM // 2, 1), pltpu.roll(zs, ATT_HEAD_DIM // 2, 1))
        return zs * rope_scr[slot, tables] + rot * rope_scr[slot, tables + 1]

    def silu_to(ref, sl):
        def epi(z):
            hz = 0.5 * z
            ref[:, sl] = (hz + hz * jnp.tanh(hz)).astype(BF16)
        return epi

    def cast_to(ref, sl):
        def epi(z):
            ref[:, sl] = z.astype(BF16)
        return epi

    def sigmoid_to(ref, sl):
        def epi(z):
            ref[:, sl] = _sigmoid(z).astype(BF16)
        return epi

    def forget_to(sl):
        def epi(z):
            c = 0.5 - 0.5 * lb[:, sl]
            ct = c * jnp.tanh(0.5 * z)
            lf_ref[:, sl] = jnp.log2((lb[:, sl] + c) + ct)
            ka_ref[:, sl] = (c - ct).astype(BF16)
        return epi

    def attn_to(ref, d, slab, tables):
        def epi(z):
            for s in range(ATT_DIM // LANES):
                ls = slice(s * LANES, (s + 1) * LANES)
                val = z[:, ls]
                if tables is not None:
                    val = rope(val, tables)
                if d == 1:
                    ref[:, ls] = val.astype(BF16)
                    continue
                perm_scr[0, slab] = val
                if d > PERM_STRIDE:
                    for r1 in range(PERM_STRIDE):
                        perm_scr[1, slab, r1 * (tm // PERM_STRIDE):(r1 + 1) * (tm // PERM_STRIDE), :] = (
                            perm_scr[0, slab, pl.ds(r1, tm // PERM_STRIDE, stride=PERM_STRIDE), :])
                for r in range(d):
                    cols = slice(r * ATT_DIM + s * LANES, r * ATT_DIM + (s + 1) * LANES)
                    if d > PERM_STRIDE:
                        r1, r2 = r % PERM_STRIDE, r // PERM_STRIDE
                        rows = pl.ds(r1 * (tm // PERM_STRIDE) + r2, tm // d, stride=PERM_STRIDE)
                        ref[:, cols] = perm_scr[1, slab, rows, :].astype(BF16)
                    else:
                        ref[:, cols] = perm_scr[0, slab, pl.ds(r, tm // d, stride=d), :].astype(BF16)
        return epi

    halves = [slice(j * IN_SEG, (j + 1) * IN_SEG) for j in range(HG_DIM // IN_SEG)]
    segments = []
    for j, sl in enumerate(halves):
        segments += [(C_HQ + j * IN_SEG, IN_SEG, silu_to(qa_ref, sl)), (C_HF + j * IN_SEG, IN_SEG, forget_to(sl)),
                     (C_HG + j * IN_SEG, IN_SEG, silu_to(ga_ref, sl))]
    for j in range(2 * D_MODEL // IN_SEG):
        segments.append((C_GATES + j * IN_SEG, IN_SEG,
                         sigmoid_to(gates_ref, slice(j * IN_SEG, (j + 1) * IN_SEG))))
    segments.append((C_AG, ATT_DIM, silu_to(ag_ref, slice(0, ATT_DIM))))
    att_refs = ((q1_ref, k1_ref, v1_ref), (q2_ref, k2_ref, v2_ref), (q3_ref, k3_ref, v3_ref))
    for g in reversed(range(N_GROUPS)):
        d = ATT_DILATIONS[g]
        segments += [(C_AQ + g * ATT_DIM, ATT_DIM, attn_to(att_refs[g][0], d, 0, 2)),
                     (C_AK + g * ATT_DIM, ATT_DIM, attn_to(att_refs[g][1], d, 1, 0)),
                     (C_AV + g * ATT_DIM, ATT_DIM, attn_to(att_refs[g][2], d, 2, None))]
    for j, sl in enumerate(halves):
        segments.append((C_HI + j * IN_SEG, IN_SEG, cast_to(va_ref, sl)))

    @pl.when(step >= 0)
    def _():
        z = proj(*segments[0][:2])
        prepare(xn_ref, posn_ref, 1 - slot)
        for j, (_, _, epilogue) in enumerate(segments):
            z_next = proj(*segments[j + 1][:2]) if j + 1 < len(segments) else None
            epilogue(z)
            z = z_next


def _inproj(x2, pos_b, norm_w, w_in, lb_logits, invf):
    T = x2.shape[0]
    tm = IN_TM
    steps = T // tm
    tile = lambda i: jnp.clip(i - IN_W_STEPS, 0, steps - 1)
    row = lambda n: pl.BlockSpec((tm, n), lambda i: (tile(i), 0))
    const = lambda shape: pl.BlockSpec(shape, lambda i: (0, 0))
    nxt = lambda r, n: pl.BlockSpec((r, n), lambda i: (tile(i + 1), 0))
    out_shapes = ([jax.ShapeDtypeStruct((T, HG_DIM), BF16),
                   jax.ShapeDtypeStruct((T, HG_DIM), BF16),
                   jax.ShapeDtypeStruct((T, HG_DIM), F32),
                   jax.ShapeDtypeStruct((T, HG_DIM), BF16),
                   jax.ShapeDtypeStruct((T, HG_DIM), BF16)]
                  + [jax.ShapeDtypeStruct((T // d, d * ATT_DIM), BF16) for d in ATT_DILATIONS for _ in range(3)]
                  + [jax.ShapeDtypeStruct((T, ATT_DIM), BF16),
                     jax.ShapeDtypeStruct((T, 2 * D_MODEL), BF16)])
    out_specs = ([row(HG_DIM)] * 5
                 + [pl.BlockSpec((tm // d, d * ATT_DIM), lambda i: (tile(i), 0))
                    for d in ATT_DILATIONS for _ in range(3)]
                 + [row(ATT_DIM), row(2 * D_MODEL)])
    return pl.pallas_call(
        _inproj_kernel,
        grid=(IN_W_STEPS + steps,),
        in_specs=[nxt(tm, D_MODEL), nxt(tm // ROPE_PACK, LANES), const((1, D_MODEL)),
                  pl.BlockSpec((D_MODEL, IN_W_PER_STEP * IN_SEG), lambda i: (0, jnp.minimum(i, IN_W_STEPS - 1))),
                  const((2, HG_DIM)), const((1, LANES))],
        out_specs=out_specs,
        out_shape=out_shapes,
        scratch_shapes=[pltpu.VMEM((2, 3, tm, LANES), F32), pltpu.VMEM((2, tm, D_MODEL), BF16),
                        pltpu.VMEM((2, 4, tm, LANES), F32), pltpu.VMEM((IN_W_CHUNKS, D_MODEL, IN_SEG), BF16)],
        compiler_params=pltpu.CompilerParams(dimension_semantics=("arbitrary",),
                                             vmem_limit_bytes=VMEM_LIMIT_BYTES),
        name="inproj",
    )(x2, pos_b, norm_w, w_in, lb_logits, invf)


def _hgrn_sum_matrix():
    C = HG_CHUNK
    fine = [m for m in HG_LEVEL_SIZES if HG_VPU_LEVEL < m < HG_ROW_TILE]
    A = np.zeros(((1 + len(fine)) * C, C), np.float32)
    for t in range(C):
        A[t, :t + 1] = 1.0
    for i, m in enumerate(fine):
        for t in range(C):
            start = (t // m) * m
            if t & m:
                A[(1 + i) * C + t, start:t + 1] = 1.0
            else:
                A[(1 + i) * C + t, t + 1:start + m] = 1.0
    return np.concatenate([A, A], axis=1)


def _hgrn_kernel(q_ref, k_ref, lf_ref, v_ref, g_ref, a_ref, nw_ref, o_ref, state_scr, e_scr):
    C = HG_CHUNK
    n_lev = len(HG_LEVEL_SIZES)
    n_chunks = HG_TS // C

    @pl.when(pl.program_id(1) == 0)
    def _():
        state_scr[...] = jnp.zeros_like(state_scr)

    t_idx = lax.broadcasted_iota(jnp.int32, (C, HG_GROUP * C), 0)
    s_idx = lax.broadcasted_iota(jnp.int32, (C, HG_GROUP * C), 1) & (C - 1)
    diag_mask = t_idx == s_idx
    level_masks = []
    for m in HG_LEVEL_SIZES:
        sh = (2 * m).bit_length() - 1
        level_masks.append(((t_idx & m) != 0) & ((s_idx & m) == 0) & ((t_idx >> sh) == (s_idx >> sh)))
    row_full = lax.broadcasted_iota(jnp.int32, (C, HG_DIM), 0)
    row_idx = lax.broadcasted_iota(jnp.int32, (C, HG_KEY), 0)
    right_rows = [(row_idx & m) != 0 for m in HG_LEVEL_SIZES]

    def exponents(ci):
        slot = ci % 2
        lf = lf_ref[ci * C:(ci + 1) * C, :]
        hi = lf.astype(BF16)
        lo = (lf - hi.astype(F32)).astype(BF16)
        sums = _dot(a_ref[...], jnp.concatenate([hi, lo], axis=0))
        b = sums[0:C, :]

        def put(i, arg):
            e_scr[slot, i * C:(i + 1) * C, :] = jnp.exp2(arg)

        i_fine = 1
        for i, m in enumerate(HG_LEVEL_SIZES):
            if m >= HG_ROW_TILE:
                parts = []
                for j in range(C // (2 * m)):
                    ref = jnp.broadcast_to(b[j * 2 * m + m - 1:j * 2 * m + m, :], (m, HG_DIM))
                    parts += [ref - b[j * 2 * m:j * 2 * m + m, :], b[j * 2 * m + m:(j + 1) * 2 * m, :] - ref]
                put(i, jnp.concatenate(parts, axis=0))
            elif m > HG_VPU_LEVEL:
                put(i, sums[i_fine * C:(i_fine + 1) * C, :])
                i_fine += 1
            elif m == 2:
                nxt = pltpu.roll(lf, C - 1, 0)
                prv = pltpu.roll(lf, 1, 0)
                pos4 = row_full & 3
                put(i, jnp.where(pos4 == 0, nxt, jnp.where(pos4 == 1, 0.0, jnp.where(pos4 == 2, lf, lf + prv))))
            else:
                put(i, jnp.where((row_full & 1) == 1, lf, 0.0))
        put(n_lev, b)
        put(n_lev + 1, b[C - 1:C, :] - b)

    def block_diag(xs):
        zero = jnp.zeros_like(xs[0])
        return jnp.concatenate([jnp.concatenate([x if b == a else zero for b in range(len(xs))], axis=1)
                                for a, x in enumerate(xs)], axis=0)

    def pair_scores(ci, hp):
        slot = ci % 2
        rows = slice(ci * C, (ci + 1) * C)
        lanes = [slice(h * HG_KEY, (h + 1) * HG_KEY) for h in range(HG_GROUP * hp, HG_GROUP * (hp + 1))]
        q_bf =[q_ref[rows, hl] for hl in lanes]
        k_bf = [k_ref[rows, hl] for hl in lanes]
        q = [x.astype(F32) for x in q_bf]
        k = [x.astype(F32) for x in k_bf]
        scores = jnp.where(diag_mask, _dot_nt(jnp.concatenate(q_bf, axis=1), block_diag(k_bf)), 0.0)
        for i, m in enumerate(HG_LEVEL_SIZES):
            zs = []
            for a, hl in enumerate(lanes):
                if m >= HG_ROW_TILE:
                    w = jnp.concatenate([(q[a] if j % 2 else k[a])[j * m:(j + 1) * m, :] for j in range(C // m)],
                                        axis=0)
                else:
                    w = jnp.where(right_rows[i], q[a], k[a])
                zs.append((w * e_scr[slot, i * C:(i + 1) * C, hl]).astype(BF16))
            scores = jnp.where(level_masks[i], _dot_nt(jnp.concatenate(zs, axis=1), block_diag(zs)), scores)
        q_dec = [(q[a] * e_scr[slot, n_lev * C:(n_lev + 1) * C, hl]).astype(BF16) for a, hl in enumerate(lanes)]
        k_dec = [(k[a] * e_scr[slot, (n_lev + 1) * C:(n_lev + 2) * C, hl]).astype(BF16) for a, hl in enumerate(lanes)]
        return scores, q_dec, k_dec

    def finish(ci, hp, stage1):
        scores, q_dec, k_dec = stage1
        slot = ci % 2
        rows = slice(ci * C, (ci + 1) * C)
        heads = tuple(range(HG_GROUP * hp, HG_GROUP * (hp + 1)))
        lanes = [slice(h * HG_KEY, (h + 1) * HG_KEY) for h in heads]
        v_bf = [v_ref[rows, hl] for hl in lanes]
        o_pair = _dot(scores.astype(BF16), block_diag(v_bf))
        for a, (h, hl) in enumerate(zip(heads, lanes)):
            e_last = e_scr[slot, (n_lev + 1) * C - 1:(n_lev + 1) * C, hl]
            st = state_scr[h]
            o = o_pair[:, a * HG_VAL:(a + 1) * HG_VAL] + _dot_nt(q_dec[a], st.astype(BF16))
            state_scr[h] = st * e_last + _dot_tn(v_bf[a], k_dec[a])
            ms = jnp.mean(o * o, axis=-1, keepdims=True)
            y = o * lax.rsqrt(ms + NORM_EPS) * nw_ref[0:1, 0:HG_VAL] * g_ref[rows, hl].astype(F32)
            o_ref[rows, hl] = y.astype(BF16)

    n_pairs = HG_HEADS // HG_GROUP
    exponents(0)
    pending = []
    for ci in range(n_chunks):
        for hp in range(n_pairs):
            if hp == n_pairs // 2 and ci + 1 < n_chunks:
                exponents(ci + 1)
            pending.append((ci, hp, pair_scores(ci, hp)))
            if len(pending) > HG_SKEW:
                finish(*pending.pop(0))
    while pending:
        finish(*pending.pop(0))


def _hgrn(qa, ka, lf, va, ga, a_mat, hgrn_w, B, S):
    ts = HG_TS
    steps = S // ts
    blk = pl.BlockSpec((ts, HG_DIM), lambda b, i: (b * steps + i, 0))
    rows_e = (len(HG_LEVEL_SIZES) + 2) * HG_CHUNK
    return pl.pallas_call(
        _hgrn_kernel,
        grid=(B, steps),
        in_specs=[blk, blk, blk, blk, blk,
                  pl.BlockSpec(a_mat.shape, lambda b, i: (0, 0)),
                  pl.BlockSpec((PAR_ROWS, HG_DIM), lambda b, i: (0, 0))],
        out_specs=blk,
        out_shape=jax.ShapeDtypeStruct((B * S, HG_DIM), BF16),
        scratch_shapes=[pltpu.VMEM((HG_HEADS, HG_VAL, HG_KEY), F32),
                        pltpu.VMEM((2, rows_e, HG_DIM), F32)],
        compiler_params=pltpu.CompilerParams(dimension_semantics=("parallel", "arbitrary"),
                                             vmem_limit_bytes=VMEM_LIMIT_BYTES),
        name="hgrn2",
    )(qa, ka, lf, va, ga, a_mat, hgrn_w)


def _attn_kernel(*refs, seq):
    n_in = 3 * sum(ATT_DILATIONS) + 1
    in_refs = refs[:n_in]
    ag_ref = in_refs[-1]
    out_ref = refs[n_in]
    o_scr = refs[n_in + 1:n_in + 1 + N_GROUPS]
    lse_scr = refs[n_in + 1 + N_GROUPS:n_in + 1 + 2 * N_GROUPS]
    n = ATT_STEPS

    lane = lax.broadcasted_iota(jnp.int32, (n, LANES), 1)
    head_a = lane < ATT_HEAD_DIM
    qi = lax.broadcasted_iota(jnp.int32, (n, 2 * n), 0)
    kj = lax.broadcasted_iota(jnp.int32, (n, 2 * n), 1)
    band_ok = (kj >= qi) & (kj <= qi + n)
    first_ok = (lax.broadcasted_iota(jnp.int32, (n, n), 1)
                <= lax.broadcasted_iota(jnp.int32, (n, n), 0))

    def scores(q_ref, k_ref, c, pl_):
        q = q_ref[c * n:(c + 1) * n, pl_]
        kk = k_ref[max(c - 1, 0) * n:(c + 1) * n, pl_]
        zero_q = jnp.zeros_like(q)
        return [_dot_nt(jnp.where(head_a, q, zero_q), kk), _dot_nt(jnp.where(head_a, zero_q, q), kk)]

    def finish(g, d, r, c, p, v_ref, s_pair):
        ok = first_ok if c == 0 else band_ok
        vv = v_ref[max(c - 1, 0) * n:(c + 1) * n, p * LANES:(p + 1) * LANES]
        ps, ms = [], []
        for s in s_pair:
            s = jnp.where(ok, s, NEG_BIG)
            sm = s if c == 0 else jnp.maximum(s[:, :n], s[:, n:])
            m = jnp.max(sm, axis=-1, keepdims=True)
            ps.append(jnp.exp2(s - m).astype(BF16))
            ms.append(m)
        ol = _dot(jnp.concatenate(ps, axis=0), jnp.concatenate([vv, jnp.ones_like(vv)], axis=1))
        l = jnp.where(head_a, ol[:n, LANES:], ol[n:, LANES:])
        o = jnp.where(head_a, ol[:n, :LANES], ol[n:, :LANES]) * (1.0 / l)
        lse = jnp.where(head_a, ms[0], ms[1]) + jnp.log2(l)
        dst = pl.ds(c * n, n) if d == 1 else pl.ds(c * (n * d) + r, n, stride=d)
        o_scr[g][p, dst, :] = o
        lse_scr[g][p, dst, :] = lse

    rows_per = 256
    last_block = (N_GROUPS - 1, ATT_DILATIONS[-1] - 1, seq // ATT_DILATIONS[-1] // n - 1)

    def merge_pair(p):
        lanes = slice(p * LANES, (p + 1) * LANES)
        for i in range(seq // rows_per):
            rows = slice(i * rows_per, (i + 1) * rows_per)
            l1, l2, l3 = lse_scr[0][p, rows, :], lse_scr[1][p, rows, :], lse_scr[2][p, rows, :]
            mx = jnp.maximum(jnp.maximum(l1, l2), l3)
            w1, w2, w3 = jnp.exp2(l1 - mx), jnp.exp2(l2 - mx), jnp.exp2(l3 - mx)
            num = w1 * o_scr[0][p, rows, :] + w2 * o_scr[1][p, rows, :] + w3 * o_scr[2][p, rows, :]
            ob = num / (w1 + w2 + w3) * ag_ref[rows, lanes].astype(F32)
            out_ref[rows, lanes] = ob.astype(BF16)

    def retire(item):
        finish(*item)
        if (item[0], item[2], item[3]) == last_block:
            merge_pair(item[4])

    pending = []
    for p in range(ATT_PAIRS):
        pos = 0
        for g, d in enumerate(ATT_DILATIONS):
            q_refs = in_refs[pos:pos + d]
            k_refs = in_refs[pos + d:pos + 2 * d]
            v_refs = in_refs[pos + 2 * d:pos + 3 * d]
            pos += 3 * d
            for r in range(d):
                for c in range(seq // d // n):
                    s_pair = scores(q_refs[r], k_refs[r], c, slice(p * LANES, (p + 1) * LANES))
                    pending.append((g, d, r, c, p, v_refs[r], s_pair))
                    if len(pending) > ATT_SKEW:
                        retire(pending.pop(0))
    while pending:
        retire(pending.pop(0))


def _attention(qkv, ag, B, S):
    width = ATT_PAIRS * LANES
    steps = ATT_DIM // width
    operands, in_specs = [], []
    for g, d in enumerate(ATT_DILATIONS):
        m_len = S // d
        for arr in qkv[g]:
            view = arr.reshape(B, m_len, d * ATT_DIM)
            for r in range(d):
                operands.append(view)
                in_specs.append(pl.BlockSpec((None, m_len, width),
                                             functools.partial(lambda b, hp, r: (b, 0, r * steps + hp), r=r)))
    operands.append(ag.reshape(B, S, ATT_DIM))
    nat = pl.BlockSpec((None, S, width), lambda b, hp: (b, 0, hp))
    in_specs.append(nat)
    out = pl.pallas_call(
        functools.partial(_attn_kernel, seq=S),
        grid=(B, steps),
        in_specs=in_specs,
        out_specs=nat,
        out_shape=jax.ShapeDtypeStruct((B, S, ATT_DIM), BF16),
        scratch_shapes=[pltpu.VMEM((ATT_PAIRS, S, LANES), F32)] * (2 * N_GROUPS),
        compiler_params=pltpu.CompilerParams(dimension_semantics=("parallel", "parallel"),
                                             vmem_limit_bytes=VMEM_LIMIT_BYTES),
        name="dilated_attn",
    )(*operands)
    return out.reshape(B * S, ATT_DIM)


def _out_kernel(x_ref, oa_ref, ob_ref, gates_ref, wa32_ref, wb32_ref, wo32_ref, fnw_ref, out_ref,
                wa_ref, wb_ref, wo_ref):
    @pl.when(pl.program_id(0) == 0)
    def _():
        wa_ref[...] = wa32_ref[...].astype(BF16)
        wb_ref[...] = wb32_ref[...].astype(BF16)
        wo_ref[...] = wo32_ref[...].astype(BF16)

    def branches(rows):
        return _dot(oa_ref[rows, :], wa_ref[...]), _dot(ob_ref[rows, :], wb_ref[...])

    def finish(rows, ya, yb):
        g_a = gates_ref[rows, :D_MODEL].astype(F32)
        g_b = gates_ref[rows, D_MODEL:].astype(F32)
        merged = (g_a * ya + g_b * yb).astype(BF16)
        y = x_ref[rows, :] + _dot(merged, wo_ref[...])
        ms = jnp.mean(y * y, axis=-1, keepdims=True)
        out_ref[rows, :] = y * lax.rsqrt(ms + NORM_EPS) * fnw_ref[0:1, :]

    pending = None
    for r in range(x_ref.shape[0] // OUT_SUB):
        rows = slice(r * OUT_SUB, (r + 1) * OUT_SUB)
        nxt = (rows, *branches(rows))
        if pending is not None:
            finish(*pending)
        pending = nxt
    finish(*pending)


def _output(x2, oa, ob, gates, wa, wb, wo, fnw):
    T = x2.shape[0]
    tm = OUT_TM
    row = lambda n: pl.BlockSpec((tm, n), lambda i: (i, 0))
    const = lambda shape: pl.BlockSpec(shape, lambda i: (0, 0))
    return pl.pallas_call(
        _out_kernel,
        grid=(T // tm,),
        in_specs=[row(D_MODEL), row(HG_DIM), row(ATT_DIM), row(2 * D_MODEL),
                  const((HG_DIM, D_MODEL)), const((ATT_DIM, D_MODEL)), const((D_MODEL, D_MODEL)),
                  const((PAR_ROWS, D_MODEL))],
        out_specs=row(D_MODEL),
        out_shape=jax.ShapeDtypeStruct((T, D_MODEL), F32),
        scratch_shapes=[pltpu.VMEM((HG_DIM, D_MODEL), BF16), pltpu.VMEM((ATT_DIM, D_MODEL), BF16),
                        pltpu.VMEM((D_MODEL, D_MODEL), BF16)],
        compiler_params=pltpu.CompilerParams(dimension_semantics=("arbitrary",),
                                             vmem_limit_bytes=VMEM_LIMIT_BYTES),
        name="merge_out",
    )(x2, oa, ob, gates, wa, wb, wo, fnw)


def kernel(x, positions, norm_w, w_in, lb_logits, hgrn_norm_w, w_branch_a, w_branch_b, w_out, final_norm_w):
    B, S, D = x.shape
    assert D == D_MODEL and norm_w.shape[0] == 1 and lb_logits.shape == (2, HG_DIM)
    assert S % (ATT_STEPS * max(ATT_DILATIONS)) == 0 and S % HG_TS == 0 and (B * S) % OUT_TM == 0
    T = B * S
    x2 = x.reshape(T, D)
    pos_b = jnp.repeat(positions.reshape(T // ROPE_PACK, ROPE_PACK), ROPE_FREQS, axis=1)
    inv_freq = ROPE_THETA ** (-jnp.arange(0, ATT_HEAD_DIM, 2, dtype=F32) / ATT_HEAD_DIM)
    invf = jnp.tile(inv_freq, LANES // ROPE_FREQS).reshape(1, LANES)

    outs = _inproj(x2, pos_b, norm_w.reshape(1, D), w_in[0].astype(F32), lb_logits.astype(F32), invf)
    qa, ka, lf, va, ga = outs[:5]
    qkv = [tuple(outs[5 + 3 * g:8 + 3 * g]) for g in range(N_GROUPS)]
    ag, gates = outs[14], outs[15]

    a_mat = jnp.asarray(_hgrn_sum_matrix(), BF16)
    hgrn_w = jnp.broadcast_to(jnp.tile(hgrn_norm_w.reshape(1, HG_VAL).astype(F32), (1, HG_HEADS)), (PAR_ROWS, HG_DIM))
    oa = _hgrn(qa, ka, lf, va, ga, a_mat, hgrn_w, B, S)
    ob = _attention(qkv, ag, B, S)
    out = _output(x2, oa, ob, gates, w_branch_a[0].astype(F32), w_branch_b[0].astype(F32),
                  w_out[0].astype(F32), jnp.broadcast_to(final_norm_w.reshape(1, D).astype(F32), (PAR_ROWS, D)))
    return out.reshape(B, S, D)
```
